```python
import jax
import jax.numpy as jnp
from jax import lax
import numpy as np

D_MODEL = 2048
BATCH = 1
SEQ = 8192
DEPTH = 1

HEAD_DIM = 128
SB_HEADS = 8
NSA_HEADS = 8
NSA_KV_GROUPS = 2
SB_WIDTH = SB_HEADS * HEAD_DIM
NSA_WIDTH = NSA_HEADS * HEAD_DIM
MIX_WIDTH = SB_WIDTH + NSA_WIDTH
KV_WIDTH = NSA_KV_GROUPS * HEAD_DIM
N_GATES = 3
CMP_BLOCK = 32
CMP_STRIDE = 16
CMP_HIDDEN = 256
SEL_BLOCK = 64
SEL_TOPK = 16
WINDOW = 512
Q_BLOCK = 128
MEM_TOKENS = 256
MEM_HEADS = 4
D_FF = 5504
ROPE_THETA = 10000.0
LN_EPS = 1e-5
RMS_EPS = 1e-6
MASK_FILL = -1e30
DEEPNORM_ALPHA = (2 * DEPTH) ** 0.25
DEEPNORM_BETA = (8 * DEPTH) ** -0.25
IN_SPLITS = (SB_WIDTH, SB_WIDTH, SB_WIDTH, NSA_WIDTH, KV_WIDTH, KV_WIDTH, KV_WIDTH, KV_WIDTH, KV_WIDTH, KV_WIDTH, NSA_HEADS * N_GATES)
VALUE_SEGMENTS = (2, 5, 7, 9)
IN_COLS = sum(IN_SPLITS)

kernel_name = 'hybrid_sb_nsa_macaron_deepnorm'


def layer_norm(x, g, b):
    xf = x.astype(jnp.float32)
    mu = jnp.mean(xf, axis=-1, keepdims=True)
    var = jnp.mean(jnp.square(xf - mu), axis=-1, keepdims=True)
    return ((xf - mu) * lax.rsqrt(var + LN_EPS) * g + b).astype(x.dtype)


def swiglu(x, w_gate, w_up, w_down):
    return (jax.nn.silu(x @ w_gate) * (x @ w_up)) @ w_down


def split_heads(t, n_heads):
    B, T, _ = t.shape
    return t.reshape(B, T, n_heads, HEAD_DIM).transpose(0, 2, 1, 3)


def rope(x, pos):
    half = HEAD_DIM // 2
    inv_freq = ROPE_THETA ** (-jnp.arange(half, dtype=jnp.float32) / half)
    ang = pos.astype(jnp.float32)[:, None] * inv_freq[None, :]
    cos = jnp.cos(ang).astype(x.dtype)
    sin = jnp.sin(ang).astype(x.dtype)
    x1, x2 = x[..., :half], x[..., half:]
    return jnp.concatenate([x1 * cos - x2 * sin, x1 * sin + x2 * cos], axis=-1)


def masked_softmax(s, mask, axes):
    s = jnp.where(mask, s.astype(jnp.float32), MASK_FILL)
    m = jnp.max(s, axis=axes, keepdims=True)
    p = jnp.where(mask, jnp.exp(s - m), 0.0)
    return p / jnp.maximum(jnp.sum(p, axis=axes, keepdims=True), 1e-30)


def stick_breaking_attention(q, k, v):
    B, H, T, Dh = q.shape
    nb = T // Q_BLOCK
    scale = Dh ** -0.5
    key_pos = jnp.arange(T)
    qb = jnp.moveaxis(q.reshape(B, H, nb, Q_BLOCK, Dh), 2, 0)

    def block(args):
        i, q_i = args
        q_pos = i * Q_BLOCK + jnp.arange(Q_BLOCK)
        z = jnp.einsum('bhqd,bhkd->bhqk', q_i, k).astype(jnp.float32) * scale
        strict = key_pos[None, :] < q_pos[:, None]
        log_beta = jax.nn.log_sigmoid(z)
        log_1m_beta = jnp.where(strict, jax.nn.log_sigmoid(-z), 0.0)
        log_stay = lax.cumsum(log_1m_beta, axis=3, reverse=True) - log_1m_beta
        w = jnp.where(strict, jnp.exp(log_beta + log_stay), 0.0)
        return jnp.einsum('bhqk,bhkd->bhqd', w.astype(v.dtype), v)

    out = lax.map(block, (jnp.arange(nb), qb))
    return jnp.moveaxis(out, 0, 2).reshape(B, H, T, Dh)


def compress(x_tok, pos_emb, w1, w2):
    B, G, T, Dh = x_tok.shape
    n_cmp = (T - CMP_BLOCK) // CMP_STRIDE + 1
    idx = jnp.arange(n_cmp)[:, None] * CMP_STRIDE + jnp.arange(CMP_BLOCK)[None, :]
    blocks = x_tok[:, :, idx, :] + pos_emb
    flat = blocks.reshape(B, G, n_cmp, CMP_BLOCK * Dh)
    return jax.nn.gelu(flat @ w1) @ w2


def native_sparse_attention(q, k_cmp, v_cmp, k_slc, v_slc, k_win, v_win, gates):
    B, H, T, Dh = q.shape
    G = NSA_KV_GROUPS
    R = H // G
    scale = Dh ** -0.5
    nb = T // Q_BLOCK
    n_cmp = k_cmp.shape[2]
    n_sel = T // SEL_BLOCK
    k_top = min(SEL_TOPK, n_sel)
    cmp_start = jnp.arange(n_cmp) * CMP_STRIDE
    cmp_end = cmp_start + CMP_BLOCK - 1
    sel_start = jnp.arange(n_sel) * SEL_BLOCK
    blk_ids = jnp.arange(n_sel)
    overlap = jnp.clip(jnp.minimum(cmp_start[:, None] + CMP_BLOCK, sel_start[None, :] + SEL_BLOCK)
                       - jnp.maximum(cmp_start[:, None], sel_start[None, :]), 0, None)
    cmp_to_sel = overlap.astype(jnp.float32) / CMP_BLOCK
    ks_blocks = k_slc.reshape(B, G, n_sel, SEL_BLOCK, Dh)
    vs_blocks = v_slc.reshape(B, G, n_sel, SEL_BLOCK, Dh)
    pad = ((0, 0), (0, 0), (WINDOW, 0), (0, 0))
    kw_pad = jnp.pad(k_win, pad)
    vw_pad = jnp.pad(v_win, pad)
    b_idx = jnp.arange(B)[:, None, None, None]
    g_idx = jnp.arange(G)[None, :, None, None]
    qb = jnp.moveaxis(q.reshape(B, G, R, nb, Q_BLOCK, Dh), 3, 0)
    gb = jnp.moveaxis(gates.reshape(B, G, R, nb, Q_BLOCK, N_GATES), 3, 0)

    def block(args):
        i, q_i, g_i = args
        q0 = i * Q_BLOCK
        q_pos = q0 + jnp.arange(Q_BLOCK)
        s_c = jnp.einsum('bgrqd,bgcd->bgrqc', q_i, k_cmp).astype(jnp.float32) * scale
        p_c = masked_softmax(s_c, cmp_end[None, :] <= q_pos[:, None], (-1,))
        o_c = jnp.einsum('bgrqc,bgcd->bgrqd', p_c.astype(v_cmp.dtype), v_cmp)
        imp = jnp.einsum('bgrqc,cn->bgqn', p_c, cmp_to_sel)
        forced = (blk_ids[None, :] == 0) | (blk_ids[None, :] == q_pos[:, None] // SEL_BLOCK)
        valid = sel_start[None, :] <= q_pos[:, None]
        imp = jnp.where(forced, jnp.inf, jnp.where(valid, imp, -jnp.inf))
        _, sel = lax.top_k(imp, k_top)
        k_g = ks_blocks[b_idx, g_idx, sel]
        v_g = vs_blocks[b_idx, g_idx, sel]
        tok_pos = sel[..., None] * SEL_BLOCK + jnp.arange(SEL_BLOCK)
        m_s = (tok_pos <= q_pos[:, None, None])[:, :, None]
        s_s = jnp.einsum('bgrqd,bgqnsd->bgrqns', q_i, k_g).astype(jnp.float32) * scale
        p_s = masked_softmax(s_s, m_s, (-2, -1))
        o_s = jnp.einsum('bgrqns,bgqnsd->bgrqd', p_s.astype(v_g.dtype), v_g)
        k_w = lax.dynamic_slice_in_dim(kw_pad, q0, WINDOW + Q_BLOCK, axis=2)
        v_w = lax.dynamic_slice_in_dim(vw_pad, q0, WINDOW + Q_BLOCK, axis=2)
        kv_pos = q0 - WINDOW + jnp.arange(WINDOW + Q_BLOCK)
        dist = q_pos[:, None] - kv_pos[None, :]
        m_w = (dist >= 0) & (dist < WINDOW) & (kv_pos[None, :] >= 0)
        s_w = jnp.einsum('bgrqd,bgkd->bgrqk', q_i, k_w).astype(jnp.float32) * scale
        p_w = masked_softmax(s_w, m_w, (-1,))
        o_w = jnp.einsum('bgrqk,bgkd->bgrqd', p_w.astype(v_w.dtype), v_w)
        out = g_i[..., 0:1] * o_c + g_i[..., 1:2] * o_s + g_i[..., 2:3] * o_w
        return out.astype(q_i.dtype)

    out = lax.map(block, (jnp.arange(nb), qb, gb))
    return jnp.moveaxis(out, 0, 3).reshape(B, H, T, Dh)


def group_rms(o, g):
    B, H, T, Dh = o.shape
    o = o.transpose(0, 2, 1, 3).reshape(B, T, H * Dh).astype(jnp.float32)
    return o * lax.rsqrt(jnp.mean(o * o, axis=-1, keepdims=True) + RMS_EPS) * g


def hybrid_mixer(h, w_in, cmp_pos_k, cmp_w1_k, cmp_w2_k, cmp_pos_v, cmp_w1_v, cmp_w2_v, mix_norm_g, w_out):
    B, T, _ = h.shape
    bounds = [int(b) for b in np.cumsum(IN_SPLITS)[:-1]]
    (sb_q, sb_k, sb_v, nsa_q, kc, vc, ksl, vsl, kwn, vwn, gate_logits) = jnp.split(h @ w_in, bounds, axis=-1)
    pos = jnp.arange(T)
    o_sb = stick_breaking_attention(split_heads(sb_q, SB_HEADS), split_heads(sb_k, SB_HEADS), split_heads(sb_v, SB_HEADS))
    q = rope(split_heads(nsa_q, NSA_HEADS), pos)
    k_cmp = compress(rope(split_heads(kc, NSA_KV_GROUPS), pos), cmp_pos_k, cmp_w1_k, cmp_w2_k)
    v_cmp = compress(split_heads(vc, NSA_KV_GROUPS), cmp_pos_v, cmp_w1_v, cmp_w2_v)
    k_slc = rope(split_heads(ksl, NSA_KV_GROUPS), pos)
    v_slc = split_heads(vsl, NSA_KV_GROUPS)
    k_win = rope(split_heads(kwn, NSA_KV_GROUPS), pos)
    v_win = split_heads(vwn, NSA_KV_GROUPS)
    gates = jax.nn.sigmoid(gate_logits.astype(jnp.float32)).reshape(B, T, NSA_HEADS, N_GATES).transpose(0, 2, 1, 3)
    o_nsa = native_sparse_attention(q, k_cmp, v_cmp, k_slc, v_slc, k_win, v_win, gates)
    y = jnp.concatenate([group_rms(o_sb, mix_norm_g[:SB_WIDTH]), group_rms(o_nsa, mix_norm_g[SB_WIDTH:])], axis=-1)
    return y.astype(h.dtype) @ w_out


def memory_cross_attention(x, mem, w_q, w_k, w_v, w_o):
    B, T, D = x.shape
    M = mem.shape[1]
    dh = D // MEM_HEADS
    q = (x @ w_q).reshape(B, T, MEM_HEADS, dh)
    k = (mem @ w_k).reshape(B, M, MEM_HEADS, dh)
    v = (mem @ w_v).reshape(B, M, MEM_HEADS, dh)
    s = jnp.einsum('bthd,bmhd->bhtm', q, k).astype(jnp.float32) * dh ** -0.5
    p = jax.nn.softmax(s, axis=-1).astype(v.dtype)
    o = jnp.einsum('bhtm,bmhd->bthd', p, v).reshape(B, T, D)
    return o @ w_o


def setup_inputs(seed: int = 0) -> dict:
    key = jax.random.key(seed)
    keys = iter(jax.random.split(key, 48))
    D = D_MODEL
    beta = DEEPNORM_BETA

    def nrm(shape, scale):
        return jax.random.normal(next(keys), shape, jnp.float32) * scale

    def gain(n):
        return 1.0 + nrm((DEPTH, n), 0.02)

    def bias(n):
        return nrm((DEPTH, n), 0.02)

    in_scale = jnp.asarray(np.concatenate([np.full((n,), beta if i in VALUE_SEGMENTS else 1.0, np.float32)
                                           for i, n in enumerate(IN_SPLITS)]))
    return {
        'x': nrm((BATCH, SEQ, D), 1.0),
        'mem': nrm((BATCH, MEM_TOKENS, D), 1.0),
        'ln1_g': gain(D), 'ln1_b': bias(D),
        'ffn1_gate': nrm((DEPTH, D, D_FF), D ** -0.5),
        'ffn1_up': nrm((DEPTH, D, D_FF), D ** -0.5 * beta),
        'ffn1_down': nrm((DEPTH, D_FF, D), D_FF ** -0.5 * beta),
        'w_in': nrm((DEPTH, D, IN_COLS), D ** -0.5) * in_scale,
        'cmp_pos_k': nrm((DEPTH, CMP_BLOCK, HEAD_DIM), 0.02),
        'cmp_w1_k': nrm((DEPTH, CMP_BLOCK * HEAD_DIM, CMP_HIDDEN), (CMP_BLOCK * HEAD_DIM) ** -0.5),
        'cmp_w2_k': nrm((DEPTH, CMP_HIDDEN, HEAD_DIM), CMP_HIDDEN ** -0.5),
        'cmp_pos_v': nrm((DEPTH, CMP_BLOCK, HEAD_DIM), 0.02),
        'cmp_w1_v': nrm((DEPTH, CMP_BLOCK * HEAD_DIM, CMP_HIDDEN), (CMP_BLOCK * HEAD_DIM) ** -0.5),
        'cmp_w2_v': nrm((DEPTH, CMP_HIDDEN, HEAD_DIM), CMP_HIDDEN ** -0.5),
        'mix_norm_g': gain(MIX_WIDTH),
        'w_out': nrm((DEPTH, MIX_WIDTH, D), MIX_WIDTH ** -0.5 * beta),
        'ln2_g': gain(D), 'ln2_b': bias(D),
        'mem_wq': nrm((DEPTH, D, D), D ** -0.5),
        'mem_wk': nrm((DEPTH, D, D), D ** -0.5),
        'mem_wv': nrm((DEPTH, D, D), D ** -0.5 * beta),
        'mem_wo': nrm((DEPTH, D, D), D ** -0.5 * beta),
        'ln3_g': gain(D), 'ln3_b': bias(D),
        'ffn2_gate': nrm((DEPTH, D, D_FF), D ** -0.5),
        'ffn2_up': nrm((DEPTH, D, D_FF), D ** -0.5 * beta),
        'ffn2_down': nrm((DEPTH, D_FF, D), D_FF ** -0.5 * beta),
        'ln4_g': gain(D), 'ln4_b': bias(D),
    }


def reference(x, mem, ln1_g, ln1_b, ffn1_gate, ffn1_up, ffn1_down, w_in, cmp_pos_k, cmp_w1_k, cmp_w2_k,
              cmp_pos_v, cmp_w1_v, cmp_w2_v, mix_norm_g, w_out, ln2_g, ln2_b, mem_wq, mem_wk, mem_wv, mem_wo,
              ln3_g, ln3_b, ffn2_gate, ffn2_up, ffn2_down, ln4_g, ln4_b):
    a = DEEPNORM_ALPHA
    for l in range(DEPTH):
        x = layer_norm(a * x + 0.5 * swiglu(x, ffn1_gate[l], ffn1_up[l], ffn1_down[l]), ln1_g[l], ln1_b[l])
        x = layer_norm(a * x + hybrid_mixer(x, w_in[l], cmp_pos_k[l], cmp_w1_k[l], cmp_w2_k[l], cmp_pos_v[l],
                                            cmp_w1_v[l], cmp_w2_v[l], mix_norm_g[l], w_out[l]), ln2_g[l], ln2_b[l])
        x = layer_norm(a * x + memory_cross_attention(x, mem, mem_wq[l], mem_wk[l], mem_wv[l], mem_wo[l]), ln3_g[l], ln3_b[l])
        x = layer_norm(a * x + 0.5 * swiglu(x, ffn2_gate[l], ffn2_up[l], ffn2_down[l]), ln4_g[l], ln4_b[l])
    return x
```

```python
import functools

import numpy as np
import jax
import jax.numpy as jnp
from jax import lax
from jax.experimental import pallas as pl
from jax.experimental.pallas import tpu as pltpu

HEAD_DIM = 128
SB_HEADS = 8
NSA_HEADS = 8
NSA_KV_GROUPS = 2
NSA_REP = NSA_HEADS // NSA_KV_GROUPS
N_GATES = 3
CMP_BLOCK = 32
CMP_STRIDE = 16
SEL_BLOCK = 64
SEL_SHIFT = 6
SEL_TOPK = 16
WINDOW = 512
MEM_HEADS = 4
ROPE_THETA = 10000.0
LN_EPS = 1e-5
RMS_EPS = 1e-6
MASK_FILL = -1e30
assert 1 << SEL_SHIFT == SEL_BLOCK

LANES = 128
SUBLANES = 8
FF_PAD = 512
VMEM_LIMIT = 56 * 1024 * 1024

BF16 = jnp.bfloat16
F32 = jnp.float32


def _params(sem):
    return pltpu.CompilerParams(dimension_semantics=sem, vmem_limit_bytes=VMEM_LIMIT)


def _dot(a, b):
    return jnp.dot(a, b, preferred_element_type=F32)


def _dot_nt(a, b):
    return lax.dot_general(a, b, (((1,), (1,)), ((), ())), preferred_element_type=F32)


def _split_dot(a, b):
    hi = a.astype(BF16)
    lo = (a - hi.astype(F32)).astype(BF16)
    return _dot(hi, b) + _dot(lo, b)


def _layer_norm(z, g, b):
    mu = jnp.mean(z, axis=-1, keepdims=True)
    zc = z - mu
    var = jnp.mean(zc * zc, axis=-1, keepdims=True)
    return zc * lax.rsqrt(var + LN_EPS) * g + b


def _proj_kernel(x_ref, w_ref, *rest, rope):
    if rope:
        cos_ref, sin_ref, o_ref = rest
    else:
        (o_ref,) = rest
    y = _dot(x_ref[...], w_ref[...])
    if rope:
        c = cos_ref[...]
        s = sin_ref[...]
        heads = []
        for h in range(y.shape[1] // HEAD_DIM):
            yh = y[:, h * HEAD_DIM:(h + 1) * HEAD_DIM]
            heads.append(yh * c + pltpu.roll(yh, HEAD_DIM // 2, 1) * s)
        y = jnp.concatenate(heads, axis=1) if len(heads) > 1 else heads[0]
    o_ref[...] = y.astype(o_ref.dtype)


def _project(x, w, out_dtype, tm, tn, rope_tables=None):
    M, K = x.shape
    N = w.shape[1]
    tm = min(tm, M)
    assert M % tm == 0 and N % tn == 0
    in_specs = [pl.BlockSpec((tm, K), lambda i, j: (i, 0)),
                pl.BlockSpec((K, tn), lambda i, j: (0, j))]
    args = [x, w]
    if rope_tables is not None:
        in_specs += [pl.BlockSpec((tm, HEAD_DIM), lambda i, j: (i, 0))] * 2
        args += list(rope_tables)
    return pl.pallas_call(
        functools.partial(_proj_kernel, rope=rope_tables is not None),
        out_shape=jax.ShapeDtypeStruct((M, N), out_dtype),
        grid=(M // tm, N // tn),
        in_specs=in_specs,
        out_specs=pl.BlockSpec((tm, tn), lambda i, j: (i, j)),
        compiler_params=_params(("parallel", "arbitrary")),
        name="proj_rope" if rope_tables is not None else "proj",
    )(*args)


def _swiglu_up_kernel(x_ref, wg_ref, wu_ref, o_ref):
    x = x_ref[...]
    g = _dot(x, wg_ref[...])
    u = _dot(x, wu_ref[...])
    o_ref[...] = (jax.nn.silu(g) * u).astype(o_ref.dtype)


def _swiglu_up(x, wg, wu, tm=1024, tn=512):
    M, K = x.shape
    N = wg.shape[1]
    tm = min(tm, M)
    assert M % tm == 0 and N % tn == 0
    return pl.pallas_call(
        _swiglu_up_kernel,
        out_shape=jax.ShapeDtypeStruct((M, N), BF16),
        grid=(M // tm, N // tn),
        in_specs=[pl.BlockSpec((tm, K), lambda i, j: (i, 0)),
                  pl.BlockSpec((K, tn), lambda i, j: (0, j)),
                  pl.BlockSpec((K, tn), lambda i, j: (0, j))],
        out_specs=pl.BlockSpec((tm, tn), lambda i, j: (i, j)),
        compiler_params=_params(("parallel", "arbitrary")),
        name="swiglu_up",
    )(x, wg, wu)


def _mm_res_ln_kernel(h_ref, w_ref, res_ref, g_ref, b_ref, *rest, alpha, coef, emit_bf16):
    if emit_bf16:
        of_ref, ob_ref, acc_ref = rest
    else:
        of_ref, acc_ref = rest
    k = pl.program_id(1)

    @pl.when(k == 0)
    def _():
        acc_ref[...] = jnp.zeros_like(acc_ref)

    acc_ref[...] += _dot(h_ref[...], w_ref[...])

    @pl.when(k == pl.num_programs(1) - 1)
    def _():
        z = alpha * res_ref[...] + coef * acc_ref[...]
        y = _layer_norm(z, g_ref[...], b_ref[...])
        of_ref[...] = y
        if emit_bf16:
            ob_ref[...] = y.astype(BF16)


def _mm_res_ln(h, w, res, g, b, alpha, coef, emit_bf16, tm=512, tk=512):
    M, K = h.shape
    N = w.shape[1]
    tm = min(tm, M)
    assert M % tm == 0 and K % tk == 0
    out_shape = [jax.ShapeDtypeStruct((M, N), F32)]
    out_specs = [pl.BlockSpec((tm, N), lambda i, k: (i, 0))]
    if emit_bf16:
        out_shape.append(jax.ShapeDtypeStruct((M, N), BF16))
        out_specs.append(pl.BlockSpec((tm, N), lambda i, k: (i, 0)))
    out = pl.pallas_call(
        functools.partial(_mm_res_ln_kernel, alpha=alpha, coef=coef, emit_bf16=emit_bf16),
        out_shape=out_shape,
        grid=(M // tm, K // tk),
        in_specs=[pl.BlockSpec((tm, tk), lambda i, k: (i, k)),
                  pl.BlockSpec((tk, N), lambda i, k: (k, 0)),
                  pl.BlockSpec((tm, N), lambda i, k: (i, 0)),
                  pl.BlockSpec((1, N), lambda i, k: (0, 0)),
                  pl.BlockSpec((1, N), lambda i, k: (0, 0))],
        out_specs=out_specs,
        scratch_shapes=[pltpu.VMEM((tm, N), F32)],
        compiler_params=_params(("parallel", "arbitrary")),
        name="mm_res_ln",
    )(h, w, res, g.reshape(1, N), b.reshape(1, N))
    return (out[0], out[1]) if emit_bf16 else (out[0], None)


def _log_sigmoid(z):
    return jnp.minimum(z, 0.0) - jnp.log1p(jnp.exp(-jnp.abs(z)))


def _sb_kernel(q_ref, k_ref, v_ref, o_ref, acc_ref, c_ref, *, blk, scale):
    qi = pl.program_id(1)
    q = q_ref[...]
    row = lax.broadcasted_iota(jnp.int32, (blk, blk), 0)
    col = lax.broadcasted_iota(jnp.int32, (blk, blk), 1)
    suffix = jnp.where(row > col, 1.0, 0.0).astype(BF16)
    strict = col < row

    def step(kb, diagonal):
        ks = pl.multiple_of(kb * blk, blk)
        z = _dot_nt(q, k_ref[pl.ds(ks, blk), :]) * scale
        log_beta = _log_sigmoid(z)
        log_1m = log_beta - z
        if diagonal:
            log_1m = jnp.where(strict, log_1m, 0.0)
        log_stay = _split_dot(log_1m, suffix) + c_ref[...]
        w = jnp.exp(log_beta + log_stay)
        if diagonal:
            w = jnp.where(strict, w, 0.0)
        acc_ref[...] += _dot(w.astype(BF16), v_ref[pl.ds(ks, blk), :])
        c_ref[...] += jnp.sum(log_1m, axis=-1, keepdims=True)

    acc_ref[...] = jnp.zeros_like(acc_ref)
    c_ref[...] = jnp.zeros_like(c_ref)
    step(qi, True)

    def body(it, carry):
        step(qi - 1 - it, False)
        return carry

    lax.fori_loop(0, qi, body, 0)
    o_ref[...] = acc_ref[...]


def _sb_attention(qkv, n_heads, q_col, k_col, v_col, blk=256):
    T = qkv.shape[0]
    blk = min(blk, T)
    assert T % blk == 0
    return pl.pallas_call(
        functools.partial(_sb_kernel, blk=blk, scale=HEAD_DIM ** -0.5),
        out_shape=jax.ShapeDtypeStruct((T, n_heads * HEAD_DIM), F32),
        grid=(n_heads, T // blk),
        in_specs=[pl.BlockSpec((blk, HEAD_DIM), lambda h, i: (i, q_col + h)),
                  pl.BlockSpec((T, HEAD_DIM), lambda h, i: (0, k_col + h)),
                  pl.BlockSpec((T, HEAD_DIM), lambda h, i: (0, v_col + h))],
        out_specs=pl.BlockSpec((blk, HEAD_DIM), lambda h, i: (i, h)),
        scratch_shapes=[pltpu.VMEM((blk, HEAD_DIM), F32), pltpu.VMEM((blk, 1), F32)],
        compiler_params=_params(("parallel", "arbitrary")),
        name="sb_attention",
    )(qkv, qkv, qkv)


def _compress_kernel(x_ref, pos_ref, w1_ref, w2_ref, o_ref):
    n = x_ref.shape[1]
    half = CMP_STRIDE * HEAD_DIM
    x = x_ref[0].astype(BF16)
    w1 = w1_ref[0].astype(BF16)
    a = _dot(x, w1[:half])
    b = _dot(x, w1[half:])
    pos = _dot(pos_ref[0].astype(BF16), w1)
    b_next = pltpu.roll(b, n - 1, 0)
    hid = jax.nn.gelu(a + b_next + pos[0:1])
    out = _dot(hid.astype(BF16), w2_ref[0].astype(BF16))
    valid = lax.broadcasted_iota(jnp.int32, out.shape, 0) < n - 1
    o_ref[0] = jnp.where(valid, out, 0.0).astype(o_ref.dtype)


def _compress(x_tok, pos_emb, w1, w2):
    S, T, _ = x_tok.shape
    n = T // CMP_STRIDE
    width = CMP_STRIDE * HEAD_DIM
    hidden = w1.shape[-1]
    x2 = x_tok.reshape(S, n, width)
    pos_flat = jnp.broadcast_to(pos_emb.reshape(S, 1, 2 * width), (S, SUBLANES, 2 * width))
    return pl.pallas_call(
        _compress_kernel,
        out_shape=jax.ShapeDtypeStruct((S, n, HEAD_DIM), BF16),
        grid=(S,),
        in_specs=[pl.BlockSpec((1, n, width), lambda s: (s, 0, 0)),
                  pl.BlockSpec((1, SUBLANES, 2 * width), lambda s: (s, 0, 0)),
                  pl.BlockSpec((1, 2 * width, hidden), lambda s: (s, 0, 0)),
                  pl.BlockSpec((1, hidden, HEAD_DIM), lambda s: (s, 0, 0))],
        out_specs=pl.BlockSpec((1, n, HEAD_DIM), lambda s: (s, 0, 0)),
        compiler_params=_params(("parallel",)),
        name="nsa_compress",
    )(x2, pos_flat, w1, w2)


def _nsa_kernel(q_ref, kc_ref, vc_ref, ks_ref, vs_ref, kw_ref, vw_ref, gate_ref, c2s_ref, o_ref,
                m_ref, l_ref, acc_ref, *, blk, scale):
    R = NSA_REP
    qi = pl.program_id(1)
    q0 = qi * blk
    n_cmp = kc_ref.shape[1]
    n_sel = c2s_ref.shape[1]
    rows = [slice(r * blk, (r + 1) * blk) for r in range(R)]
    qs = jnp.concatenate([q_ref[:, r * HEAD_DIM:(r + 1) * HEAD_DIM] for r in range(R)], axis=0)
    q_pos = q0 + lax.broadcasted_iota(jnp.int32, (blk, 1), 0)

    s_all = _dot_nt(qs, kc_ref[0])
    cmp_end = lax.broadcasted_iota(jnp.int32, (1, n_cmp), 1) * CMP_STRIDE + (CMP_BLOCK - 1)
    mask_c = cmp_end <= q_pos
    p_heads = []
    for r in range(R):
        s = jnp.where(mask_c, s_all[rows[r]] * scale, MASK_FILL)
        p = jnp.where(mask_c, jnp.exp(s - jnp.max(s, axis=-1, keepdims=True)), 0.0)
        p_heads.append(p / jnp.maximum(jnp.sum(p, axis=-1, keepdims=True), 1e-30))
    o_cmp = _dot(jnp.concatenate([p.astype(BF16) for p in p_heads], axis=0), vc_ref[0])

    imp = _split_dot(sum(p_heads[1:], p_heads[0]), c2s_ref[...])
    blk_id = lax.broadcasted_iota(jnp.int32, (1, n_sel), 1)
    forced = (blk_id == 0) | (blk_id == (q_pos >> SEL_SHIFT))
    valid = blk_id * SEL_BLOCK <= q_pos
    work = jnp.where(forced, jnp.inf, jnp.where(valid, imp, -jnp.inf))
    blk_idf = blk_id.astype(F32)
    sel = jnp.zeros((blk, n_sel), F32)
    for _ in range(min(SEL_TOPK, n_sel)):
        free = sel == 0.0
        top = jnp.max(jnp.where(free, work, -jnp.inf), axis=-1, keepdims=True)
        cand = jnp.where(free & (work == top), blk_idf, float(n_sel))
        sel = jnp.where(blk_idf == jnp.min(cand, axis=-1, keepdims=True), 1.0, sel)
    sel = sel.astype(BF16)

    def softmax_step(kb, k_ref, v_ref, bias):
        ks = pl.multiple_of(kb * blk, blk)
        sc_all = _dot_nt(qs, k_ref[pl.ds(ks, blk), :])
        probs, alphas = [], []
        for r in range(R):
            sc = sc_all[rows[r]] * scale + bias
            m_old = m_ref[rows[r]]
            m_new = jnp.maximum(m_old, jnp.max(sc, axis=-1, keepdims=True))
            pr = jnp.exp(sc - m_new)
            alpha = jnp.exp(m_old - m_new)
            l_ref[rows[r]] = alpha * l_ref[rows[r]] + jnp.sum(pr, axis=-1, keepdims=True)
            m_ref[rows[r]] = m_new
            probs.append(pr.astype(BF16))
            alphas.append(alpha)
        pv = _dot(jnp.concatenate(probs, axis=0), v_ref[pl.ds(ks, blk), :])
        acc_ref[...] = jnp.concatenate(alphas, axis=0) * acc_ref[...] + pv

    def reset():
        m_ref[...] = jnp.full_like(m_ref, MASK_FILL)
        l_ref[...] = jnp.zeros_like(l_ref)
        acc_ref[...] = jnp.zeros_like(acc_ref)

    key_off = lax.broadcasted_iota(jnp.int32, (1, blk), 1)
    tok = lax.broadcasted_iota(jnp.int32, (n_sel, blk), 1)
    sel_row = lax.broadcasted_iota(jnp.int32, (n_sel, blk), 0)

    def sel_bias(kb, diagonal):
        expand = jnp.where(((kb * blk + tok) >> SEL_SHIFT) == sel_row, 1.0, 0.0).astype(BF16)
        chosen = _dot(sel, expand) > 0.5
        if diagonal:
            chosen = chosen & (kb * blk + key_off <= q_pos)
        return jnp.where(chosen, 0.0, MASK_FILL)

    reset()

    def sel_body(kb, carry):
        softmax_step(kb, ks_ref, vs_ref, sel_bias(kb, False))
        return carry

    lax.fori_loop(0, qi, sel_body, 0)
    softmax_step(qi, ks_ref, vs_ref, sel_bias(qi, True))
    o_sel = acc_ref[...] / l_ref[...]

    reset()

    def win_body(kb, carry):
        dist = q_pos - (kb * blk + key_off)
        softmax_step(kb, kw_ref, vw_ref, jnp.where((dist >= 0) & (dist < WINDOW), 0.0, MASK_FILL))
        return carry

    lax.fori_loop(jnp.maximum(qi - (WINDOW + blk - 1) // blk, 0), qi + 1, win_body, 0)
    o_win = acc_ref[...] / l_ref[...]

    gates = jax.nn.sigmoid(gate_ref[...])
    outs = []
    for r in range(R):
        g = [gates[:, r * N_GATES + j:r * N_GATES + j + 1] for j in range(N_GATES)]
        outs.append(g[0] * o_cmp[rows[r]] + g[1] * o_sel[rows[r]] + g[2] * o_win[rows[r]])
    o_ref[...] = jnp.concatenate(outs, axis=1)


def _nsa_attention(q_tok, v_tok, k_cmp, v_cmp, gates, ks_col, kw_col, vs_col, vw_col, gate_col, blk=128):
    T = q_tok.shape[0]
    G = NSA_KV_GROUPS
    blk = min(blk, T)
    n_cmp = T // CMP_STRIDE
    n_sel = T // SEL_BLOCK
    cmp_start = np.arange(n_cmp)[:, None] * CMP_STRIDE
    sel_start = np.arange(n_sel)[None, :] * SEL_BLOCK
    overlap = np.clip(np.minimum(cmp_start + CMP_BLOCK, sel_start + SEL_BLOCK)
                      - np.maximum(cmp_start, sel_start), 0, None)
    c2s = jnp.asarray(overlap.astype(np.float32) / CMP_BLOCK, dtype=BF16)
    width = NSA_REP * HEAD_DIM
    tok = lambda col: pl.BlockSpec((T, HEAD_DIM), lambda g, i: (0, col + g))
    cmp_spec = pl.BlockSpec((1, n_cmp, HEAD_DIM), lambda g, i: (g, 0, 0))
    return pl.pallas_call(
        functools.partial(_nsa_kernel, blk=blk, scale=HEAD_DIM ** -0.5),
        out_shape=jax.ShapeDtypeStruct((T, G * width), F32),
        grid=(G, T // blk),
        in_specs=[pl.BlockSpec((blk, width), lambda g, i: (i, g)),
                  cmp_spec, cmp_spec,
                  tok(ks_col), tok(vs_col), tok(kw_col), tok(vw_col),
                  pl.BlockSpec((blk, LANES), lambda g, i: (i, gate_col + g)),
                  pl.BlockSpec((n_cmp, n_sel), lambda g, i: (0, 0))],
        out_specs=pl.BlockSpec((blk, width), lambda g, i: (i, g)),
        scratch_shapes=[pltpu.VMEM((NSA_REP * blk, 1), F32), pltpu.VMEM((NSA_REP * blk, 1), F32),
                        pltpu.VMEM((NSA_REP * blk, HEAD_DIM), F32)],
        compiler_params=_params(("parallel", "arbitrary")),
        name="nsa_attention",
    )(q_tok, k_cmp, v_cmp, q_tok, v_tok, q_tok, v_tok, gates, c2s)


def _group_rms_kernel(a_ref, b_ref, g_ref, o_ref):
    wa = a_ref.shape[1]

    def norm(o, g):
        return o * lax.rsqrt(jnp.mean(o * o, axis=-1, keepdims=True) + RMS_EPS) * g

    g = g_ref[...]
    o_ref[:, :wa] = norm(a_ref[...], g[:, :wa]).astype(o_ref.dtype)
    o_ref[:, wa:] = norm(b_ref[...], g[:, wa:]).astype(o_ref.dtype)


def _group_rms(o_a, o_b, g, tm=512):
    T, wa = o_a.shape
    wb = o_b.shape[1]
    tm = min(tm, T)
    return pl.pallas_call(
        _group_rms_kernel,
        out_shape=jax.ShapeDtypeStruct((T, wa + wb), BF16),
        grid=(T // tm,),
        in_specs=[pl.BlockSpec((tm, wa), lambda i: (i, 0)),
                  pl.BlockSpec((tm, wb), lambda i: (i, 0)),
                  pl.BlockSpec((1, wa + wb), lambda i: (0, 0))],
        out_specs=pl.BlockSpec((tm, wa + wb), lambda i: (i, 0)),
        compiler_params=_params(("parallel",)),
        name="group_rms",
    )(o_a, o_b, g.reshape(1, wa + wb))


def _cross_kernel(q_ref, k_ref, v_ref, o_ref, *, scale):
    s = _dot_nt(q_ref[...], k_ref[...]) * scale
    p = jnp.exp(s - jnp.max(s, axis=-1, keepdims=True))
    p = p / jnp.sum(p, axis=-1, keepdims=True)
    o_ref[...] = _dot(p.astype(BF16), v_ref[...]).astype(o_ref.dtype)


def _cross_attention(q, kv, n_heads, tm=1024):
    T, D = q.shape
    M = kv.shape[0]
    dh = D // n_heads
    tm = min(tm, T)
    return pl.pallas_call(
        functools.partial(_cross_kernel, scale=dh ** -0.5),
        out_shape=jax.ShapeDtypeStruct((T, D), BF16),
        grid=(T // tm, n_heads),
        in_specs=[pl.BlockSpec((tm, dh), lambda i, h: (i, h)),
                  pl.BlockSpec((M, dh), lambda i, h: (0, h)),
                  pl.BlockSpec((M, dh), lambda i, h: (0, n_heads + h))],
        out_specs=pl.BlockSpec((tm, dh), lambda i, h: (i, h)),
        compiler_params=_params(("parallel", "arbitrary")),
        name="cross_attention",
    )(q, kv, kv)


def _rope_tables(T):
    half = HEAD_DIM // 2
    inv_freq = ROPE_THETA ** (-jnp.arange(half, dtype=F32) / half)
    ang = jnp.arange(T).astype(F32)[:, None] * inv_freq[None, :]
    cos, sin = jnp.cos(ang), jnp.sin(ang)
    return jnp.concatenate([cos, cos], axis=1), jnp.concatenate([-sin, sin], axis=1)


def _pad_to(w, axis, mult):
    pad = -w.shape[axis] % mult
    if not pad:
        return w
    widths = [(0, 0)] * w.ndim
    widths[axis] = (0, pad)
    return jnp.pad(w, widths)


def _ffn(x_f32, x_bf16, w_gate, w_up, w_down, ln_g, ln_b, alpha, emit_bf16):
    wg = _pad_to(w_gate, 1, FF_PAD).astype(BF16)
    wu = _pad_to(w_up, 1, FF_PAD).astype(BF16)
    wd = _pad_to(w_down, 0, FF_PAD).astype(BF16)
    h = _swiglu_up(x_bf16, wg, wu)
    return _mm_res_ln(h, wd, x_f32, ln_g, ln_b, alpha, 0.5, emit_bf16)


def _mixer(x_f32, x_bf16, w_in, cmp_pos_k, cmp_w1_k, cmp_w2_k, cmp_pos_v, cmp_w1_v, cmp_w2_v,
           mix_norm_g, w_out, ln_g, ln_b, alpha):
    T = x_f32.shape[0]
    G = NSA_KV_GROUPS
    sbw = SB_HEADS * HEAD_DIM
    nqw = NSA_HEADS * HEAD_DIM
    kvw = G * HEAD_DIM
    bounds = np.cumsum([0, sbw, sbw, sbw, nqw, kvw, kvw, kvw, kvw, kvw, kvw, NSA_HEADS * N_GATES])
    (w_sbq, w_sbk, w_sbv, w_nq, w_kc, w_vc, w_ks, w_vs, w_kw, w_vw, w_gate) = [
        w_in[:, bounds[i]:bounds[i + 1]] for i in range(11)]
    per_group = NSA_REP * N_GATES
    w_gate = jnp.concatenate([_pad_to(w_gate[:, g * per_group:(g + 1) * per_group], 1, LANES)
                              for g in range(G)], axis=1)
    tables = _rope_tables(T)
    cat = lambda ws: jnp.concatenate(ws, axis=1).astype(BF16)
    plain = _project(x_bf16, cat([w_sbq, w_sbk, w_sbv, w_vs, w_vw]), BF16, 1024, 512)
    roped = _project(x_bf16, cat([w_nq, w_ks, w_kw]), BF16, 1024, 512, tables)
    kc = _project(x_bf16, w_kc.astype(BF16), F32, 1024, kvw, tables)
    vc_gate = _project(x_bf16, cat([w_vc, w_gate]), F32, 1024, 2 * kvw)

    o_sb = _sb_attention(plain, SB_HEADS, 0, SB_HEADS, 2 * SB_HEADS)

    head = lambda a, g: a[:, g * HEAD_DIM:(g + 1) * HEAD_DIM]
    streams = jnp.stack([head(kc, g) for g in range(G)] + [head(vc_gate, g) for g in range(G)])
    rep = lambda a, b: jnp.stack([a] * G + [b] * G)
    cmp = _compress(streams, rep(cmp_pos_k, cmp_pos_v), rep(cmp_w1_k, cmp_w1_v), rep(cmp_w2_k, cmp_w2_v))
    o_nsa = _nsa_attention(roped, plain, cmp[:G], cmp[G:], vc_gate,
                           ks_col=NSA_HEADS, kw_col=NSA_HEADS + G,
                           vs_col=3 * SB_HEADS, vw_col=3 * SB_HEADS + G, gate_col=G)
    y = _group_rms(o_sb, o_nsa, mix_norm_g)
    return _mm_res_ln(y, w_out.astype(BF16), x_f32, ln_g, ln_b, alpha, 1.0, True)


def _memory_block(x_f32, x_bf16, mem, w_q, w_k, w_v, w_o, ln_g, ln_b, alpha):
    q = _project(x_bf16, w_q.astype(BF16), BF16, 1024, 1024)
    kv = _project(mem.astype(BF16), jnp.concatenate([w_k, w_v], axis=1).astype(BF16), BF16, 256, 1024)
    o = _cross_attention(q, kv, MEM_HEADS)
    return _mm_res_ln(o, w_o.astype(BF16), x_f32, ln_g, ln_b, alpha, 1.0, True)


def kernel(x, mem, ln1_g, ln1_b, ffn1_gate, ffn1_up, ffn1_down, w_in, cmp_pos_k, cmp_w1_k, cmp_w2_k, cmp_pos_v, cmp_w1_v, cmp_w2_v, mix_norm_g, w_out, ln2_g, ln2_b, mem_wq, mem_wk, mem_wv, mem_wo, ln3_g, ln3_b, ffn2_gate, ffn2_up, ffn2_down, ln4_g, ln4_b):
    n_layers = ffn1_gate.shape[0]
    alpha = (2 * n_layers) ** 0.25
    outs = []
    for bi in range(x.shape[0]):
        xf = x[bi]
        xb = xf.astype(BF16)
        for l in range(n_layers):
            xf, xb = _ffn(xf, xb, ffn1_gate[l], ffn1_up[l], ffn1_down[l], ln1_g[l], ln1_b[l], alpha, True)
            xf, xb = _mixer(xf, xb, w_in[l], cmp_pos_k[l], cmp_w1_k[l], cmp_w2_k[l], cmp_pos_v[l], cmp_w1_v[l],
                            cmp_w2_v[l], mix_norm_g[l], w_out[l], ln2_g[l], ln2_b[l], alpha)
            xf, xb = _memory_block(xf, xb, mem[bi], mem_wq[l], mem_wk[l], mem_wv[l], mem_wo[l], ln3_g[l], ln3_b[l],
                                   alpha)
            xf, xb = _ffn(xf, xb, ffn2_gate[l], ffn2_up[l], ffn2_down[l], ln4_g[l], ln4_b[l], alpha,
                          l + 1 < n_layers)
        outs.append(xf)
    return jnp.stack(outs)
```

```python
import functools

import numpy as np
import jax
import jax.numpy as jnp
from jax import lax
from jax.experimental import pallas as pl
from jax.experimental.pallas import tpu as pltpu

HEAD_DIM = 128
SB_HEADS = 8
NSA_HEADS = 8
NSA_KV_GROUPS = 2
NSA_REP = NSA_HEADS // NSA_KV_GROUPS
N_GATES = 3
CMP_BLOCK = 32
CMP_STRIDE = 16
SEL_BLOCK = 64
SEL_SHIFT = 6
SEL_TOPK = 16
WINDOW = 512
MEM_HEADS = 4
ROPE_THETA = 10000.0
LN_EPS = 1e-5
RMS_EPS = 1e-6
MASK_FILL = -1e30
assert 1 << SEL_SHIFT == SEL_BLOCK

LANES = 128
SUBLANES = 8
FF_PAD = 512
VMEM_LIMIT = 56 * 1024 * 1024

BF16 = jnp.bfloat16
F32 = jnp.float32


def _params(sem):
    return pltpu.CompilerParams(dimension_semantics=sem, vmem_limit_bytes=VMEM_LIMIT)


def _dot(a, b):
    return jnp.dot(a, b, preferred_element_type=F32)


def _dot_nt(a, b):
    return lax.dot_general(a, b, (((1,), (1,)), ((), ())), preferred_element_type=F32)


def _split_dot(a, b):
    hi = a.astype(BF16)
    lo = (a - hi.astype(F32)).astype(BF16)
    return _dot(hi, b) + _dot(lo, b)


def _layer_norm(z, g, b):
    mu = jnp.mean(z, axis=-1, keepdims=True)
    zc = z - mu
    var = jnp.mean(zc * zc, axis=-1, keepdims=True)
    return zc * lax.rsqrt(var + LN_EPS) * g + b


def _proj_kernel(x_ref, w_ref, *rest, scaled, rope):
    rest = list(rest)
    o_ref = rest.pop()
    y = _dot(x_ref[...], w_ref[...])
    if scaled:
        y = y * rest.pop(0)[...]
    if rope:
        cos_ref, sin_ref = rest
        c = cos_ref[...]
        s = sin_ref[...]
        heads = []
        for h in range(y.shape[1] // HEAD_DIM):
            yh = y[:, h * HEAD_DIM:(h + 1) * HEAD_DIM]
            heads.append(yh * c + pltpu.roll(yh, HEAD_DIM // 2, 1) * s)
        y = jnp.concatenate(heads, axis=1) if len(heads) > 1 else heads[0]
    o_ref[...] = y.astype(o_ref.dtype)


def _project(x, w, out_dtype, tm, tn, col_scale=None, rope_tables=None):
    M, K = x.shape
    N = w.shape[1]
    tm = min(tm, M)
    assert M % tm == 0 and N % tn == 0
    in_specs = [pl.BlockSpec((tm, K), lambda i, j: (i, 0)),
                pl.BlockSpec((K, tn), lambda i, j: (0, j))]
    args = [x, w]
    if col_scale is not None:
        in_specs.append(pl.BlockSpec((1, tn), lambda i, j: (0, j)))
        args.append(col_scale.reshape(1, N))
    if rope_tables is not None:
        in_specs += [pl.BlockSpec((tm, HEAD_DIM), lambda i, j: (i, 0))] * 2
        args += list(rope_tables)
    return pl.pallas_call(
        functools.partial(_proj_kernel, scaled=col_scale is not None, rope=rope_tables is not None),
        out_shape=jax.ShapeDtypeStruct((M, N), out_dtype),
        grid=(M // tm, N // tn),
        in_specs=in_specs,
        out_specs=pl.BlockSpec((tm, tn), lambda i, j: (i, j)),
        compiler_params=_params(("parallel", "arbitrary")),
        name="proj_rope" if rope_tables is not None else "proj",
    )(*args)


def _swiglu_up_kernel(x_ref, wg_ref, wu_ref, o_ref):
    x = x_ref[...]
    g = _dot(x, wg_ref[...])
    u = _dot(x, wu_ref[...])
    o_ref[...] = (jax.nn.silu(g) * u).astype(o_ref.dtype)


def _swiglu_up(x, wg, wu, tm=1024, tn=512):
    M, K = x.shape
    N = wg.shape[1]
    tm = min(tm, M)
    assert M % tm == 0 and N % tn == 0
    return pl.pallas_call(
        _swiglu_up_kernel,
        out_shape=jax.ShapeDtypeStruct((M, N), BF16),
        grid=(M // tm, N // tn),
        in_specs=[pl.BlockSpec((tm, K), lambda i, j: (i, 0)),
                  pl.BlockSpec((K, tn), lambda i, j: (0, j)),
                  pl.BlockSpec((K, tn), lambda i, j: (0, j))],
        out_specs=pl.BlockSpec((tm, tn), lambda i, j: (i, j)),
        compiler_params=_params(("parallel", "arbitrary")),
        name="swiglu_up",
    )(x, wg, wu)


def _mm_res_ln_kernel(h_ref, w_ref, res_ref, g_ref, b_ref, *rest, alpha, coef, emit_bf16):
    if emit_bf16:
        of_ref, ob_ref, acc_ref = rest
    else:
        of_ref, acc_ref = rest
    k = pl.program_id(1)

    @pl.when(k == 0)
    def _():
        acc_ref[...] = jnp.zeros_like(acc_ref)

    acc_ref[...] += _dot(h_ref[...], w_ref[...])

    @pl.when(k == pl.num_programs(1) - 1)
    def _():
        z = alpha * res_ref[...] + coef * acc_ref[...]
        y = _layer_norm(z, g_ref[...], b_ref[...])
        of_ref[...] = y
        if emit_bf16:
            ob_ref[...] = y.astype(BF16)


def _mm_res_ln(h, w, res, g, b, alpha, coef, emit_bf16, tm=512, tk=512):
    M, K = h.shape
    N = w.shape[1]
    tm = min(tm, M)
    assert M % tm == 0 and K % tk == 0
    out_shape = [jax.ShapeDtypeStruct((M, N), F32)]
    out_specs = [pl.BlockSpec((tm, N), lambda i, k: (i, 0))]
    if emit_bf16:
        out_shape.append(jax.ShapeDtypeStruct((M, N), BF16))
        out_specs.append(pl.BlockSpec((tm, N), lambda i, k: (i, 0)))
    out = pl.pallas_call(
        functools.partial(_mm_res_ln_kernel, alpha=alpha, coef=coef, emit_bf16=emit_bf16),
        out_shape=out_shape,
        grid=(M // tm, K // tk),
        in_specs=[pl.BlockSpec((tm, tk), lambda i, k: (i, k)),
                  pl.BlockSpec((tk, N), lambda i, k: (k, 0)),
                  pl.BlockSpec((tm, N), lambda i, k: (i, 0)),
                  pl.BlockSpec((1, N), lambda i, k: (0, 0)),
                  pl.BlockSpec((1, N), lambda i, k: (0, 0))],
        out_specs=out_specs,
        scratch_shapes=[pltpu.VMEM((tm, N), F32)],
        compiler_params=_params(("parallel", "arbitrary")),
        name="mm_res_ln",
    )(h, w, res, g.reshape(1, N), b.reshape(1, N))
    return (out[0], out[1]) if emit_bf16 else (out[0], None)


def _sb_kernel(q_ref, k_ref, v_ref, o_ref, acc_ref, c_ref, *, blk, heads):
    qi = pl.program_id(1)
    row = lax.broadcasted_iota(jnp.int32, (blk, blk), 0)
    col = lax.broadcasted_iota(jnp.int32, (blk, blk), 1)
    suffix = jnp.where(row > col, 1.0, 0.0).astype(BF16)
    strict = col < row

    def step(kb, diagonal):
        ks = pl.multiple_of(kb * blk, blk)
        cols = [slice(h * HEAD_DIM, (h + 1) * HEAD_DIM) for h in range(heads)]
        z = [_dot_nt(q_ref[:, c], k_ref[pl.ds(ks, blk), c]) for c in cols]
        log_beta, log_1m = [], []
        for y in z:
            m = jnp.minimum(y, 0.0)
            n = m - y
            soft = jnp.log(1.0 + jnp.exp(m + n))
            l1 = n - soft
            if diagonal:
                l1 = jnp.where(strict, l1, 0.0)
            log_beta.append(m - soft)
            log_1m.append(l1.astype(BF16))
        log_stay = [_dot(l1, suffix) for l1 in log_1m]
        weights = []
        for h in range(heads):
            w = jnp.exp(log_beta[h] + (log_stay[h] + c_ref[h]))
            if diagonal:
                w = jnp.where(strict, w, 0.0)
            weights.append(w.astype(BF16))
            c_ref[h] += log_stay[h][:, :1] + log_1m[h][:, :1].astype(F32)
        for h in range(heads):
            acc_ref[:, cols[h]] += _dot(weights[h], v_ref[pl.ds(ks, blk), cols[h]])

    acc_ref[...] = jnp.zeros_like(acc_ref)
    c_ref[...] = jnp.zeros_like(c_ref)
    step(qi, True)

    def body(it, carry):
        step(qi - 1 - it, False)
        return carry

    lax.fori_loop(0, qi, body, 0)
    o_ref[...] = acc_ref[...]


def _sb_attention(qkv, n_heads, q_col, k_col, v_col, blk=256, heads=4):
    T = qkv.shape[0]
    blk = min(blk, T)
    width = heads * HEAD_DIM
    assert T % blk == 0 and n_heads % heads == 0
    assert q_col % heads == 0 and k_col % heads == 0 and v_col % heads == 0
    return pl.pallas_call(
        functools.partial(_sb_kernel, blk=blk, heads=heads),
        out_shape=jax.ShapeDtypeStruct((T, n_heads * HEAD_DIM), F32),
        grid=(n_heads // heads, T // blk),
        in_specs=[pl.BlockSpec((blk, width), lambda h, i: (i, q_col // heads + h)),
                  pl.BlockSpec((T, width), lambda h, i: (0, k_col // heads + h)),
                  pl.BlockSpec((T, width), lambda h, i: (0, v_col // heads + h))],
        out_specs=pl.BlockSpec((blk, width), lambda h, i: (i, h)),
        scratch_shapes=[pltpu.VMEM((blk, width), F32), pltpu.VMEM((heads, blk, 1), F32)],
        compiler_params=_params(("parallel", "arbitrary")),
        name="sb_attention",
    )(qkv, qkv, qkv)


def _compress_kernel(x_ref, pos_ref, w1_ref, w2_ref, o_ref):
    n = x_ref.shape[1]
    half = CMP_STRIDE * HEAD_DIM
    x = x_ref[0].astype(BF16)
    w1 = w1_ref[0].astype(BF16)
    a = _dot(x, w1[:half])
    b = _dot(x, w1[half:])
    pos = _dot(pos_ref[0].astype(BF16), w1)
    b_next = pltpu.roll(b, n - 1, 0)
    hid = jax.nn.gelu(a + b_next + pos[0:1])
    out = _dot(hid.astype(BF16), w2_ref[0].astype(BF16))
    valid = lax.broadcasted_iota(jnp.int32, out.shape, 0) < n - 1
    o_ref[0] = jnp.where(valid, out, 0.0).astype(o_ref.dtype)


def _compress(x_tok, pos_emb, w1, w2):
    S, T, _ = x_tok.shape
    n = T // CMP_STRIDE
    width = CMP_STRIDE * HEAD_DIM
    hidden = w1.shape[-1]
    x2 = x_tok.reshape(S, n, width)
    pos_flat = jnp.broadcast_to(pos_emb.reshape(S, 1, 2 * width), (S, SUBLANES, 2 * width))
    return pl.pallas_call(
        _compress_kernel,
        out_shape=jax.ShapeDtypeStruct((S, n, HEAD_DIM), BF16),
        grid=(S,),
        in_specs=[pl.BlockSpec((1, n, width), lambda s: (s, 0, 0)),
                  pl.BlockSpec((1, SUBLANES, 2 * width), lambda s: (s, 0, 0)),
                  pl.BlockSpec((1, 2 * width, hidden), lambda s: (s, 0, 0)),
                  pl.BlockSpec((1, hidden, HEAD_DIM), lambda s: (s, 0, 0))],
        out_specs=pl.BlockSpec((1, n, HEAD_DIM), lambda s: (s, 0, 0)),
        compiler_params=_params(("parallel",)),
        name="nsa_compress",
    )(x2, pos_flat, w1, w2)


def _nsa_kernel(q_ref, kc_ref, vc_ref, ks_ref, vs_ref, kw_ref, vw_ref, gate_ref, c2s_t_ref, o_ref,
                m_ref, acc_ref, *, blk, tk):
    R = NSA_REP
    qi = pl.program_id(1)
    q0 = qi * blk
    n_cmp = kc_ref.shape[1]
    n_sel = c2s_t_ref.shape[0]
    rows = [slice(r * blk, (r + 1) * blk) for r in range(R)]
    qs = jnp.concatenate([q_ref[:, r * HEAD_DIM:(r + 1) * HEAD_DIM] for r in range(R)], axis=0)
    q_pos = q0 + lax.broadcasted_iota(jnp.int32, (blk, 1), 0)
    q_lane = q0 + lax.broadcasted_iota(jnp.int32, (1, blk), 1)

    def masked_softmax(s, mask, axis):
        s = jnp.where(mask, s, MASK_FILL)
        p = jnp.where(mask, jnp.exp(s - jnp.max(s, axis=axis, keepdims=True)), 0.0)
        return p / jnp.maximum(jnp.sum(p, axis=axis, keepdims=True), 1e-30)

    kc = kc_ref[0]
    s_all = _dot_nt(qs, kc)
    cmp_end = lax.broadcasted_iota(jnp.int32, (1, n_cmp), 1) * CMP_STRIDE + (CMP_BLOCK - 1)
    mask_c = cmp_end <= q_pos
    p_cmp = [masked_softmax(s_all[rows[r]], mask_c, 1).astype(BF16) for r in range(R)]
    o_cmp = _dot(jnp.concatenate(p_cmp, axis=0), vc_ref[0])

    st_all = _dot_nt(kc, qs)
    cmp_end_t = lax.broadcasted_iota(jnp.int32, (n_cmp, 1), 0) * CMP_STRIDE + (CMP_BLOCK - 1)
    mask_t = cmp_end_t <= q_lane
    p_sum = masked_softmax(st_all[:, rows[0]], mask_t, 0)
    for r in range(1, R):
        p_sum = p_sum + masked_softmax(st_all[:, rows[r]], mask_t, 0)
    hi = p_sum.astype(BF16)
    lo = (p_sum - hi.astype(F32)).astype(BF16)
    imp = _dot(c2s_t_ref[...], hi) + _dot(c2s_t_ref[...], lo)
    blk_id = lax.broadcasted_iota(jnp.int32, (n_sel, 1), 0)
    forced = (blk_id == 0) | (blk_id == (q_lane >> SEL_SHIFT))
    valid = blk_id * SEL_BLOCK <= q_lane
    work = jnp.where(forced, jnp.inf, jnp.where(valid, imp, -jnp.inf))
    blk_idf = blk_id.astype(F32)
    sel = jnp.zeros((n_sel, blk), F32)
    for _ in range(min(SEL_TOPK, n_sel)):
        free = sel == 0.0
        top = jnp.max(jnp.where(free, work, -jnp.inf), axis=0, keepdims=True)
        cand = jnp.where(free & (work == top), blk_idf, float(n_sel))
        sel = jnp.where(blk_idf == jnp.min(cand, axis=0, keepdims=True), 1.0, sel)
    sel_bias = jnp.where(sel > 0.0, 0.0, MASK_FILL).T.astype(BF16)

    n_chain = 2
    hpc = R // n_chain
    chains = [slice(c * hpc * blk, (c + 1) * hpc * blk) for c in range(n_chain)]

    def attend(k, v1, bias, online):
        s = [_dot_nt(qs[c], k) for c in chains]
        for ci, c in enumerate(chains):
            probs, alphas = [], []
            for r in range(hpc):
                rs = slice(c.start + r * blk, c.start + (r + 1) * blk)
                sc = s[ci][r * blk:(r + 1) * blk] + bias
                m_new = jnp.max(sc, axis=-1, keepdims=True)
                if online:
                    m_old = m_ref[rs]
                    m_new = jnp.maximum(m_old, m_new)
                    m_ref[rs] = m_new
                    alphas.append(jnp.exp(m_old - m_new))
                probs.append(jnp.exp(sc - m_new).astype(BF16))
            pv = _dot(jnp.concatenate(probs, axis=0), v1)
            if online:
                acc_ref[c] = jnp.concatenate(alphas, axis=0) * acc_ref[c] + pv
            else:
                acc_ref[c] = pv

    def attended():
        return acc_ref[:, :HEAD_DIM] / acc_ref[:, HEAD_DIM:]

    tok = lax.broadcasted_iota(jnp.int32, (n_sel, tk), 1)
    sel_row = lax.broadcasted_iota(jnp.int32, (n_sel, tk), 0)
    key_off = lax.broadcasted_iota(jnp.int32, (1, tk), 1)

    def sel_step(kb, diagonal):
        ks = pl.multiple_of(kb * tk, tk)
        expand = jnp.where(((ks + tok) >> SEL_SHIFT) == sel_row, 1.0, 0.0).astype(BF16)
        bias = _dot(sel_bias, expand)
        if diagonal:
            bias = jnp.where(ks + key_off <= q_pos, bias, MASK_FILL)
        attend(ks_ref[pl.ds(ks, tk), :], vs_ref[pl.ds(ks, tk), :], bias, True)

    m_ref[...] = jnp.full_like(m_ref, MASK_FILL)
    acc_ref[...] = jnp.zeros_like(acc_ref)
    kb_diag = qi // (tk // blk)

    def sel_body(kb, carry):
        sel_step(kb, False)
        return carry

    lax.fori_loop(0, kb_diag, sel_body, 0)
    sel_step(kb_diag, True)
    o_sel = attended()

    wk = WINDOW + blk
    ws = pl.multiple_of(jnp.maximum(q0 - WINDOW, 0), blk)
    dist = q_pos - (ws + lax.broadcasted_iota(jnp.int32, (1, wk), 1))
    bias_w = jnp.where((dist >= 0) & (dist < WINDOW), 0.0, MASK_FILL)
    attend(kw_ref[pl.ds(ws, wk), :], vw_ref[pl.ds(ws, wk), :], bias_w, False)
    o_win = attended()

    gates = jax.nn.sigmoid(gate_ref[...])
    outs = []
    for r in range(R):
        g = [gates[:, r * N_GATES + j:r * N_GATES + j + 1] for j in range(N_GATES)]
        outs.append(g[0] * o_cmp[rows[r]] + g[1] * o_sel[rows[r]] + g[2] * o_win[rows[r]])
    o_ref[...] = jnp.concatenate(outs, axis=1)


def _nsa_attention(q_tok, v_aug, k_cmp, v_cmp, gates, ks_col, kw_col, vs_col, vw_col, gate_col,
                   blk=256, tk=1024):
    T = q_tok.shape[0]
    G = NSA_KV_GROUPS
    blk = min(blk, T)
    tk = min(tk, T)
    assert T % tk == 0 and tk % blk == 0 and WINDOW % blk == 0 and WINDOW + blk <= T
    n_cmp = T // CMP_STRIDE
    n_sel = T // SEL_BLOCK
    cmp_start = np.arange(n_cmp)[None, :] * CMP_STRIDE
    sel_start = np.arange(n_sel)[:, None] * SEL_BLOCK
    overlap = np.clip(np.minimum(cmp_start + CMP_BLOCK, sel_start + SEL_BLOCK)
                      - np.maximum(cmp_start, sel_start), 0, None)
    c2s_t = jnp.asarray(overlap.astype(np.float32) / CMP_BLOCK, dtype=BF16)
    width = NSA_REP * HEAD_DIM
    key_spec = lambda col: pl.BlockSpec((T, HEAD_DIM), lambda g, i: (0, col + g))
    val_spec = lambda col: pl.BlockSpec((T, 2 * HEAD_DIM), lambda g, i: (0, col + g))
    cmp_spec = pl.BlockSpec((1, n_cmp, HEAD_DIM), lambda g, i: (g, 0, 0))
    return pl.pallas_call(
        functools.partial(_nsa_kernel, blk=blk, tk=tk),
        out_shape=jax.ShapeDtypeStruct((T, G * width), F32),
        grid=(G, T // blk),
        in_specs=[pl.BlockSpec((blk, width), lambda g, i: (i, g)),
                  cmp_spec, cmp_spec,
                  key_spec(ks_col), val_spec(vs_col), key_spec(kw_col), val_spec(vw_col),
                  pl.BlockSpec((blk, LANES), lambda g, i: (i, gate_col + g)),
                  pl.BlockSpec((n_sel, n_cmp), lambda g, i: (0, 0))],
        out_specs=pl.BlockSpec((blk, width), lambda g, i: (i, g)),
        scratch_shapes=[pltpu.VMEM((NSA_REP * blk, 1), F32),
                        pltpu.VMEM((NSA_REP * blk, 2 * HEAD_DIM), F32)],
        compiler_params=_params(("parallel", "arbitrary")),
        name="nsa_attention",
    )(q_tok, k_cmp, v_cmp, q_tok, v_aug, q_tok, v_aug, gates, c2s_t)


def _group_rms_kernel(a_ref, b_ref, g_ref, o_ref):
    wa = a_ref.shape[1]

    def norm(o, g):
        return o * lax.rsqrt(jnp.mean(o * o, axis=-1, keepdims=True) + RMS_EPS) * g

    g = g_ref[...]
    o_ref[:, :wa] = norm(a_ref[...], g[:, :wa]).astype(o_ref.dtype)
    o_ref[:, wa:] = norm(b_ref[...], g[:, wa:]).astype(o_ref.dtype)


def _group_rms(o_a, o_b, g, tm=512):
    T, wa = o_a.shape
    wb = o_b.shape[1]
    tm = min(tm, T)
    return pl.pallas_call(
        _group_rms_kernel,
        out_shape=jax.ShapeDtypeStruct((T, wa + wb), BF16),
        grid=(T // tm,),
        in_specs=[pl.BlockSpec((tm, wa), lambda i: (i, 0)),
                  pl.BlockSpec((tm, wb), lambda i: (i, 0)),
                  pl.BlockSpec((1, wa + wb), lambda i: (0, 0))],
        out_specs=pl.BlockSpec((tm, wa + wb), lambda i: (i, 0)),
        compiler_params=_params(("parallel",)),
        name="group_rms",
    )(o_a, o_b, g.reshape(1, wa + wb))


def _cross_kernel(q_ref, k_ref, v_ref, o_ref):
    s = _dot_nt(q_ref[...], k_ref[...])
    p = jnp.exp(s - jnp.max(s, axis=-1, keepdims=True))
    p = p / jnp.sum(p, axis=-1, keepdims=True)
    o_ref[...] = _dot(p.astype(BF16), v_ref[...]).astype(o_ref.dtype)


def _cross_attention(q, kv, n_heads, tm=1024):
    T, D = q.shape
    M = kv.shape[0]
    dh = D // n_heads
    tm = min(tm, T)
    return pl.pallas_call(
        _cross_kernel,
        out_shape=jax.ShapeDtypeStruct((T, D), BF16),
        grid=(T // tm, n_heads),
        in_specs=[pl.BlockSpec((tm, dh), lambda i, h: (i, h)),
                  pl.BlockSpec((M, dh), lambda i, h: (0, h)),
                  pl.BlockSpec((M, dh), lambda i, h: (0, n_heads + h))],
        out_specs=pl.BlockSpec((tm, dh), lambda i, h: (i, h)),
        compiler_params=_params(("parallel", "arbitrary")),
        name="cross_attention",
    )(q, kv, kv)


def _rope_tables(T):
    half = HEAD_DIM // 2
    inv_freq = ROPE_THETA ** (-jnp.arange(half, dtype=F32) / half)
    ang = jnp.arange(T).astype(F32)[:, None] * inv_freq[None, :]
    cos, sin = jnp.cos(ang), jnp.sin(ang)
    return jnp.concatenate([cos, cos], axis=1), jnp.concatenate([-sin, sin], axis=1)


def _pad_to(w, axis, mult):
    pad = -w.shape[axis] % mult
    if not pad:
        return w
    widths = [(0, 0)] * w.ndim
    widths[axis] = (0, pad)
    return jnp.pad(w, widths)


def _ffn(x_f32, x_bf16, w_gate, w_up, w_down, ln_g, ln_b, alpha, emit_bf16):
    wg = _pad_to(w_gate, 1, FF_PAD).astype(BF16)
    wu = _pad_to(w_up, 1, FF_PAD).astype(BF16)
    wd = _pad_to(w_down, 0, FF_PAD).astype(BF16)
    h = _swiglu_up(x_bf16, wg, wu)
    return _mm_res_ln(h, wd, x_f32, ln_g, ln_b, alpha, 0.5, emit_bf16)


def _mixer(x_f32, x_bf16, w_in, cmp_pos_k, cmp_w1_k, cmp_w2_k, cmp_pos_v, cmp_w1_v, cmp_w2_v,
           mix_norm_g, w_out, ln_g, ln_b, alpha):
    T = x_f32.shape[0]
    G = NSA_KV_GROUPS
    sbw = SB_HEADS * HEAD_DIM
    nqw = NSA_HEADS * HEAD_DIM
    kvw = G * HEAD_DIM
    bounds = np.cumsum([0, sbw, sbw, sbw, nqw, kvw, kvw, kvw, kvw, kvw, kvw, NSA_HEADS * N_GATES])
    (w_sbq, w_sbk, w_sbv, w_nq, w_kc, w_vc, w_ks, w_vs, w_kw, w_vw, w_gate) = [
        w_in[:, bounds[i]:bounds[i + 1]] for i in range(11)]
    per_group = NSA_REP * N_GATES
    w_gate = jnp.concatenate([_pad_to(w_gate[:, g * per_group:(g + 1) * per_group], 1, LANES)
                              for g in range(G)], axis=1)
    tables = _rope_tables(T)
    cat = lambda ws: jnp.concatenate(ws, axis=1).astype(BF16)
    q_scale = lambda n_q, n_rest: jnp.concatenate([jnp.full((n_q,), HEAD_DIM ** -0.5, F32), jnp.ones((n_rest,), F32)])
    plain = _project(x_bf16, cat([w_sbq, w_sbk, w_sbv, w_vs, w_vw]), BF16, 1024, 512,
                     col_scale=q_scale(sbw, 2 * sbw + 2 * kvw))
    roped = _project(x_bf16, cat([w_nq, w_ks, w_kw]), BF16, 1024, 512,
                     col_scale=q_scale(nqw, 2 * kvw), rope_tables=tables)
    kc = _project(x_bf16, w_kc.astype(BF16), F32, 1024, kvw, rope_tables=tables)
    vc_gate = _project(x_bf16, cat([w_vc, w_gate]), F32, 1024, 2 * kvw)

    o_sb = _sb_attention(plain, SB_HEADS, 0, SB_HEADS, 2 * SB_HEADS)

    head = lambda a, g: a[:, g * HEAD_DIM:(g + 1) * HEAD_DIM]
    streams = jnp.stack([head(kc, g) for g in range(G)] + [head(vc_gate, g) for g in range(G)])
    rep = lambda a, b: jnp.stack([a] * G + [b] * G)
    cmp = _compress(streams, rep(cmp_pos_k, cmp_pos_v), rep(cmp_w1_k, cmp_w1_v), rep(cmp_w2_k, cmp_w2_v))
    ones = jnp.ones((T, HEAD_DIM), BF16)
    v_aug = jnp.concatenate([a for h in range(2 * G) for a in (head(plain, 3 * SB_HEADS + h), ones)], axis=1)
    o_nsa = _nsa_attention(roped, v_aug, cmp[:G], cmp[G:], vc_gate,
                           ks_col=NSA_HEADS, kw_col=NSA_HEADS + G, vs_col=0, vw_col=G, gate_col=G)
    y = _group_rms(o_sb, o_nsa, mix_norm_g)
    return _mm_res_ln(y, w_out.astype(BF16), x_f32, ln_g, ln_b, alpha, 1.0, True)


def _memory_block(x_f32, x_bf16, mem, w_q, w_k, w_v, w_o, ln_g, ln_b, alpha):
    D = w_q.shape[1]
    q = _project(x_bf16, w_q.astype(BF16), BF16, 1024, 1024,
                 col_scale=jnp.full((D,), (D // MEM_HEADS) ** -0.5, F32))
    kv = _project(mem.astype(BF16), jnp.concatenate([w_k, w_v], axis=1).astype(BF16), BF16, 256, 1024)
    o = _cross_attention(q, kv, MEM_HEADS)
    return _mm_res_ln(o, w_o.astype(BF16), x_f32, ln_g, ln_b, alpha, 1.0, True)


def kernel(x, mem, ln1_g, ln1_b, ffn1_gate, ffn1_up, ffn1_down, w_in, cmp_pos_k, cmp_w1_k, cmp_w2_k, cmp_pos_v, cmp_w1_v, cmp_w2_v, mix_norm_g, w_out, ln2_g, ln2_b, mem_wq, mem_wk, mem_wv, mem_wo, ln3_g, ln3_b, ffn2_gate, ffn2_up, ffn2_down, ln4_g, ln4_b):
    n_layers = ffn1_gate.shape[0]
    alpha = (2 * n_layers) ** 0.25
    outs = []
    for bi in range(x.shape[0]):
        xf = x[bi]
        xb = xf.astype(BF16)
        for l in range(n_layers):
            xf, xb = _ffn(xf, xb, ffn1_gate[l], ffn1_up[l], ffn1_down[l], ln1_g[l], ln1_b[l], alpha, True)
            xf, xb = _mixer(xf, xb, w_in[l], cmp_pos_k[l], cmp_w1_k[l], cmp_w2_k[l], cmp_pos_v[l], cmp_w1_v[l],
                            cmp_w2_v[l], mix_norm_g[l], w_out[l], ln2_g[l], ln2_b[l], alpha)
            xf, xb = _memory_block(xf, xb, mem[bi], mem_wq[l], mem_wk[l], mem_wv[l], mem_wo[l], ln3_g[l], ln3_b[l],
                                   alpha)
            xf, xb = _ffn(xf, xb, ffn2_gate[l], ffn2_up[l], ffn2_down[l], ln4_g[l], ln4_b[l], alpha,
                          l + 1 < n_layers)
        outs.append(xf)
    return jnp.stack(outs)
```

```python
import functools

import numpy as np
import jax
import jax.numpy as jnp
from jax import lax
from jax.experimental import pallas as pl
from jax.experimental.pallas import tpu as pltpu

HEAD_DIM = 128
SB_HEADS = 8
NSA_HEADS = 8
NSA_KV_GROUPS = 2
NSA_REP = NSA_HEADS // NSA_KV_GROUPS
N_GATES = 3
CMP_BLOCK = 32
CMP_STRIDE = 16
SEL_BLOCK = 64
SEL_SHIFT = 6
SEL_TOPK = 16
WINDOW = 512
MEM_HEADS = 4
ROPE_THETA = 10000.0
LN_EPS = 1e-5
RMS_EPS = 1e-6
MASK_FILL = -1e30
LOG2E = 1.4426950408889634
assert 1 << SEL_SHIFT == SEL_BLOCK

LANES = 128
SUBLANES = 8
FF_DOWN_STEPS = 4
FF_PAD = 512
assert FF_PAD % (FF_DOWN_STEPS * LANES) == 0
VMEM_LIMIT = 56 * 1024 * 1024

BF16 = jnp.bfloat16
F32 = jnp.float32


def _params(sem):
    return pltpu.CompilerParams(dimension_semantics=sem, vmem_limit_bytes=VMEM_LIMIT)


def _dot(a, b):
    return jnp.dot(a, b, preferred_element_type=F32)


def _dot_nt(a, b):
    return lax.dot_general(a, b, (((1,), (1,)), ((), ())), preferred_element_type=F32)


def _split_dot(a, b):
    hi = a.astype(BF16)
    lo = (a - hi.astype(F32)).astype(BF16)
    return _dot(hi, b) + _dot(lo, b)


def _layer_norm(z, g, b):
    mu = jnp.mean(z, axis=-1, keepdims=True)
    zc = z - mu
    var = jnp.mean(zc * zc, axis=-1, keepdims=True)
    return zc * lax.rsqrt(var + LN_EPS) * g + b


def _proj_kernel(x_ref, w_ref, *rest, scaled, rope):
    rest = list(rest)
    o_ref = rest.pop()
    y = _dot(x_ref[...], w_ref[...])
    if scaled:
        y = y * rest.pop(0)[...]
    if rope:
        cos_ref, sin_ref = rest
        c = cos_ref[...]
        s = sin_ref[...]
        heads = []
        for h in range(y.shape[1] // HEAD_DIM):
            yh = y[:, h * HEAD_DIM:(h + 1) * HEAD_DIM]
            heads.append(yh * c + pltpu.roll(yh, HEAD_DIM // 2, 1) * s)
        y = jnp.concatenate(heads, axis=1) if len(heads) > 1 else heads[0]
    o_ref[...] = y.astype(o_ref.dtype)


def _project(x, w, out_dtype, tm, tn, col_scale=None, rope_tables=None):
    M, K = x.shape
    N = w.shape[1]
    tm = min(tm, M)
    assert M % tm == 0 and N % tn == 0
    in_specs = [pl.BlockSpec((tm, K), lambda i, j: (i, 0)),
                pl.BlockSpec((K, tn), lambda i, j: (0, j))]
    args = [x, w]
    if col_scale is not None:
        in_specs.append(pl.BlockSpec((1, tn), lambda i, j: (0, j)))
        args.append(col_scale.reshape(1, N))
    if rope_tables is not None:
        in_specs += [pl.BlockSpec((tm, HEAD_DIM), lambda i, j: (i, 0))] * 2
        args += list(rope_tables)
    return pl.pallas_call(
        functools.partial(_proj_kernel, scaled=col_scale is not None, rope=rope_tables is not None),
        out_shape=jax.ShapeDtypeStruct((M, N), out_dtype),
        grid=(M // tm, N // tn),
        in_specs=in_specs,
        out_specs=pl.BlockSpec((tm, tn), lambda i, j: (i, j)),
        compiler_params=_params(("parallel", "arbitrary")),
        name="proj_rope" if rope_tables is not None else "proj",
    )(*args)


def _swiglu_up_kernel(x_ref, wg_ref, wu_ref, o_ref):
    x = x_ref[...]
    g = _dot(x, wg_ref[...])
    u = _dot(x, wu_ref[...])
    o_ref[...] = (jax.nn.silu(g) * u).astype(o_ref.dtype)


def _swiglu_up(x, wg, wu, tm=1024, tn=512):
    M, K = x.shape
    N = wg.shape[1]
    tm = min(tm, M)
    assert M % tm == 0 and N % tn == 0
    return pl.pallas_call(
        _swiglu_up_kernel,
        out_shape=jax.ShapeDtypeStruct((M, N), BF16),
        grid=(M // tm, N // tn),
        in_specs=[pl.BlockSpec((tm, K), lambda i, j: (i, 0)),
                  pl.BlockSpec((K, tn), lambda i, j: (0, j)),
                  pl.BlockSpec((K, tn), lambda i, j: (0, j))],
        out_specs=pl.BlockSpec((tm, tn), lambda i, j: (i, j)),
        compiler_params=_params(("parallel", "arbitrary")),
        name="swiglu_up",
    )(x, wg, wu)


def _mm_res_ln_kernel(h_ref, w_ref, res_ref, g_ref, b_ref, *rest, alpha, coef, emit_bf16, n_k, n_split):
    rest = list(rest)
    of_ref = rest.pop(0)
    ob_ref = rest.pop(0) if emit_bf16 else None
    acc_ref = rest.pop(0) if n_k > 1 else None
    tm = h_ref.shape[0]
    chunks = [slice(c * tm // n_split, (c + 1) * tm // n_split) for c in range(n_split)]

    def partial_products():
        return [_dot(h_ref[c, :], w_ref[...]) for c in chunks]

    def finish(totals):
        for c, y in zip(chunks, totals):
            z = alpha * res_ref[c, :] + coef * y
            out = _layer_norm(z, g_ref[...], b_ref[...])
            of_ref[c, :] = out
            if emit_bf16:
                ob_ref[c, :] = out.astype(BF16)

    if n_k == 1:
        finish(partial_products())
        return
    k = pl.program_id(1)

    @pl.when(k == 0)
    def _():
        for c, y in zip(chunks, partial_products()):
            acc_ref[c, :] = y

    @pl.when((k > 0) & (k < n_k - 1))
    def _():
        for c, y in zip(chunks, partial_products()):
            acc_ref[c, :] += y

    @pl.when(k == n_k - 1)
    def _():
        finish([acc_ref[c, :] + y for c, y in zip(chunks, partial_products())])


def _mm_res_ln(h, w, res, g, b, alpha, coef, emit_bf16, tm=512, n_k=1, n_split=2):
    M, K = h.shape
    N = w.shape[1]
    tm = min(tm, M)
    tk = K // n_k
    assert M % tm == 0 and K % n_k == 0 and (n_k == 1 or tk % LANES == 0)
    out_shape = [jax.ShapeDtypeStruct((M, N), F32)]
    out_specs = [pl.BlockSpec((tm, N), lambda i, k: (i, 0))]
    if emit_bf16:
        out_shape.append(jax.ShapeDtypeStruct((M, N), BF16))
        out_specs.append(pl.BlockSpec((tm, N), lambda i, k: (i, 0)))
    out = pl.pallas_call(
        functools.partial(_mm_res_ln_kernel, alpha=alpha, coef=coef, emit_bf16=emit_bf16,
                          n_k=n_k, n_split=n_split),
        out_shape=out_shape,
        grid=(M // tm, n_k),
        in_specs=[pl.BlockSpec((tm, tk), lambda i, k: (i, k)),
                  pl.BlockSpec((tk, N), lambda i, k: (k, 0)),
                  pl.BlockSpec((tm, N), lambda i, k: (i, 0)),
                  pl.BlockSpec((1, N), lambda i, k: (0, 0)),
                  pl.BlockSpec((1, N), lambda i, k: (0, 0))],
        out_specs=out_specs,
        scratch_shapes=[pltpu.VMEM((tm, N), F32)] if n_k > 1 else [],
        compiler_params=_params(("parallel", "arbitrary")),
        name="mm_res_ln",
    )(h, w, res, g.reshape(1, N), b.reshape(1, N))
    return (out[0], out[1]) if emit_bf16 else (out[0], None)


def _sb_kernel(q_ref, k_ref, v_ref, o_ref, acc_ref, c_ref, *, blk, heads):
    qi = pl.program_id(1)
    row = lax.broadcasted_iota(jnp.int32, (blk, blk), 0)
    col = lax.broadcasted_iota(jnp.int32, (blk, blk), 1)
    suffix = jnp.where(row > col, 1.0, 0.0).astype(BF16)
    strict = col < row

    cols = [slice(h * HEAD_DIM, (h + 1) * HEAD_DIM) for h in range(heads)]

    def step(kb_far, n_blocks, diagonal):
        ks = pl.multiple_of(kb_far * blk, blk)
        key_rows = [pl.ds(ks + b * blk, blk) for b in range(n_blocks)]
        pairs = [(b, h) for b in reversed(range(n_blocks)) for h in range(heads)]
        z = {p: _dot_nt(q_ref[:, cols[p[1]]], k_ref[key_rows[p[0]], cols[p[1]]]) for p in pairs}
        log_beta, log_1m = {}, {}
        for p in pairs:
            y = z[p]
            lb = jnp.minimum(y, 0.0) - jnp.log(1.0 + jnp.exp2(jnp.abs(y) * -LOG2E))
            l1 = lb - y
            if diagonal:
                l1 = jnp.where(strict, l1, 0.0)
            log_beta[p] = lb
            log_1m[p] = l1.astype(BF16)
        log_stay = {p: _dot(log_1m[p], suffix) for p in pairs}
        weights = {}
        for h in range(heads):
            c = c_ref[h]
            for b in reversed(range(n_blocks)):
                p = (b, h)
                w = jnp.exp(log_beta[p] + (log_stay[p] + c))
                if diagonal:
                    w = jnp.where(strict, w, 0.0)
                weights[p] = w.astype(BF16)
                c = c + (log_stay[p][:, :1] + log_1m[p][:, :1].astype(F32))
            c_ref[h] = c
        for h in range(heads):
            w = jnp.concatenate([weights[(b, h)] for b in range(n_blocks)], axis=1)
            acc_ref[:, cols[h]] += _dot(w, v_ref[pl.ds(ks, n_blocks * blk), cols[h]])

    acc_ref[...] = jnp.zeros_like(acc_ref)
    c_ref[...] = jnp.zeros_like(c_ref)
    step(qi, 1, True)

    @pl.when(qi % 2 == 1)
    def _():
        step(qi - 1, 1, False)

    def body(it, carry):
        step(qi - (qi % 2) - 2 * (it + 1), 2, False)
        return carry

    lax.fori_loop(0, qi // 2, body, 0)
    o_ref[...] = acc_ref[...]


def _sb_attention(qkv, n_heads, q_col, k_col, v_col, blk=256, heads=4):
    T = qkv.shape[0]
    blk = min(blk, T)
    width = heads * HEAD_DIM
    assert T % blk == 0 and n_heads % heads == 0
    assert q_col % heads == 0 and k_col % heads == 0 and v_col % heads == 0
    return pl.pallas_call(
        functools.partial(_sb_kernel, blk=blk, heads=heads),
        out_shape=jax.ShapeDtypeStruct((T, n_heads * HEAD_DIM), F32),
        grid=(n_heads // heads, T // blk),
        in_specs=[pl.BlockSpec((blk, width), lambda h, i: (i, q_col // heads + h)),
                  pl.BlockSpec((T, width), lambda h, i: (0, k_col // heads + h)),
                  pl.BlockSpec((T, width), lambda h, i: (0, v_col // heads + h))],
        out_specs=pl.BlockSpec((blk, width), lambda h, i: (i, h)),
        scratch_shapes=[pltpu.VMEM((blk, width), F32), pltpu.VMEM((heads, blk, 1), F32)],
        compiler_params=_params(("parallel", "arbitrary")),
        name="sb_attention",
    )(qkv, qkv, qkv)


def _compress_kernel(x_ref, pos_ref, w1_ref, w2_ref, o_ref):
    n = x_ref.shape[1]
    half = CMP_STRIDE * HEAD_DIM
    x = x_ref[0].astype(BF16)
    w1 = w1_ref[0].astype(BF16)
    a = _dot(x, w1[:half])
    b = _dot(x, w1[half:])
    pos = _dot(pos_ref[0].astype(BF16), w1)
    b_next = pltpu.roll(b, n - 1, 0)
    hid = jax.nn.gelu(a + b_next + pos[0:1])
    out = _dot(hid.astype(BF16), w2_ref[0].astype(BF16))
    valid = lax.broadcasted_iota(jnp.int32, out.shape, 0) < n - 1
    o_ref[0] = jnp.where(valid, out, 0.0).astype(o_ref.dtype)


def _compress(x_tok, pos_emb, w1, w2):
    S, T, _ = x_tok.shape
    n = T // CMP_STRIDE
    width = CMP_STRIDE * HEAD_DIM
    hidden = w1.shape[-1]
    x2 = x_tok.reshape(S, n, width)
    pos_flat = jnp.broadcast_to(pos_emb.reshape(S, 1, 2 * width), (S, SUBLANES, 2 * width))
    return pl.pallas_call(
        _compress_kernel,
        out_shape=jax.ShapeDtypeStruct((S, n, HEAD_DIM), BF16),
        grid=(S,),
        in_specs=[pl.BlockSpec((1, n, width), lambda s: (s, 0, 0)),
                  pl.BlockSpec((1, SUBLANES, 2 * width), lambda s: (s, 0, 0)),
                  pl.BlockSpec((1, 2 * width, hidden), lambda s: (s, 0, 0)),
                  pl.BlockSpec((1, hidden, HEAD_DIM), lambda s: (s, 0, 0))],
        out_specs=pl.BlockSpec((1, n, HEAD_DIM), lambda s: (s, 0, 0)),
        compiler_params=_params(("parallel",)),
        name="nsa_compress",
    )(x2, pos_flat, w1, w2)


def _nsa_kernel(q_ref, kc_ref, vc_ref, ks_ref, vs_ref, kw_ref, vw_ref, gate_ref, c2s_t_ref, o_ref,
                m_ref, acc_ref, *, blk, tk):
    R = NSA_REP
    qi = pl.program_id(1)
    q0 = qi * blk
    n_cmp = kc_ref.shape[1]
    n_sel = c2s_t_ref.shape[0]
    rows = [slice(r * blk, (r + 1) * blk) for r in range(R)]
    qs = jnp.concatenate([q_ref[:, r * HEAD_DIM:(r + 1) * HEAD_DIM] for r in range(R)], axis=0)
    q_pos = q0 + lax.broadcasted_iota(jnp.int32, (blk, 1), 0)
    q_lane = q0 + lax.broadcasted_iota(jnp.int32, (1, blk), 1)

    def masked_softmax(s, mask, axis):
        s = jnp.where(mask, s, MASK_FILL)
        p = jnp.where(mask, jnp.exp(s - jnp.max(s, axis=axis, keepdims=True)), 0.0)
        return p / jnp.maximum(jnp.sum(p, axis=axis, keepdims=True), 1e-30)

    kc = kc_ref[0]
    s_all = _dot_nt(qs, kc)
    cmp_end = lax.broadcasted_iota(jnp.int32, (1, n_cmp), 1) * CMP_STRIDE + (CMP_BLOCK - 1)
    mask_c = cmp_end <= q_pos
    p_cmp = [masked_softmax(s_all[rows[r]], mask_c, 1).astype(BF16) for r in range(R)]
    o_cmp = _dot(jnp.concatenate(p_cmp, axis=0), vc_ref[0])

    st_all = _dot_nt(kc, qs)
    cmp_end_t = lax.broadcasted_iota(jnp.int32, (n_cmp, 1), 0) * CMP_STRIDE + (CMP_BLOCK - 1)
    mask_t = cmp_end_t <= q_lane
    p_sum = masked_softmax(st_all[:, rows[0]], mask_t, 0)
    for r in range(1, R):
        p_sum = p_sum + masked_softmax(st_all[:, rows[r]], mask_t, 0)
    hi = p_sum.astype(BF16)
    lo = (p_sum - hi.astype(F32)).astype(BF16)
    imp = _dot(c2s_t_ref[...], hi) + _dot(c2s_t_ref[...], lo)
    blk_id = lax.broadcasted_iota(jnp.int32, (n_sel, 1), 0)
    forced = (blk_id == 0) | (blk_id == (q_lane >> SEL_SHIFT))
    valid = blk_id * SEL_BLOCK <= q_lane
    work = jnp.where(forced, jnp.inf, jnp.where(valid, imp, -jnp.inf))
    blk_idf = blk_id.astype(F32)
    sel = jnp.zeros((n_sel, blk), F32)
    for _ in range(min(SEL_TOPK, n_sel)):
        free = sel == 0.0
        top = jnp.max(jnp.where(free, work, -jnp.inf), axis=0, keepdims=True)
        cand = jnp.where(free & (work == top), blk_idf, float(n_sel))
        sel = jnp.where(blk_idf == jnp.min(cand, axis=0, keepdims=True), 1.0, sel)
    sel_bias = jnp.where(sel > 0.0, 0.0, MASK_FILL).T.astype(BF16)

    n_chain = 4
    hpc = R // n_chain
    chains = [slice(c * hpc * blk, (c + 1) * hpc * blk) for c in range(n_chain)]

    def attend(k, v1, bias, online):
        s = [_dot_nt(qs[c], k) for c in chains]
        for ci, c in enumerate(chains):
            probs, alphas = [], []
            for r in range(hpc):
                rs = slice(c.start + r * blk, c.start + (r + 1) * blk)
                sc = s[ci][r * blk:(r + 1) * blk] + bias
                m_new = jnp.max(sc, axis=-1, keepdims=True)
                if online:
                    m_old = m_ref[rs]
                    m_new = jnp.maximum(m_old, m_new)
                    m_ref[rs] = m_new
                    alphas.append(jnp.exp(m_old - m_new))
                probs.append(jnp.exp(sc - m_new).astype(BF16))
            pv = _dot(jnp.concatenate(probs, axis=0), v1)
            if online:
                acc_ref[c] = jnp.concatenate(alphas, axis=0) * acc_ref[c] + pv
            else:
                acc_ref[c] = pv

    def attended():
        return acc_ref[:, :HEAD_DIM] / acc_ref[:, HEAD_DIM:]

    tok = lax.broadcasted_iota(jnp.int32, (n_sel, tk), 1)
    sel_row = lax.broadcasted_iota(jnp.int32, (n_sel, tk), 0)
    key_off = lax.broadcasted_iota(jnp.int32, (1, tk), 1)

    def sel_step(kb, diagonal):
        ks = pl.multiple_of(kb * tk, tk)
        expand = jnp.where(((ks + tok) >> SEL_SHIFT) == sel_row, 1.0, 0.0).astype(BF16)
        bias = _dot(sel_bias, expand)
        if diagonal:
            bias = jnp.where(ks + key_off <= q_pos, bias, MASK_FILL)
        attend(ks_ref[pl.ds(ks, tk), :], vs_ref[pl.ds(ks, tk), :], bias, True)

    m_ref[...] = jnp.full_like(m_ref, MASK_FILL)
    acc_ref[...] = jnp.zeros_like(acc_ref)
    kb_diag = qi // (tk // blk)

    def sel_body(kb, carry):
        sel_step(kb, False)
        return carry

    lax.fori_loop(0, kb_diag, sel_body, 0)
    sel_step(kb_diag, True)
    o_sel = attended()

    wk = WINDOW + blk
    ws = pl.multiple_of(jnp.maximum(q0 - WINDOW, 0), blk)
    dist = q_pos - (ws + lax.broadcasted_iota(jnp.int32, (1, wk), 1))
    bias_w = jnp.where((dist >= 0) & (dist < WINDOW), 0.0, MASK_FILL)
    attend(kw_ref[pl.ds(ws, wk), :], vw_ref[pl.ds(ws, wk), :], bias_w, False)
    o_win = attended()

    gates = jax.nn.sigmoid(gate_ref[...])
    outs = []
    for r in range(R):
        g = [gates[:, r * N_GATES + j:r * N_GATES + j + 1] for j in range(N_GATES)]
        outs.append(g[0] * o_cmp[rows[r]] + g[1] * o_sel[rows[r]] + g[2] * o_win[rows[r]])
    o_ref[...] = jnp.concatenate(outs, axis=1)


def _nsa_attention(q_tok, v_aug, k_cmp, v_cmp, gates, ks_col, kw_col, vs_col, vw_col, gate_col,
                   blk=256, tk=1024):
    T = q_tok.shape[0]
    G = NSA_KV_GROUPS
    blk = min(blk, T)
    tk = min(tk, T)
    assert T % tk == 0 and tk % blk == 0 and WINDOW % blk == 0 and WINDOW + blk <= T
    n_cmp = T // CMP_STRIDE
    n_sel = T // SEL_BLOCK
    cmp_start = np.arange(n_cmp)[None, :] * CMP_STRIDE
    sel_start = np.arange(n_sel)[:, None] * SEL_BLOCK
    overlap = np.clip(np.minimum(cmp_start + CMP_BLOCK, sel_start + SEL_BLOCK)
                      - np.maximum(cmp_start, sel_start), 0, None)
    c2s_t = jnp.asarray(overlap.astype(np.float32) / CMP_BLOCK, dtype=BF16)
    width = NSA_REP * HEAD_DIM
    key_spec = lambda col: pl.BlockSpec((T, HEAD_DIM), lambda g, i: (0, col + g))
    val_spec = lambda col: pl.BlockSpec((T, 2 * HEAD_DIM), lambda g, i: (0, col + g))
    cmp_spec = pl.BlockSpec((1, n_cmp, HEAD_DIM), lambda g, i: (g, 0, 0))
    return pl.pallas_call(
        functools.partial(_nsa_kernel, blk=blk, tk=tk),
        out_shape=jax.ShapeDtypeStruct((T, G * width), F32),
        grid=(G, T // blk),
        in_specs=[pl.BlockSpec((blk, width), lambda g, i: (i, g)),
                  cmp_spec, cmp_spec,
                  key_spec(ks_col), val_spec(vs_col), key_spec(kw_col), val_spec(vw_col),
                  pl.BlockSpec((blk, LANES), lambda g, i: (i, gate_col + g)),
                  pl.BlockSpec((n_sel, n_cmp), lambda g, i: (0, 0))],
        out_specs=pl.BlockSpec((blk, width), lambda g, i: (i, g)),
        scratch_shapes=[pltpu.VMEM((NSA_REP * blk, 1), F32),
                        pltpu.VMEM((NSA_REP * blk, 2 * HEAD_DIM), F32)],
        compiler_params=_params(("parallel", "arbitrary")),
        name="nsa_attention",
    )(q_tok, k_cmp, v_cmp, q_tok, v_aug, q_tok, v_aug, gates, c2s_t)


def _group_rms_kernel(a_ref, b_ref, g_ref, o_ref):
    wa = a_ref.shape[1]

    def norm(o, g):
        return o * lax.rsqrt(jnp.mean(o * o, axis=-1, keepdims=True) + RMS_EPS) * g

    g = g_ref[...]
    o_ref[:, :wa] = norm(a_ref[...], g[:, :wa]).astype(o_ref.dtype)
    o_ref[:, wa:] = norm(b_ref[...], g[:, wa:]).astype(o_ref.dtype)


def _group_rms(o_a, o_b, g, tm=512):
    T, wa = o_a.shape
    wb = o_b.shape[1]
    tm = min(tm, T)
    return pl.pallas_call(
        _group_rms_kernel,
        out_shape=jax.ShapeDtypeStruct((T, wa + wb), BF16),
        grid=(T // tm,),
        in_specs=[pl.BlockSpec((tm, wa), lambda i: (i, 0)),
                  pl.BlockSpec((tm, wb), lambda i: (i, 0)),
                  pl.BlockSpec((1, wa + wb), lambda i: (0, 0))],
        out_specs=pl.BlockSpec((tm, wa + wb), lambda i: (i, 0)),
        compiler_params=_params(("parallel",)),
        name="group_rms",
    )(o_a, o_b, g.reshape(1, wa + wb))


def _cross_kernel(q_ref, k_ref, v_ref, o_ref):
    s = _dot_nt(q_ref[...], k_ref[...])
    p = jnp.exp(s - jnp.max(s, axis=-1, keepdims=True))
    p = p / jnp.sum(p, axis=-1, keepdims=True)
    o_ref[...] = _dot(p.astype(BF16), v_ref[...]).astype(o_ref.dtype)


def _cross_attention(q, kv, n_heads, tm=1024):
    T, D = q.shape
    M = kv.shape[0]
    dh = D // n_heads
    tm = min(tm, T)
    return pl.pallas_call(
        _cross_kernel,
        out_shape=jax.ShapeDtypeStruct((T, D), BF16),
        grid=(T // tm, n_heads),
        in_specs=[pl.BlockSpec((tm, dh), lambda i, h: (i, h)),
                  pl.BlockSpec((M, dh), lambda i, h: (0, h)),
                  pl.BlockSpec((M, dh), lambda i, h: (0, n_heads + h))],
        out_specs=pl.BlockSpec((tm, dh), lambda i, h: (i, h)),
        compiler_params=_params(("parallel", "arbitrary")),
        name="cross_attention",
    )(q, kv, kv)


def _rope_tables(T):
    half = HEAD_DIM // 2
    inv_freq = ROPE_THETA ** (-jnp.arange(half, dtype=F32) / half)
    ang = jnp.arange(T).astype(F32)[:, None] * inv_freq[None, :]
    cos, sin = jnp.cos(ang), jnp.sin(ang)
    return jnp.concatenate([cos, cos], axis=1), jnp.concatenate([-sin, sin], axis=1)


def _pad_to(w, axis, mult):
    pad = -w.shape[axis] % mult
    if not pad:
        return w
    shape = list(w.shape)
    shape[axis] = pad
    return jnp.concatenate([w, jnp.zeros(shape, w.dtype)], axis=axis)


def _ffn(x_f32, x_bf16, w_gate, w_up, w_down, ln_g, ln_b, alpha, emit_bf16):
    wg = _pad_to(w_gate.astype(BF16), 1, FF_PAD)
    wu = _pad_to(w_up.astype(BF16), 1, FF_PAD)
    wd = _pad_to(w_down.astype(BF16), 0, FF_PAD)
    h = _swiglu_up(x_bf16, wg, wu)
    return _mm_res_ln(h, wd, x_f32, ln_g, ln_b, alpha, 0.5, emit_bf16, n_k=FF_DOWN_STEPS)


def _mixer(x_f32, x_bf16, w_in, cmp_pos_k, cmp_w1_k, cmp_w2_k, cmp_pos_v, cmp_w1_v, cmp_w2_v,
           mix_norm_g, w_out, ln_g, ln_b, alpha):
    T = x_f32.shape[0]
    G = NSA_KV_GROUPS
    sbw = SB_HEADS * HEAD_DIM
    nqw = NSA_HEADS * HEAD_DIM
    kvw = G * HEAD_DIM
    bounds = np.cumsum([0, sbw, sbw, sbw, nqw, kvw, kvw, kvw, kvw, kvw, kvw, NSA_HEADS * N_GATES])
    (w_sbq, w_sbk, w_sbv, w_nq, w_kc, w_vc, w_ks, w_vs, w_kw, w_vw, w_gate) = [
        w_in[:, bounds[i]:bounds[i + 1]] for i in range(11)]
    per_group = NSA_REP * N_GATES
    w_gate = jnp.concatenate([_pad_to(w_gate[:, g * per_group:(g + 1) * per_group], 1, LANES)
                              for g in range(G)], axis=1)
    tables = _rope_tables(T)
    cat = lambda ws: jnp.concatenate(ws, axis=1).astype(BF16)
    q_scale = lambda n_q, n_rest: jnp.concatenate([jnp.full((n_q,), HEAD_DIM ** -0.5, F32), jnp.ones((n_rest,), F32)])
    plain = _project(x_bf16, cat([w_sbq, w_sbk, w_sbv, w_vs, w_vw]), BF16, 1024, 512,
                     col_scale=q_scale(sbw, 2 * sbw + 2 * kvw))
    roped = _project(x_bf16, cat([w_nq, w_ks, w_kw]), BF16, 1024, 512,
                     col_scale=q_scale(nqw, 2 * kvw), rope_tables=tables)
    kc = _project(x_bf16, w_kc.astype(BF16), F32, 1024, kvw, rope_tables=tables)
    vc_gate = _project(x_bf16, cat([w_vc, w_gate]), F32, 1024, 2 * kvw)

    o_sb = _sb_attention(plain, SB_HEADS, 0, SB_HEADS, 2 * SB_HEADS)

    head = lambda a, g: a[:, g * HEAD_DIM:(g + 1) * HEAD_DIM]
    streams = jnp.stack([head(kc, g) for g in range(G)] + [head(vc_gate, g) for g in range(G)])
    rep = lambda a, b: jnp.stack([a] * G + [b] * G)
    cmp = _compress(streams, rep(cmp_pos_k, cmp_pos_v), rep(cmp_w1_k, cmp_w1_v), rep(cmp_w2_k, cmp_w2_v))
    ones = jnp.ones((T, HEAD_DIM), BF16)
    v_aug = jnp.concatenate([a for h in range(2 * G) for a in (head(plain, 3 * SB_HEADS + h), ones)], axis=1)
    o_nsa = _nsa_attention(roped, v_aug, cmp[:G], cmp[G:], vc_gate,
                           ks_col=NSA_HEADS, kw_col=NSA_HEADS + G, vs_col=0, vw_col=G, gate_col=G)
    y = _group_rms(o_sb, o_nsa, mix_norm_g)
    return _mm_res_ln(y, w_out.astype(BF16), x_f32, ln_g, ln_b, alpha, 1.0, True)


def _memory_block(x_f32, x_bf16, mem, w_q, w_k, w_v, w_o, ln_g, ln_b, alpha):
    D = w_q.shape[1]
    q = _project(x_bf16, w_q.astype(BF16), BF16, 1024, 1024,
                 col_scale=jnp.full((D,), (D // MEM_HEADS) ** -0.5, F32))
    kv = _project(mem.astype(BF16), jnp.concatenate([w_k, w_v], axis=1).astype(BF16), BF16, 256, 1024)
    o = _cross_attention(q, kv, MEM_HEADS)
    return _mm_res_ln(o, w_o.astype(BF16), x_f32, ln_g, ln_b, alpha, 1.0, True)


def kernel(x, mem, ln1_g, ln1_b, ffn1_gate, ffn1_up, ffn1_down, w_in, cmp_pos_k, cmp_w1_k, cmp_w2_k, cmp_pos_v, cmp_w1_v, cmp_w2_v, mix_norm_g, w_out, ln2_g, ln2_b, mem_wq, mem_wk, mem_wv, mem_wo, ln3_g, ln3_b, ffn2_gate, ffn2_up, ffn2_down, ln4_g, ln4_b):
    n_layers = ffn1_gate.shape[0]
    alpha = (2 * n_layers) ** 0.25
    outs = []
    for bi in range(x.shape[0]):
        xf = x[bi]
        xb = xf.astype(BF16)
        for l in range(n_layers):
            xf, xb = _ffn(xf, xb, ffn1_gate[l], ffn1_up[l], ffn1_down[l], ln1_g[l], ln1_b[l], alpha, True)
            xf, xb = _mixer(xf, xb, w_in[l], cmp_pos_k[l], cmp_w1_k[l], cmp_w2_k[l], cmp_pos_v[l], cmp_w1_v[l],
                            cmp_w2_v[l], mix_norm_g[l], w_out[l], ln2_g[l], ln2_b[l], alpha)
            xf, xb = _memory_block(xf, xb, mem[bi], mem_wq[l], mem_wk[l], mem_wv[l], mem_wo[l], ln3_g[l], ln3_b[l],
                                   alpha)
            xf, xb = _ffn(xf, xb, ffn2_gate[l], ffn2_up[l], ffn2_down[l], ln4_g[l], ln4_b[l], alpha,
                          l + 1 < n_layers)
        outs.append(xf)
    return outs[0][None] if len(outs) == 1 else jnp.stack(outs)
```

```python
import functools

import numpy as np
import jax
import jax.numpy as jnp
from jax import lax
from jax.experimental import pallas as pl
from jax.experimental.pallas import tpu as pltpu

HEAD_DIM = 128
SB_HEADS = 8
NSA_HEADS = 8
NSA_KV_GROUPS = 2
NSA_REP = NSA_HEADS // NSA_KV_GROUPS
N_GATES = 3
CMP_BLOCK = 32
CMP_STRIDE = 16
SEL_BLOCK = 64
SEL_SHIFT = 6
SEL_TOPK = 16
WINDOW = 512
MEM_HEADS = 4
ROPE_THETA = 10000.0
LN_EPS = 1e-5
RMS_EPS = 1e-6
MASK_FILL = -1e30
LOG2E = 1.4426950408889634
assert 1 << SEL_SHIFT == SEL_BLOCK

LANES = 128
SUBLANES = 8
FF_TILE = 512
FF_DOWN_STEPS = 4
VMEM_LIMIT = 56 * 1024 * 1024

BF16 = jnp.bfloat16
F32 = jnp.float32


def _params(sem):
    return pltpu.CompilerParams(dimension_semantics=sem, vmem_limit_bytes=VMEM_LIMIT)


def _dot(a, b):
    return jnp.dot(a, b, preferred_element_type=F32)


def _dot_nt(a, b):
    return lax.dot_general(a, b, (((1,), (1,)), ((), ())), preferred_element_type=F32)


def _split_dot(a, b):
    hi = a.astype(BF16)
    lo = (a - hi.astype(F32)).astype(BF16)
    return _dot(hi, b) + _dot(lo, b)


def _layer_norm(z, g, b):
    mu = jnp.mean(z, axis=-1, keepdims=True)
    zc = z - mu
    var = jnp.mean(zc * zc, axis=-1, keepdims=True)
    return zc * lax.rsqrt(var + LN_EPS) * g + b


def _proj_kernel(x_ref, w_ref, *rest, scaled, rope):
    rest = list(rest)
    o_ref = rest.pop()
    y = _dot(x_ref[...], w_ref[...])
    if scaled:
        y = y * rest.pop(0)[...]
    if rope:
        cos_ref, sin_ref = rest
        c = cos_ref[...]
        s = sin_ref[...]
        heads = []
        for h in range(y.shape[1] // HEAD_DIM):
            yh = y[:, h * HEAD_DIM:(h + 1) * HEAD_DIM]
            heads.append(yh * c + pltpu.roll(yh, HEAD_DIM // 2, 1) * s)
        y = jnp.concatenate(heads, axis=1) if len(heads) > 1 else heads[0]
    o_ref[...] = y.astype(o_ref.dtype)


def _project(x, w, out_dtype, tm, tn, col_scale=None, rope_tables=None):
    M, K = x.shape
    N = w.shape[1]
    tm = min(tm, M)
    assert M % tm == 0 and N % tn == 0
    in_specs = [pl.BlockSpec((tm, K), lambda i, j: (i, 0)),
                pl.BlockSpec((K, tn), lambda i, j: (0, j))]
    args = [x, w]
    if col_scale is not None:
        in_specs.append(pl.BlockSpec((1, tn), lambda i, j: (0, j)))
        args.append(col_scale.reshape(1, N))
    if rope_tables is not None:
        in_specs += [pl.BlockSpec((tm, HEAD_DIM), lambda i, j: (i, 0))] * 2
        args += list(rope_tables)
    return pl.pallas_call(
        functools.partial(_proj_kernel, scaled=col_scale is not None, rope=rope_tables is not None),
        out_shape=jax.ShapeDtypeStruct((M, N), out_dtype),
        grid=(M // tm, N // tn),
        in_specs=in_specs,
        out_specs=pl.BlockSpec((tm, tn), lambda i, j: (i, j)),
        compiler_params=_params(("parallel", "arbitrary")),
        name="proj_rope" if rope_tables is not None else "proj",
    )(*args)


def _swiglu_up_kernel(x_ref, wg_ref, wu_ref, o_ref, wg_bf16, wu_bf16):
    @pl.when(pl.program_id(1) == 0)
    def _():
        wg_bf16[...] = wg_ref[...].astype(BF16)
        wu_bf16[...] = wu_ref[...].astype(BF16)

    x = x_ref[...]
    g = _dot(x, wg_bf16[...])
    u = _dot(x, wu_bf16[...])
    o_ref[...] = (jax.nn.silu(g) * u).astype(o_ref.dtype)


def _swiglu_up(x, wg, wu, tn, n_blocks, tm=1024):
    M, K = x.shape
    tm = min(tm, M)
    assert M % tm == 0 and n_blocks * tn <= wg.shape[1] and wg.shape == wu.shape
    return pl.pallas_call(
        _swiglu_up_kernel,
        out_shape=jax.ShapeDtypeStruct((M, n_blocks * tn), BF16),
        grid=(n_blocks, M // tm),
        in_specs=[pl.BlockSpec((tm, K), lambda j, i: (i, 0)),
                  pl.BlockSpec((K, tn), lambda j, i: (0, j)),
                  pl.BlockSpec((K, tn), lambda j, i: (0, j))],
        out_specs=pl.BlockSpec((tm, tn), lambda j, i: (i, j)),
        scratch_shapes=[pltpu.VMEM((K, tn), BF16), pltpu.VMEM((K, tn), BF16)],
        compiler_params=_params(("parallel", "arbitrary")),
        name="swiglu_up",
    )(x, wg, wu)


def _row_chunks(tm, n_split):
    return [slice(c * tm // n_split, (c + 1) * tm // n_split) for c in range(n_split)]


def _res_ln_store(chunks, totals, res_ref, g_ref, b_ref, of_ref, ob_ref, alpha, coef):
    for c, y in zip(chunks, totals):
        out = _layer_norm(alpha * res_ref[c, :] + coef * y, g_ref[...], b_ref[...])
        of_ref[c, :] = out
        if ob_ref is not None:
            ob_ref[c, :] = out.astype(BF16)


def _mm_res_ln_kernel(*refs, alpha, coef, emit_bf16, n_k, n_split, has_tail):
    refs = list(refs)
    h_ref, w_ref = refs.pop(0), refs.pop(0)
    ht_ref, wt_ref = (refs.pop(0), refs.pop(0)) if has_tail else (None, None)
    res_ref, g_ref, b_ref, of_ref = refs.pop(0), refs.pop(0), refs.pop(0), refs.pop(0)
    ob_ref = refs.pop(0) if emit_bf16 else None
    acc_ref = refs.pop(0) if n_k > 1 else None
    chunks = _row_chunks(h_ref.shape[0], n_split)

    def partial_products():
        return [_dot(h_ref[c, :], w_ref[...]) for c in chunks]

    def finish(totals):
        if has_tail:
            totals = [y + _dot(ht_ref[c, :], wt_ref[...]) for c, y in zip(chunks, totals)]
        _res_ln_store(chunks, totals, res_ref, g_ref, b_ref, of_ref, ob_ref, alpha, coef)

    if n_k == 1:
        finish(partial_products())
        return
    k = pl.program_id(1)

    @pl.when(k == 0)
    def _():
        for c, y in zip(chunks, partial_products()):
            acc_ref[c, :] = y

    @pl.when((k > 0) & (k < n_k - 1))
    def _():
        for c, y in zip(chunks, partial_products()):
            acc_ref[c, :] += y

    @pl.when(k == n_k - 1)
    def _():
        finish([acc_ref[c, :] + y for c, y in zip(chunks, partial_products())])


def _mm_res_ln(h, w, res, g, b, alpha, coef, emit_bf16, tm=512, n_k=1, n_split=2, tail=None):
    M, K = h.shape
    N = w.shape[1]
    tm = min(tm, M)
    tk = K // n_k
    assert M % tm == 0 and K % n_k == 0 and K <= w.shape[0] and (n_k == 1 or tk % LANES == 0)
    row_block = pl.BlockSpec((tm, N), lambda i, k: (i, 0))
    vec = pl.BlockSpec((1, N), lambda i, k: (0, 0))
    in_specs = [pl.BlockSpec((tm, tk), lambda i, k: (i, k)), pl.BlockSpec((tk, N), lambda i, k: (k, 0))]
    args = [h, w]
    if tail is not None:
        kt = tail[0].shape[1]
        in_specs += [pl.BlockSpec((tm, kt), lambda i, k: (i, 0)), pl.BlockSpec((kt, N), lambda i, k: (0, 0))]
        args += list(tail)
    out = pl.pallas_call(
        functools.partial(_mm_res_ln_kernel, alpha=alpha, coef=coef, emit_bf16=emit_bf16,
                          n_k=n_k, n_split=n_split, has_tail=tail is not None),
        out_shape=[jax.ShapeDtypeStruct((M, N), F32)] + [jax.ShapeDtypeStruct((M, N), BF16)] * emit_bf16,
        grid=(M // tm, n_k),
        in_specs=in_specs + [row_block, vec, vec],
        out_specs=[row_block] * (1 + emit_bf16),
        scratch_shapes=[pltpu.VMEM((tm, N), F32)] if n_k > 1 else [],
        compiler_params=_params(("parallel", "arbitrary")),
        name="mm_res_ln",
    )(*args, res, g.reshape(1, N), b.reshape(1, N))
    return (out[0], out[1]) if emit_bf16 else (out[0], None)


def _rms_mm_res_ln_kernel(a_ref, b_ref, gain_ref, w_ref, res_ref, g_ref, beta_ref, of_ref, ob_ref, *,
                          alpha, n_split):
    chunks = _row_chunks(a_ref.shape[0], n_split)
    wa = a_ref.shape[1]
    gain = gain_ref[...]

    def rms(o, gn):
        return (o * lax.rsqrt(jnp.mean(o * o, axis=-1, keepdims=True) + RMS_EPS) * gn).astype(BF16)

    totals = [_dot(jnp.concatenate([rms(a_ref[c, :], gain[:, :wa]), rms(b_ref[c, :], gain[:, wa:])], axis=1),
                   w_ref[...]) for c in chunks]
    _res_ln_store(chunks, totals, res_ref, g_ref, beta_ref, of_ref, ob_ref, alpha, 1.0)


def _rms_mm_res_ln(o_a, o_b, gain, w, res, g, b, alpha, tm=512, n_split=2):
    M, wa = o_a.shape
    wb = o_b.shape[1]
    N = w.shape[1]
    tm = min(tm, M)
    assert M % tm == 0 and w.shape[0] == wa + wb
    row_block = pl.BlockSpec((tm, N), lambda i: (i, 0))
    vec = pl.BlockSpec((1, N), lambda i: (0, 0))
    return pl.pallas_call(
        functools.partial(_rms_mm_res_ln_kernel, alpha=alpha, n_split=n_split),
        out_shape=[jax.ShapeDtypeStruct((M, N), F32), jax.ShapeDtypeStruct((M, N), BF16)],
        grid=(M // tm,),
        in_specs=[pl.BlockSpec((tm, wa), lambda i: (i, 0)),
                  pl.BlockSpec((tm, wb), lambda i: (i, 0)),
                  pl.BlockSpec((1, wa + wb), lambda i: (0, 0)),
                  pl.BlockSpec((wa + wb, N), lambda i: (0, 0)),
                  row_block, vec, vec],
        out_specs=[row_block, row_block],
        compiler_params=_params(("parallel",)),
        name="rms_mm_res_ln",
    )(o_a, o_b, gain.reshape(1, wa + wb), w, res, g.reshape(1, N), b.reshape(1, N))


def _sb_kernel(q_ref, k_ref, v_ref, o_ref, acc_ref, c_ref, *, blk, heads):
    qi = pl.program_id(1)
    row = lax.broadcasted_iota(jnp.int32, (blk, blk), 0)
    col = lax.broadcasted_iota(jnp.int32, (blk, blk), 1)
    suffix = jnp.where(row > col, 1.0, 0.0).astype(BF16)
    strict = col < row

    cols = [slice(h * HEAD_DIM, (h + 1) * HEAD_DIM) for h in range(heads)]

    def step(kb_far, n_blocks, diagonal):
        ks = pl.multiple_of(kb_far * blk, blk)
        key_rows = [pl.ds(ks + b * blk, blk) for b in range(n_blocks)]
        pairs = [(b, h) for b in reversed(range(n_blocks)) for h in range(heads)]
        z = {p: _dot_nt(q_ref[:, cols[p[1]]], k_ref[key_rows[p[0]], cols[p[1]]]) for p in pairs}
        log_beta, log_1m = {}, {}
        for p in pairs:
            y = z[p]
            lb = jnp.minimum(y, 0.0) - jnp.log(1.0 + jnp.exp2(jnp.abs(y) * -LOG2E))
            l1 = lb - y
            if diagonal:
                l1 = jnp.where(strict, l1, 0.0)
            log_beta[p] = lb
            log_1m[p] = l1.astype(BF16)
        log_stay = {p: _dot(log_1m[p], suffix) for p in pairs}
        weights = {}
        for h in range(heads):
            c = c_ref[h]
            for b in reversed(range(n_blocks)):
                p = (b, h)
                w = jnp.exp(log_beta[p] + (log_stay[p] + c))
                if diagonal:
                    w = jnp.where(strict, w, 0.0)
                weights[p] = w.astype(BF16)
                c = c + (log_stay[p][:, :1] + log_1m[p][:, :1].astype(F32))
            c_ref[h] = c
        for h in range(heads):
            w = jnp.concatenate([weights[(b, h)] for b in range(n_blocks)], axis=1)
            acc_ref[:, cols[h]] += _dot(w, v_ref[pl.ds(ks, n_blocks * blk), cols[h]])

    acc_ref[...] = jnp.zeros_like(acc_ref)
    c_ref[...] = jnp.zeros_like(c_ref)
    step(qi, 1, True)

    @pl.when(qi % 2 == 1)
    def _():
        step(qi - 1, 1, False)

    def body(it, carry):
        step(qi - (qi % 2) - 2 * (it + 1), 2, False)
        return carry

    lax.fori_loop(0, qi // 2, body, 0)
    o_ref[...] = acc_ref[...]


def _sb_attention(qkv, n_heads, q_col, k_col, v_col, blk=256, heads=4):
    T = qkv.shape[0]
    blk = min(blk, T)
    width = heads * HEAD_DIM
    assert T % blk == 0 and n_heads % heads == 0
    assert q_col % heads == 0 and k_col % heads == 0 and v_col % heads == 0
    return pl.pallas_call(
        functools.partial(_sb_kernel, blk=blk, heads=heads),
        out_shape=jax.ShapeDtypeStruct((T, n_heads * HEAD_DIM), F32),
        grid=(n_heads // heads, T // blk),
        in_specs=[pl.BlockSpec((blk, width), lambda h, i: (i, q_col // heads + h)),
                  pl.BlockSpec((T, width), lambda h, i: (0, k_col // heads + h)),
                  pl.BlockSpec((T, width), lambda h, i: (0, v_col // heads + h))],
        out_specs=pl.BlockSpec((blk, width), lambda h, i: (i, h)),
        scratch_shapes=[pltpu.VMEM((blk, width), F32), pltpu.VMEM((heads, blk, 1), F32)],
        compiler_params=_params(("parallel", "arbitrary")),
        name="sb_attention",
    )(qkv, qkv, qkv)


def _compress_kernel(x_ref, pos_ref, w1_ref, w2_ref, o_ref):
    n = x_ref.shape[1]
    half = CMP_STRIDE * HEAD_DIM
    x = x_ref[0].astype(BF16)
    w1 = w1_ref[0].astype(BF16)
    a = _dot(x, w1[:half])
    b = _dot(x, w1[half:])
    pos = _dot(pos_ref[0].astype(BF16), w1)
    b_next = pltpu.roll(b, n - 1, 0)
    hid = jax.nn.gelu(a + b_next + pos[0:1])
    out = _dot(hid.astype(BF16), w2_ref[0].astype(BF16))
    valid = lax.broadcasted_iota(jnp.int32, out.shape, 0) < n - 1
    o_ref[0] = jnp.where(valid, out, 0.0).astype(o_ref.dtype)


def _compress(x_tok, pos_emb, w1, w2):
    S, T, _ = x_tok.shape
    n = T // CMP_STRIDE
    width = CMP_STRIDE * HEAD_DIM
    hidden = w1.shape[-1]
    x2 = x_tok.reshape(S, n, width)
    pos_flat = jnp.broadcast_to(pos_emb.reshape(S, 1, 2 * width), (S, SUBLANES, 2 * width))
    return pl.pallas_call(
        _compress_kernel,
        out_shape=jax.ShapeDtypeStruct((S, n, HEAD_DIM), BF16),
        grid=(S,),
        in_specs=[pl.BlockSpec((1, n, width), lambda s: (s, 0, 0)),
                  pl.BlockSpec((1, SUBLANES, 2 * width), lambda s: (s, 0, 0)),
                  pl.BlockSpec((1, 2 * width, hidden), lambda s: (s, 0, 0)),
                  pl.BlockSpec((1, hidden, HEAD_DIM), lambda s: (s, 0, 0))],
        out_specs=pl.BlockSpec((1, n, HEAD_DIM), lambda s: (s, 0, 0)),
        compiler_params=_params(("parallel",)),
        name="nsa_compress",
    )(x2, pos_flat, w1, w2)


def _nsa_kernel(q_ref, kc_ref, vc_ref, ks_ref, vs_ref, kw_ref, vw_ref, gate_ref, c2s_t_ref, o_ref,
                m_ref, mb_ref, acc_ref, s_ref, *, blk, tk):
    R = NSA_REP
    qi = pl.program_id(1)
    q0 = qi * blk
    n_cmp = kc_ref.shape[1]
    n_sel = c2s_t_ref.shape[0]
    rows = [slice(r * blk, (r + 1) * blk) for r in range(R)]
    qs = jnp.concatenate([q_ref[:, r * HEAD_DIM:(r + 1) * HEAD_DIM] for r in range(R)], axis=0)
    q_pos = q0 + lax.broadcasted_iota(jnp.int32, (blk, 1), 0)
    q_lane = q0 + lax.broadcasted_iota(jnp.int32, (1, blk), 1)

    def masked_softmax(s, mask, axis):
        s = jnp.where(mask, s, MASK_FILL)
        p = jnp.where(mask, jnp.exp(s - jnp.max(s, axis=axis, keepdims=True)), 0.0)
        return p / jnp.maximum(jnp.sum(p, axis=axis, keepdims=True), 1e-30)

    kc = kc_ref[0]
    s_all = _dot_nt(qs, kc)
    cmp_end = lax.broadcasted_iota(jnp.int32, (1, n_cmp), 1) * CMP_STRIDE + (CMP_BLOCK - 1)
    mask_c = cmp_end <= q_pos
    p_cmp = [masked_softmax(s_all[rows[r]], mask_c, 1).astype(BF16) for r in range(R)]
    o_cmp = _dot(jnp.concatenate(p_cmp, axis=0), vc_ref[0])

    wk = WINDOW + blk
    ws = pl.multiple_of(jnp.maximum(q0 - WINDOW, 0), blk)
    dist = q_pos - (ws + lax.broadcasted_iota(jnp.int32, (1, wk), 1))
    bias_w = jnp.where((dist >= 0) & (dist < WINDOW), 0.0, MASK_FILL)
    k_w = kw_ref[pl.ds(ws, wk), :]
    s_w = [_dot_nt(qs[rows[r]], k_w) for r in range(R)]

    def window_head(r):
        sc = s_w[r] + bias_w
        p = jnp.exp(sc - jnp.max(sc, axis=-1, keepdims=True)).astype(BF16)
        pv = _dot(p, vw_ref[pl.ds(ws, wk), :])
        return pv[:, :HEAD_DIM] / pv[:, HEAD_DIM:]

    st_all = _dot_nt(kc, qs)
    cmp_end_t = lax.broadcasted_iota(jnp.int32, (n_cmp, 1), 0) * CMP_STRIDE + (CMP_BLOCK - 1)
    bias_t = jnp.where(cmp_end_t <= q_lane, 0.0, MASK_FILL)
    p_sum = None
    for r in range(R):
        st = st_all[:, rows[r]] + bias_t
        p = jnp.exp(st - jnp.max(st, axis=0, keepdims=True))
        p = p / jnp.sum(p, axis=0, keepdims=True)
        p_sum = p if p_sum is None else p_sum + p
    hi = p_sum.astype(BF16)
    lo = (p_sum - hi.astype(F32)).astype(BF16)
    imp = _dot(c2s_t_ref[...], hi) + _dot(c2s_t_ref[...], lo)
    blk_id = lax.broadcasted_iota(jnp.int32, (n_sel, 1), 0)
    forced = (blk_id == 0) | (blk_id == (q_lane >> SEL_SHIFT))
    valid = blk_id * SEL_BLOCK <= q_lane
    work = jnp.where(forced, jnp.inf, jnp.where(valid, imp, -jnp.inf))
    blk_idf = blk_id.astype(F32)
    left = work
    o_win = []
    n_steps = min(SEL_TOPK, n_sel)
    for step in range(n_steps):
        top = jnp.max(left, axis=0, keepdims=True)
        first = jnp.min(jnp.where(left == top, blk_idf, float(n_sel)), axis=0, keepdims=True)
        left = jnp.where(blk_idf == first, -jnp.inf, left)
        while len(o_win) * n_steps < (step + 1) * R:
            o_win.append(window_head(len(o_win)))
    sel_bias = jnp.where(left != work, 0.0, MASK_FILL).T.astype(BF16)

    gates = jax.nn.sigmoid(gate_ref[...])

    def gate(r, branch):
        return gates[:, r * N_GATES + branch:r * N_GATES + branch + 1]

    o_ref[...] = jnp.concatenate([gate(r, 0) * o_cmp[rows[r]] + gate(r, 2) * o_win[r] for r in range(R)], axis=1)

    tok = lax.broadcasted_iota(jnp.int32, (n_sel, tk), 1)
    sel_row = lax.broadcasted_iota(jnp.int32, (n_sel, tk), 0)
    key_off = lax.broadcasted_iota(jnp.int32, (1, tk), 1)

    def key_rows(kb):
        return pl.ds(pl.multiple_of(kb * tk, tk), tk)

    def block_bias(kb):
        ks = kb * tk
        expand = jnp.where(((ks + tok) >> SEL_SHIFT) == sel_row, 1.0, 0.0).astype(BF16)
        bias = _dot(sel_bias, expand)
        return jnp.where(ks + key_off <= q_pos, bias, MASK_FILL)

    def put_scores(r, k, bias):
        sc = _dot_nt(qs[rows[r]], k) + bias
        s_ref[rows[r], :] = sc
        mb_ref[rows[r], :] = jnp.max(sc, axis=-1, keepdims=True)

    def take_probs(r):
        m_old = m_ref[rows[r], :]
        m_new = jnp.maximum(m_old, mb_ref[rows[r], :])
        m_ref[rows[r], :] = m_new
        return jnp.exp(s_ref[rows[r], :] - m_new).astype(BF16), jnp.exp(m_old - m_new)

    def accumulate(r, p, alpha, v1):
        acc_ref[rows[r], :] = alpha * acc_ref[rows[r], :] + _dot(p, v1)

    m_ref[...] = jnp.full_like(m_ref, MASK_FILL)
    acc_ref[...] = jnp.zeros_like(acc_ref)
    kb_diag = qi // (tk // blk)
    bias = block_bias(0)
    for r in range(R):
        put_scores(r, ks_ref[key_rows(0), :], bias)

    def sel_body(kb, carry):
        k_next = ks_ref[key_rows(kb + 1), :]
        bias_next = block_bias(kb + 1)
        v1 = vs_ref[key_rows(kb), :]
        for r in range(R):
            p, alpha = take_probs(r)
            put_scores(r, k_next, bias_next)
            accumulate(r, p, alpha, v1)
        return carry

    lax.fori_loop(0, kb_diag, sel_body, 0)
    v1 = vs_ref[key_rows(kb_diag), :]
    for r in range(R):
        p, alpha = take_probs(r)
        accumulate(r, p, alpha, v1)
    o_sel = acc_ref[:, :HEAD_DIM] / acc_ref[:, HEAD_DIM:]
    o_ref[...] += jnp.concatenate([gate(r, 1) * o_sel[rows[r]] for r in range(R)], axis=1)


def _nsa_attention(q_tok, v_aug, k_cmp, v_cmp, gates, ks_col, kw_col, vs_col, vw_col, gate_col,
                   blk=256, tk=1024):
    T = q_tok.shape[0]
    G = NSA_KV_GROUPS
    blk = min(blk, T)
    tk = min(tk, T)
    assert T % tk == 0 and tk % blk == 0 and WINDOW % blk == 0 and WINDOW + blk <= T
    n_cmp = T // CMP_STRIDE
    n_sel = T // SEL_BLOCK
    cmp_start = np.arange(n_cmp)[None, :] * CMP_STRIDE
    sel_start = np.arange(n_sel)[:, None] * SEL_BLOCK
    overlap = np.clip(np.minimum(cmp_start + CMP_BLOCK, sel_start + SEL_BLOCK)
                      - np.maximum(cmp_start, sel_start), 0, None)
    c2s_t = jnp.asarray(overlap.astype(np.float32) / CMP_BLOCK, dtype=BF16)
    width = NSA_REP * HEAD_DIM
    key_spec = lambda col: pl.BlockSpec((T, HEAD_DIM), lambda g, i: (0, col + g))
    val_spec = lambda col: pl.BlockSpec((T, 2 * HEAD_DIM), lambda g, i: (0, col + g))
    cmp_spec = pl.BlockSpec((1, n_cmp, HEAD_DIM), lambda g, i: (g, 0, 0))
    return pl.pallas_call(
        functools.partial(_nsa_kernel, blk=blk, tk=tk),
        out_shape=jax.ShapeDtypeStruct((T, G * width), F32),
        grid=(G, T // blk),
        in_specs=[pl.BlockSpec((blk, width), lambda g, i: (i, g)),
                  cmp_spec, cmp_spec,
                  key_spec(ks_col), val_spec(vs_col), key_spec(kw_col), val_spec(vw_col),
                  pl.BlockSpec((blk, LANES), lambda g, i: (i, gate_col + g)),
                  pl.BlockSpec((n_sel, n_cmp), lambda g, i: (0, 0))],
        out_specs=pl.BlockSpec((blk, width), lambda g, i: (i, g)),
        scratch_shapes=[pltpu.VMEM((NSA_REP * blk, 1), F32),
                        pltpu.VMEM((NSA_REP * blk, 1), F32),
                        pltpu.VMEM((NSA_REP * blk, 2 * HEAD_DIM), F32),
                        pltpu.VMEM((NSA_REP * blk, tk), F32)],
        compiler_params=_params(("parallel", "arbitrary")),
        name="nsa_attention",
    )(q_tok, k_cmp, v_cmp, q_tok, v_aug, q_tok, v_aug, gates, c2s_t)


def _cross_kernel(q_ref, k_ref, v_ref, o_ref):
    s = _dot_nt(q_ref[...], k_ref[...])
    p = jnp.exp(s - jnp.max(s, axis=-1, keepdims=True))
    p = p / jnp.sum(p, axis=-1, keepdims=True)
    o_ref[...] = _dot(p.astype(BF16), v_ref[...]).astype(o_ref.dtype)


def _cross_attention(q, kv, n_heads, tm=1024):
    T, D = q.shape
    M = kv.shape[0]
    dh = D // n_heads
    tm = min(tm, T)
    return pl.pallas_call(
        _cross_kernel,
        out_shape=jax.ShapeDtypeStruct((T, D), BF16),
        grid=(T // tm, n_heads),
        in_specs=[pl.BlockSpec((tm, dh), lambda i, h: (i, h)),
                  pl.BlockSpec((M, dh), lambda i, h: (0, h)),
                  pl.BlockSpec((M, dh), lambda i, h: (0, n_heads + h))],
        out_specs=pl.BlockSpec((tm, dh), lambda i, h: (i, h)),
        compiler_params=_params(("parallel", "arbitrary")),
        name="cross_attention",
    )(q, kv, kv)


def _rope_tables(T):
    half = HEAD_DIM // 2
    inv_freq = np.float32(ROPE_THETA) ** (-np.arange(half, dtype=np.float32) / np.float32(half))
    ang = np.arange(T, dtype=np.float32)[:, None] * inv_freq[None, :]
    cos, sin = np.cos(ang), np.sin(ang)
    return jnp.asarray(np.concatenate([cos, cos], axis=1)), jnp.asarray(np.concatenate([-sin, sin], axis=1))


def _pad_to(w, axis, mult):
    pad = -w.shape[axis] % mult
    if not pad:
        return w
    shape = list(w.shape)
    shape[axis] = pad
    return jnp.concatenate([w, jnp.zeros(shape, w.dtype)], axis=axis)


def _ffn(x_f32, x_bf16, w_gate, w_up, w_down, ln_g, ln_b, alpha, emit_bf16):
    d_ff = w_gate.shape[1]
    main = d_ff // FF_TILE * FF_TILE
    assert (d_ff - main) % LANES == 0 and main % (FF_DOWN_STEPS * LANES) == 0
    h = _swiglu_up(x_bf16, w_gate, w_up, FF_TILE, main // FF_TILE)
    wd = w_down.astype(BF16)
    tail = None
    if main < d_ff:
        tail = (_swiglu_up(x_bf16, w_gate[:, main:], w_up[:, main:], d_ff - main, 1), wd[main:])
    return _mm_res_ln(h, wd, x_f32, ln_g, ln_b, alpha, 0.5, emit_bf16, n_k=FF_DOWN_STEPS, tail=tail)


def _mixer(x_f32, x_bf16, w_in, cmp_pos_k, cmp_w1_k, cmp_w2_k, cmp_pos_v, cmp_w1_v, cmp_w2_v,
           mix_norm_g, w_out, ln_g, ln_b, alpha):
    T = x_f32.shape[0]
    G = NSA_KV_GROUPS
    sbw = SB_HEADS * HEAD_DIM
    nqw = NSA_HEADS * HEAD_DIM
    kvw = G * HEAD_DIM
    bounds = np.cumsum([0, sbw, sbw, sbw, nqw, kvw, kvw, kvw, kvw, kvw, kvw, NSA_HEADS * N_GATES])
    (w_sbq, w_sbk, w_sbv, w_nq, w_kc, w_vc, w_ks, w_vs, w_kw, w_vw, w_gate) = [
        w_in[:, bounds[i]:bounds[i + 1]] for i in range(11)]
    per_group = NSA_REP * N_GATES
    w_gate = jnp.concatenate([_pad_to(w_gate[:, g * per_group:(g + 1) * per_group], 1, LANES)
                              for g in range(G)], axis=1)
    tables = _rope_tables(T)
    cat = lambda ws: jnp.concatenate(ws, axis=1).astype(BF16)
    q_scale = lambda n_q, n_rest: jnp.concatenate([jnp.full((n_q,), HEAD_DIM ** -0.5, F32), jnp.ones((n_rest,), F32)])
    plain = _project(x_bf16, cat([w_sbq, w_sbk, w_sbv, w_vs, w_vw]), BF16, 1024, 512,
                     col_scale=q_scale(sbw, 2 * sbw + 2 * kvw))
    roped = _project(x_bf16, cat([w_nq, w_ks, w_kw]), BF16, 1024, 512,
                     col_scale=q_scale(nqw, 2 * kvw), rope_tables=tables)
    kc = _project(x_bf16, w_kc.astype(BF16), F32, 1024, kvw, rope_tables=tables)
    vc_gate = _project(x_bf16, cat([w_vc, w_gate]), F32, 1024, 2 * kvw)

    o_sb = _sb_attention(plain, SB_HEADS, 0, SB_HEADS, 2 * SB_HEADS)

    head = lambda a, g: a[:, g * HEAD_DIM:(g + 1) * HEAD_DIM]
    streams = jnp.stack([head(kc, g) for g in range(G)] + [head(vc_gate, g) for g in range(G)])
    rep = lambda a, b: jnp.stack([a] * G + [b] * G)
    cmp = _compress(streams, rep(cmp_pos_k, cmp_pos_v), rep(cmp_w1_k, cmp_w1_v), rep(cmp_w2_k, cmp_w2_v))
    ones = jnp.ones((T, HEAD_DIM), BF16)
    v_aug = jnp.concatenate([a for h in range(2 * G) for a in (head(plain, 3 * SB_HEADS + h), ones)], axis=1)
    o_nsa = _nsa_attention(roped, v_aug, cmp[:G], cmp[G:], vc_gate,
                           ks_col=NSA_HEADS, kw_col=NSA_HEADS + G, vs_col=0, vw_col=G, gate_col=G)
    return _rms_mm_res_ln(o_sb, o_nsa, mix_norm_g, w_out.astype(BF16), x_f32, ln_g, ln_b, alpha)


def _memory_block(x_f32, x_bf16, mem, w_q, w_k, w_v, w_o, ln_g, ln_b, alpha):
    D = w_q.shape[1]
    q = _project(x_bf16, w_q.astype(BF16), BF16, 1024, 1024,
                 col_scale=jnp.full((D,), (D // MEM_HEADS) ** -0.5, F32))
    kv = _project(mem.astype(BF16), jnp.concatenate([w_k, w_v], axis=1).astype(BF16), BF16, 256, 1024)
    o = _cross_attention(q, kv, MEM_HEADS)
    return _mm_res_ln(o, w_o.astype(BF16), x_f32, ln_g, ln_b, alpha, 1.0, True)


def kernel(x, mem, ln1_g, ln1_b, ffn1_gate, ffn1_up, ffn1_down, w_in, cmp_pos_k, cmp_w1_k, cmp_w2_k, cmp_pos_v, cmp_w1_v, cmp_w2_v, mix_norm_g, w_out, ln2_g, ln2_b, mem_wq, mem_wk, mem_wv, mem_wo, ln3_g, ln3_b, ffn2_gate, ffn2_up, ffn2_down, ln4_g, ln4_b):
    n_layers = ffn1_gate.shape[0]
    alpha = (2 * n_layers) ** 0.25
    outs = []
    for bi in range(x.shape[0]):
        xf = x[bi]
        xb = xf.astype(BF16)
        for l in range(n_layers):
            xf, xb = _ffn(xf, xb, ffn1_gate[l], ffn1_up[l], ffn1_down[l], ln1_g[l], ln1_b[l], alpha, True)
            xf, xb = _mixer(xf, xb, w_in[l], cmp_pos_k[l], cmp_w1_k[l], cmp_w2_k[l], cmp_pos_v[l], cmp_w1_v[l],
                            cmp_w2_v[l], mix_norm_g[l], w_out[l], ln2_g[l], ln2_b[l], alpha)
            xf, xb = _memory_block(xf, xb, mem[bi], mem_wq[l], mem_wk[l], mem_wv[l], mem_wo[l], ln3_g[l], ln3_b[l],
                                   alpha)
            xf, xb = _ffn(xf, xb, ffn2_gate[l], ffn2_up[l], ffn2_down[l], ln4_g[l], ln4_b[l], alpha,
                          l + 1 < n_layers)
        outs.append(xf)
    return outs[0][None] if len(outs) == 1 else jnp.stack(outs)
```

```python
import functools

import numpy as np
import jax
import jax.numpy as jnp
from jax import lax
from jax.experimental import pallas as pl
from jax.experimental.pallas import tpu as pltpu

HEAD_DIM = 128
SB_HEADS = 8
NSA_HEADS = 8
NSA_KV_GROUPS = 2
NSA_REP = NSA_HEADS // NSA_KV_GROUPS
N_GATES = 3
CMP_BLOCK = 32
CMP_STRIDE = 16
SEL_BLOCK = 64
SEL_SHIFT = 6
SEL_TOPK = 16
WINDOW = 512
MEM_HEADS = 4
ROPE_THETA = 10000.0
LN_EPS = 1e-5
RMS_EPS = 1e-6
MASK_FILL = -1e30
LOG2E = 1.4426950408889634
SB_UNDERFLOW = -110.0
assert 1 << SEL_SHIFT == SEL_BLOCK

LANES = 128
SUBLANES = 8
FF_TILE = 512
FF_DOWN_STEPS = 4
VMEM_LIMIT = 56 * 1024 * 1024

BF16 = jnp.bfloat16
F32 = jnp.float32


def _params(sem):
    return pltpu.CompilerParams(dimension_semantics=sem, vmem_limit_bytes=VMEM_LIMIT)


def _dot(a, b):
    return jnp.dot(a, b, preferred_element_type=F32)


def _dot_nt(a, b):
    return lax.dot_general(a, b, (((1,), (1,)), ((), ())), preferred_element_type=F32)


def _split_dot(a, b):
    hi = a.astype(BF16)
    lo = (a - hi.astype(F32)).astype(BF16)
    return _dot(hi, b) + _dot(lo, b)


def _layer_norm(z, g, b):
    mu = jnp.mean(z, axis=-1, keepdims=True)
    zc = z - mu
    var = jnp.mean(zc * zc, axis=-1, keepdims=True)
    return zc * lax.rsqrt(var + LN_EPS) * g + b


def _proj_kernel(x_ref, w_ref, *rest, scaled, rope):
    rest = list(rest)
    o_ref = rest.pop()
    y = _dot(x_ref[...], w_ref[...])
    if scaled:
        y = y * rest.pop(0)[...]
    if rope:
        cos_ref, sin_ref = rest
        c = cos_ref[...]
        s = sin_ref[...]
        heads = []
        for h in range(y.shape[1] // HEAD_DIM):
            yh = y[:, h * HEAD_DIM:(h + 1) * HEAD_DIM]
            heads.append(yh * c + pltpu.roll(yh, HEAD_DIM // 2, 1) * s)
        y = jnp.concatenate(heads, axis=1) if len(heads) > 1 else heads[0]
    o_ref[...] = y.astype(o_ref.dtype)


def _project(x, w, out_dtype, tm, tn, col_scale=None, rope_tables=None):
    M, K = x.shape
    N = w.shape[1]
    tm = min(tm, M)
    assert M % tm == 0 and N % tn == 0
    in_specs = [pl.BlockSpec((tm, K), lambda i, j: (i, 0)),
                pl.BlockSpec((K, tn), lambda i, j: (0, j))]
    args = [x, w]
    if col_scale is not None:
        in_specs.append(pl.BlockSpec((1, tn), lambda i, j: (0, j)))
        args.append(col_scale.reshape(1, N))
    if rope_tables is not None:
        in_specs += [pl.BlockSpec((tm, HEAD_DIM), lambda i, j: (i, 0))] * 2
        args += list(rope_tables)
    return pl.pallas_call(
        functools.partial(_proj_kernel, scaled=col_scale is not None, rope=rope_tables is not None),
        out_shape=jax.ShapeDtypeStruct((M, N), out_dtype),
        grid=(M // tm, N // tn),
        in_specs=in_specs,
        out_specs=pl.BlockSpec((tm, tn), lambda i, j: (i, j)),
        compiler_params=_params(("parallel", "arbitrary")),
        name="proj_rope" if rope_tables is not None else "proj",
    )(*args)


def _swiglu_up_kernel(x_ref, wg_ref, wu_ref, o_ref, wg_bf16, wu_bf16):
    @pl.when(pl.program_id(1) == 0)
    def _():
        wg_bf16[...] = wg_ref[...].astype(BF16)
        wu_bf16[...] = wu_ref[...].astype(BF16)

    x = x_ref[...]
    g = _dot(x, wg_bf16[...])
    u = _dot(x, wu_bf16[...])
    o_ref[...] = (jax.nn.silu(g) * u).astype(o_ref.dtype)


def _swiglu_up(x, wg, wu, tn, n_blocks, tm=1024):
    M, K = x.shape
    tm = min(tm, M)
    assert M % tm == 0 and n_blocks * tn <= wg.shape[1] and wg.shape == wu.shape
    return pl.pallas_call(
        _swiglu_up_kernel,
        out_shape=jax.ShapeDtypeStruct((M, n_blocks * tn), BF16),
        grid=(n_blocks, M // tm),
        in_specs=[pl.BlockSpec((tm, K), lambda j, i: (i, 0)),
                  pl.BlockSpec((K, tn), lambda j, i: (0, j)),
                  pl.BlockSpec((K, tn), lambda j, i: (0, j))],
        out_specs=pl.BlockSpec((tm, tn), lambda j, i: (i, j)),
        scratch_shapes=[pltpu.VMEM((K, tn), BF16), pltpu.VMEM((K, tn), BF16)],
        compiler_params=_params(("parallel", "arbitrary")),
        name="swiglu_up",
    )(x, wg, wu)


def _row_chunks(tm, n_split):
    return [slice(c * tm // n_split, (c + 1) * tm // n_split) for c in range(n_split)]


def _res_ln_store(chunks, totals, res_ref, g_ref, b_ref, of_ref, ob_ref, alpha, coef):
    for c, y in zip(chunks, totals):
        out = _layer_norm(alpha * res_ref[c, :] + coef * y, g_ref[...], b_ref[...])
        of_ref[c, :] = out
        if ob_ref is not None:
            ob_ref[c, :] = out.astype(BF16)


def _mm_res_ln_kernel(*refs, alpha, coef, emit_bf16, n_k, n_split, has_tail):
    refs = list(refs)
    h_ref, w_ref = refs.pop(0), refs.pop(0)
    ht_ref, wt_ref = (refs.pop(0), refs.pop(0)) if has_tail else (None, None)
    res_ref, g_ref, b_ref, of_ref = refs.pop(0), refs.pop(0), refs.pop(0), refs.pop(0)
    ob_ref = refs.pop(0) if emit_bf16 else None
    acc_ref = refs.pop(0) if n_k > 1 else None
    chunks = _row_chunks(h_ref.shape[0], n_split)

    def partial_products():
        return [_dot(h_ref[c, :], w_ref[...]) for c in chunks]

    def finish(totals):
        if has_tail:
            totals = [y + _dot(ht_ref[c, :], wt_ref[...]) for c, y in zip(chunks, totals)]
        _res_ln_store(chunks, totals, res_ref, g_ref, b_ref, of_ref, ob_ref, alpha, coef)

    if n_k == 1:
        finish(partial_products())
        return
    k = pl.program_id(1)

    @pl.when(k == 0)
    def _():
        for c, y in zip(chunks, partial_products()):
            acc_ref[c, :] = y

    @pl.when((k > 0) & (k < n_k - 1))
    def _():
        for c, y in zip(chunks, partial_products()):
            acc_ref[c, :] += y

    @pl.when(k == n_k - 1)
    def _():
        finish([acc_ref[c, :] + y for c, y in zip(chunks, partial_products())])


def _mm_res_ln(h, w, res, g, b, alpha, coef, emit_bf16, tm=512, n_k=1, n_split=2, tail=None):
    M, K = h.shape
    N = w.shape[1]
    tm = min(tm, M)
    tk = K // n_k
    assert M % tm == 0 and K % n_k == 0 and K <= w.shape[0] and (n_k == 1 or tk % LANES == 0)
    row_block = pl.BlockSpec((tm, N), lambda i, k: (i, 0))
    vec = pl.BlockSpec((1, N), lambda i, k: (0, 0))
    in_specs = [pl.BlockSpec((tm, tk), lambda i, k: (i, k)), pl.BlockSpec((tk, N), lambda i, k: (k, 0))]
    args = [h, w]
    if tail is not None:
        kt = tail[0].shape[1]
        in_specs += [pl.BlockSpec((tm, kt), lambda i, k: (i, 0)), pl.BlockSpec((kt, N), lambda i, k: (0, 0))]
        args += list(tail)
    out = pl.pallas_call(
        functools.partial(_mm_res_ln_kernel, alpha=alpha, coef=coef, emit_bf16=emit_bf16,
                          n_k=n_k, n_split=n_split, has_tail=tail is not None),
        out_shape=[jax.ShapeDtypeStruct((M, N), F32)] + [jax.ShapeDtypeStruct((M, N), BF16)] * emit_bf16,
        grid=(M // tm, n_k),
        in_specs=in_specs + [row_block, vec, vec],
        out_specs=[row_block] * (1 + emit_bf16),
        scratch_shapes=[pltpu.VMEM((tm, N), F32)] if n_k > 1 else [],
        compiler_params=_params(("parallel", "arbitrary")),
        name="mm_res_ln",
    )(*args, res, g.reshape(1, N), b.reshape(1, N))
    return (out[0], out[1]) if emit_bf16 else (out[0], None)


def _rms_mm_res_ln_kernel(a_ref, b_ref, gain_ref, w_ref, res_ref, g_ref, beta_ref, of_ref, ob_ref, *,
                          alpha, n_split):
    chunks = _row_chunks(a_ref.shape[0], n_split)
    wa = a_ref.shape[1]
    gain = gain_ref[...]

    def rms(o, gn):
        return (o * lax.rsqrt(jnp.mean(o * o, axis=-1, keepdims=True) + RMS_EPS) * gn).astype(BF16)

    totals = [_dot(jnp.concatenate([rms(a_ref[c, :], gain[:, :wa]), rms(b_ref[c, :], gain[:, wa:])], axis=1),
                   w_ref[...]) for c in chunks]
    _res_ln_store(chunks, totals, res_ref, g_ref, beta_ref, of_ref, ob_ref, alpha, 1.0)


def _rms_mm_res_ln(o_a, o_b, gain, w, res, g, b, alpha, tm=512, n_split=2):
    M, wa = o_a.shape
    wb = o_b.shape[1]
    N = w.shape[1]
    tm = min(tm, M)
    assert M % tm == 0 and w.shape[0] == wa + wb
    row_block = pl.BlockSpec((tm, N), lambda i: (i, 0))
    vec = pl.BlockSpec((1, N), lambda i: (0, 0))
    return pl.pallas_call(
        functools.partial(_rms_mm_res_ln_kernel, alpha=alpha, n_split=n_split),
        out_shape=[jax.ShapeDtypeStruct((M, N), F32), jax.ShapeDtypeStruct((M, N), BF16)],
        grid=(M // tm,),
        in_specs=[pl.BlockSpec((tm, wa), lambda i: (i, 0)),
                  pl.BlockSpec((tm, wb), lambda i: (i, 0)),
                  pl.BlockSpec((1, wa + wb), lambda i: (0, 0)),
                  pl.BlockSpec((wa + wb, N), lambda i: (0, 0)),
                  row_block, vec, vec],
        out_specs=[row_block, row_block],
        compiler_params=_params(("parallel",)),
        name="rms_mm_res_ln",
    )(o_a, o_b, gain.reshape(1, wa + wb), w, res, g.reshape(1, N), b.reshape(1, N))


def _sb_kernel(q_ref, k_ref, v_ref, o_ref, acc_ref, c_ref, *, blk, heads):
    qi = pl.program_id(1)
    row = lax.broadcasted_iota(jnp.int32, (blk, blk), 0)
    col = lax.broadcasted_iota(jnp.int32, (blk, blk), 1)
    suffix = jnp.where(row > col, 1.0, 0.0).astype(BF16)
    strict = col < row

    cols = [slice(h * HEAD_DIM, (h + 1) * HEAD_DIM) for h in range(heads)]

    def step(kb_far, n_blocks, diagonal):
        ks = pl.multiple_of(kb_far * blk, blk)
        key_rows = [pl.ds(ks + b * blk, blk) for b in range(n_blocks)]
        log_beta, log_1m = {}, {}
        for p in [(b, h) for b in reversed(range(n_blocks)) for h in range(heads)]:
            y = _dot_nt(q_ref[:, cols[p[1]]], k_ref[key_rows[p[0]], cols[p[1]]])
            lb = jnp.minimum(y, 0.0) - jnp.log(1.0 + jnp.exp2(jnp.abs(y) * -LOG2E))
            l1 = lb - y
            if diagonal:
                l1 = jnp.where(strict, l1, 0.0)
            log_beta[p] = lb
            log_1m[p] = l1.astype(BF16)
        weights = {}
        for h in range(heads):
            c = c_ref[h]
            for b in reversed(range(n_blocks)):
                p = (b, h)
                log_stay = _dot(log_1m[p], suffix)
                w = jnp.exp(log_beta[p] + (log_stay + c))
                if diagonal:
                    w = jnp.where(strict, w, 0.0)
                weights[p] = w.astype(BF16)
                c = c + (log_stay[:, :1] + log_1m[p][:, :1].astype(F32))
            c_ref[h] = c
        for h in range(heads):
            w = jnp.concatenate([weights[(b, h)] for b in range(n_blocks)], axis=1)
            acc_ref[:, cols[h]] += _dot(w, v_ref[pl.ds(ks, n_blocks * blk), cols[h]])

    acc_ref[...] = jnp.zeros_like(acc_ref)
    c_ref[...] = jnp.zeros_like(c_ref)
    step(qi, 1, True)

    def more(carry):
        done, c_max = carry
        return (done < qi) & (c_max >= SB_UNDERFLOW)

    def body(carry):
        done, _ = carry
        step(qi - 1 - done, 1, False)
        return done + 1, jnp.max(c_ref[...])

    lax.while_loop(more, body, (jnp.int32(0), jnp.max(c_ref[...])))
    o_ref[...] = acc_ref[...]


def _sb_attention(qkv, n_heads, q_col, k_col, v_col, blk=256, heads=4):
    T = qkv.shape[0]
    blk = min(blk, T)
    width = heads * HEAD_DIM
    assert T % blk == 0 and n_heads % heads == 0
    assert q_col % heads == 0 and k_col % heads == 0 and v_col % heads == 0
    return pl.pallas_call(
        functools.partial(_sb_kernel, blk=blk, heads=heads),
        out_shape=jax.ShapeDtypeStruct((T, n_heads * HEAD_DIM), F32),
        grid=(n_heads // heads, T // blk),
        in_specs=[pl.BlockSpec((blk, width), lambda h, i: (i, q_col // heads + h)),
                  pl.BlockSpec((T, width), lambda h, i: (0, k_col // heads + h)),
                  pl.BlockSpec((T, width), lambda h, i: (0, v_col // heads + h))],
        out_specs=pl.BlockSpec((blk, width), lambda h, i: (i, h)),
        scratch_shapes=[pltpu.VMEM((blk, width), F32), pltpu.VMEM((heads, blk, 1), F32)],
        compiler_params=_params(("parallel", "arbitrary")),
        name="sb_attention",
    )(qkv, qkv, qkv)


def _compress_kernel(x_ref, pos_ref, w1_ref, w2_ref, o_ref):
    n = x_ref.shape[1]
    half = CMP_STRIDE * HEAD_DIM
    x = x_ref[0].astype(BF16)
    w1 = w1_ref[0].astype(BF16)
    a = _dot(x, w1[:half])
    b = _dot(x, w1[half:])
    pos = _dot(pos_ref[0].astype(BF16), w1)
    b_next = pltpu.roll(b, n - 1, 0)
    hid = jax.nn.gelu(a + b_next + pos[0:1])
    out = _dot(hid.astype(BF16), w2_ref[0].astype(BF16))
    valid = lax.broadcasted_iota(jnp.int32, out.shape, 0) < n - 1
    o_ref[0] = jnp.where(valid, out, 0.0).astype(o_ref.dtype)


def _compress(x_tok, pos_emb, w1, w2):
    S, T, _ = x_tok.shape
    n = T // CMP_STRIDE
    width = CMP_STRIDE * HEAD_DIM
    hidden = w1.shape[-1]
    x2 = x_tok.reshape(S, n, width)
    pos_flat = jnp.broadcast_to(pos_emb.reshape(S, 1, 2 * width), (S, SUBLANES, 2 * width))
    return pl.pallas_call(
        _compress_kernel,
        out_shape=jax.ShapeDtypeStruct((S, n, HEAD_DIM), BF16),
        grid=(S,),
        in_specs=[pl.BlockSpec((1, n, width), lambda s: (s, 0, 0)),
                  pl.BlockSpec((1, SUBLANES, 2 * width), lambda s: (s, 0, 0)),
                  pl.BlockSpec((1, 2 * width, hidden), lambda s: (s, 0, 0)),
                  pl.BlockSpec((1, hidden, HEAD_DIM), lambda s: (s, 0, 0))],
        out_specs=pl.BlockSpec((1, n, HEAD_DIM), lambda s: (s, 0, 0)),
        compiler_params=_params(("parallel",)),
        name="nsa_compress",
    )(x2, pos_flat, w1, w2)


def _nsa_kernel(q_ref, kc_ref, vc_ref, ks_ref, vs_ref, kw_ref, vw_ref, gate_ref, c2s_t_ref, o_ref,
                m_ref, mb_ref, acc_ref, s_ref, *, blk, tk):
    R = NSA_REP
    qi = pl.program_id(1)
    q0 = qi * blk
    n_cmp = kc_ref.shape[1]
    n_sel = c2s_t_ref.shape[0]
    rows = [slice(r * blk, (r + 1) * blk) for r in range(R)]
    qs = jnp.concatenate([q_ref[:, r * HEAD_DIM:(r + 1) * HEAD_DIM] for r in range(R)], axis=0)
    q_pos = q0 + lax.broadcasted_iota(jnp.int32, (blk, 1), 0)
    q_lane = q0 + lax.broadcasted_iota(jnp.int32, (1, blk), 1)

    def masked_softmax(s, mask, axis):
        s = jnp.where(mask, s, MASK_FILL)
        p = jnp.where(mask, jnp.exp(s - jnp.max(s, axis=axis, keepdims=True)), 0.0)
        return p / jnp.maximum(jnp.sum(p, axis=axis, keepdims=True), 1e-30)

    kc = kc_ref[0]
    s_all = _dot_nt(qs, kc)
    cmp_end = lax.broadcasted_iota(jnp.int32, (1, n_cmp), 1) * CMP_STRIDE + (CMP_BLOCK - 1)
    mask_c = cmp_end <= q_pos
    p_cmp = [masked_softmax(s_all[rows[r]], mask_c, 1).astype(BF16) for r in range(R)]
    o_cmp = _dot(jnp.concatenate(p_cmp, axis=0), vc_ref[0])

    wk = WINDOW + blk
    ws = pl.multiple_of(jnp.maximum(q0 - WINDOW, 0), blk)
    dist = q_pos - (ws + lax.broadcasted_iota(jnp.int32, (1, wk), 1))
    bias_w = jnp.where((dist >= 0) & (dist < WINDOW), 0.0, MASK_FILL)
    k_w = kw_ref[pl.ds(ws, wk), :]
    s_w = [_dot_nt(qs[rows[r]], k_w) for r in range(R)]

    def window_head(r):
        sc = s_w[r] + bias_w
        p = jnp.exp(sc - jnp.max(sc, axis=-1, keepdims=True)).astype(BF16)
        pv = _dot(p, vw_ref[pl.ds(ws, wk), :])
        return pv[:, :HEAD_DIM] / pv[:, HEAD_DIM:]

    st_all = _dot_nt(kc, qs)
    cmp_end_t = lax.broadcasted_iota(jnp.int32, (n_cmp, 1), 0) * CMP_STRIDE + (CMP_BLOCK - 1)
    bias_t = jnp.where(cmp_end_t <= q_lane, 0.0, MASK_FILL)
    p_sum = None
    for r in range(R):
        st = st_all[:, rows[r]] + bias_t
        p = jnp.exp(st - jnp.max(st, axis=0, keepdims=True))
        p = p / jnp.sum(p, axis=0, keepdims=True)
        p_sum = p if p_sum is None else p_sum + p
    hi = p_sum.astype(BF16)
    lo = (p_sum - hi.astype(F32)).astype(BF16)
    imp = _dot(c2s_t_ref[...], hi) + _dot(c2s_t_ref[...], lo)
    blk_id = lax.broadcasted_iota(jnp.int32, (n_sel, 1), 0)
    forced = (blk_id == 0) | (blk_id == (q_lane >> SEL_SHIFT))
    valid = blk_id * SEL_BLOCK <= q_lane
    work = jnp.where(forced, jnp.inf, jnp.where(valid, imp, -jnp.inf))
    blk_idf = blk_id.astype(F32)
    left = work
    o_win = []
    n_steps = min(SEL_TOPK, n_sel)
    for step in range(n_steps):
        top = jnp.max(left, axis=0, keepdims=True)
        first = jnp.min(jnp.where(left == top, blk_idf, float(n_sel)), axis=0, keepdims=True)
        left = jnp.where(blk_idf == first, -jnp.inf, left)
        while len(o_win) * n_steps < (step + 1) * R:
            o_win.append(window_head(len(o_win)))
    sel_bias = jnp.where(left != work, 0.0, MASK_FILL).T.astype(BF16)

    gates = jax.nn.sigmoid(gate_ref[...])

    def gate(r, branch):
        return gates[:, r * N_GATES + branch:r * N_GATES + branch + 1]

    o_ref[...] = jnp.concatenate([gate(r, 0) * o_cmp[rows[r]] + gate(r, 2) * o_win[r] for r in range(R)], axis=1)

    tok = lax.broadcasted_iota(jnp.int32, (n_sel, tk), 1)
    sel_row = lax.broadcasted_iota(jnp.int32, (n_sel, tk), 0)
    key_off = lax.broadcasted_iota(jnp.int32, (1, tk), 1)

    def key_rows(kb):
        return pl.ds(pl.multiple_of(kb * tk, tk), tk)

    def block_bias(kb):
        ks = kb * tk
        expand = jnp.where(((ks + tok) >> SEL_SHIFT) == sel_row, 1.0, 0.0).astype(BF16)
        bias = _dot(sel_bias, expand)
        return jnp.where(ks + key_off <= q_pos, bias, MASK_FILL)

    def put_scores(r, k, bias):
        sc = _dot_nt(qs[rows[r]], k) + bias
        s_ref[rows[r], :] = sc
        mb_ref[rows[r], :] = jnp.max(sc, axis=-1, keepdims=True)

    def take_probs(r):
        m_old = m_ref[rows[r], :]
        m_new = jnp.maximum(m_old, mb_ref[rows[r], :])
        m_ref[rows[r], :] = m_new
        return jnp.exp(s_ref[rows[r], :] - m_new).astype(BF16), jnp.exp(m_old - m_new)

    def accumulate(r, p, alpha, v1):
        acc_ref[rows[r], :] = alpha * acc_ref[rows[r], :] + _dot(p, v1)

    m_ref[...] = jnp.full_like(m_ref, MASK_FILL)
    acc_ref[...] = jnp.zeros_like(acc_ref)
    kb_diag = qi // (tk // blk)
    bias = block_bias(0)
    for r in range(R):
        put_scores(r, ks_ref[key_rows(0), :], bias)

    def sel_body(kb, carry):
        k_next = ks_ref[key_rows(kb + 1), :]
        bias_next = block_bias(kb + 1)
        v1 = vs_ref[key_rows(kb), :]
        for r in range(R):
            p, alpha = take_probs(r)
            put_scores(r, k_next, bias_next)
            accumulate(r, p, alpha, v1)
        return carry

    lax.fori_loop(0, kb_diag, sel_body, 0)
    v1 = vs_ref[key_rows(kb_diag), :]
    for r in range(R):
        p, alpha = take_probs(r)
        accumulate(r, p, alpha, v1)
    o_sel = acc_ref[:, :HEAD_DIM] / acc_ref[:, HEAD_DIM:]
    o_ref[...] += jnp.concatenate([gate(r, 1) * o_sel[rows[r]] for r in range(R)], axis=1)


def _nsa_attention(q_tok, v_aug, k_cmp, v_cmp, gates, ks_col, kw_col, vs_col, vw_col, gate_col,
                   blk=256, tk=1024):
    T = q_tok.shape[0]
    G = NSA_KV_GROUPS
    blk = min(blk, T)
    tk = min(tk, T)
    assert T % tk == 0 and tk % blk == 0 and WINDOW % blk == 0 and WINDOW + blk <= T
    n_cmp = T // CMP_STRIDE
    n_sel = T // SEL_BLOCK
    cmp_start = np.arange(n_cmp)[None, :] * CMP_STRIDE
    sel_start = np.arange(n_sel)[:, None] * SEL_BLOCK
    overlap = np.clip(np.minimum(cmp_start + CMP_BLOCK, sel_start + SEL_BLOCK)
                      - np.maximum(cmp_start, sel_start), 0, None)
    c2s_t = jnp.asarray(overlap.astype(np.float32) / CMP_BLOCK, dtype=BF16)
    width = NSA_REP * HEAD_DIM
    key_spec = lambda col: pl.BlockSpec((T, HEAD_DIM), lambda g, i: (0, col + g))
    val_spec = lambda col: pl.BlockSpec((T, 2 * HEAD_DIM), lambda g, i: (0, col + g))
    cmp_spec = pl.BlockSpec((1, n_cmp, HEAD_DIM), lambda g, i: (g, 0, 0))
    return pl.pallas_call(
        functools.partial(_nsa_kernel, blk=blk, tk=tk),
        out_shape=jax.ShapeDtypeStruct((T, G * width), F32),
        grid=(G, T // blk),
        in_specs=[pl.BlockSpec((blk, width), lambda g, i: (i, g)),
                  cmp_spec, cmp_spec,
                  key_spec(ks_col), val_spec(vs_col), key_spec(kw_col), val_spec(vw_col),
                  pl.BlockSpec((blk, LANES), lambda g, i: (i, gate_col + g)),
                  pl.BlockSpec((n_sel, n_cmp), lambda g, i: (0, 0))],
        out_specs=pl.BlockSpec((blk, width), lambda g, i: (i, g)),
        scratch_shapes=[pltpu.VMEM((NSA_REP * blk, 1), F32),
                        pltpu.VMEM((NSA_REP * blk, 1), F32),
                        pltpu.VMEM((NSA_REP * blk, 2 * HEAD_DIM), F32),
                        pltpu.VMEM((NSA_REP * blk, tk), F32)],
        compiler_params=_params(("parallel", "arbitrary")),
        name="nsa_attention",
    )(q_tok, k_cmp, v_cmp, q_tok, v_aug, q_tok, v_aug, gates, c2s_t)


def _cross_kernel(q_ref, k_ref, v_ref, o_ref):
    s = _dot_nt(q_ref[...], k_ref[...])
    p = jnp.exp(s - jnp.max(s, axis=-1, keepdims=True))
    p = p / jnp.sum(p, axis=-1, keepdims=True)
    o_ref[...] = _dot(p.astype(BF16), v_ref[...]).astype(o_ref.dtype)


def _cross_attention(q, kv, n_heads, tm=1024):
    T, D = q.shape
    M = kv.shape[0]
    dh = D // n_heads
    tm = min(tm, T)
    return pl.pallas_call(
        _cross_kernel,
        out_shape=jax.ShapeDtypeStruct((T, D), BF16),
        grid=(T // tm, n_heads),
        in_specs=[pl.BlockSpec((tm, dh), lambda i, h: (i, h)),
                  pl.BlockSpec((M, dh), lambda i, h: (0, h)),
                  pl.BlockSpec((M, dh), lambda i, h: (0, n_heads + h))],
        out_specs=pl.BlockSpec((tm, dh), lambda i, h: (i, h)),
        compiler_params=_params(("parallel", "arbitrary")),
        name="cross_attention",
    )(q, kv, kv)


def _rope_tables(T):
    half = HEAD_DIM // 2
    inv_freq = ROPE_THETA ** (-np.arange(half, dtype=np.float64) / half)
    ang = np.arange(T, dtype=np.float64)[:, None] * inv_freq[None, :]
    cos, sin = np.cos(ang).astype(np.float32), np.sin(ang).astype(np.float32)
    return jnp.asarray(np.concatenate([cos, cos], axis=1)), jnp.asarray(np.concatenate([-sin, sin], axis=1))


def _pad_to(w, axis, mult):
    pad = -w.shape[axis] % mult
    if not pad:
        return w
    shape = list(w.shape)
    shape[axis] = pad
    return jnp.concatenate([w, jnp.zeros(shape, w.dtype)], axis=axis)


def _ffn(x_f32, x_bf16, w_gate, w_up, w_down, ln_g, ln_b, alpha, emit_bf16):
    d_ff = w_gate.shape[1]
    main = d_ff // FF_TILE * FF_TILE
    assert (d_ff - main) % LANES == 0 and main % (FF_DOWN_STEPS * LANES) == 0
    h = _swiglu_up(x_bf16, w_gate, w_up, FF_TILE, main // FF_TILE)
    wd = w_down.astype(BF16)
    tail = None
    if main < d_ff:
        tail = (_swiglu_up(x_bf16, w_gate[:, main:], w_up[:, main:], d_ff - main, 1), wd[main:])
    return _mm_res_ln(h, wd, x_f32, ln_g, ln_b, alpha, 0.5, emit_bf16, n_k=FF_DOWN_STEPS, tail=tail)


def _mixer(x_f32, x_bf16, w_in, cmp_pos_k, cmp_w1_k, cmp_w2_k, cmp_pos_v, cmp_w1_v, cmp_w2_v,
           mix_norm_g, w_out, ln_g, ln_b, alpha):
    T = x_f32.shape[0]
    G = NSA_KV_GROUPS
    sbw = SB_HEADS * HEAD_DIM
    nqw = NSA_HEADS * HEAD_DIM
    kvw = G * HEAD_DIM
    bounds = np.cumsum([0, sbw, sbw, sbw, nqw, kvw, kvw, kvw, kvw, kvw, kvw, NSA_HEADS * N_GATES])
    (w_sbq, w_sbk, w_sbv, w_nq, w_kc, w_vc, w_ks, w_vs, w_kw, w_vw, w_gate) = [
        w_in[:, bounds[i]:bounds[i + 1]] for i in range(11)]
    per_group = NSA_REP * N_GATES
    w_gate = jnp.concatenate([_pad_to(w_gate[:, g * per_group:(g + 1) * per_group], 1, LANES)
                              for g in range(G)], axis=1)
    tables = _rope_tables(T)
    cat = lambda ws: jnp.concatenate(ws, axis=1).astype(BF16)
    q_scale = lambda n_q, n_rest: jnp.concatenate([jnp.full((n_q,), HEAD_DIM ** -0.5, F32), jnp.ones((n_rest,), F32)])
    plain = _project(x_bf16, cat([w_sbq, w_sbk, w_sbv, w_vs, w_vw]), BF16, 1024, 512,
                     col_scale=q_scale(sbw, 2 * sbw + 2 * kvw))
    roped = _project(x_bf16, cat([w_nq, w_ks, w_kw]), BF16, 1024, 512,
                     col_scale=q_scale(nqw, 2 * kvw), rope_tables=tables)
    kc = _project(x_bf16, w_kc.astype(BF16), F32, 1024, kvw, rope_tables=tables)
    vc_gate = _project(x_bf16, cat([w_vc, w_gate]), F32, 1024, 2 * kvw)

    o_sb = _sb_attention(plain, SB_HEADS, 0, SB_HEADS, 2 * SB_HEADS)

    head = lambda a, g: a[:, g * HEAD_DIM:(g + 1) * HEAD_DIM]
    streams = jnp.stack([head(kc, g) for g in range(G)] + [head(vc_gate, g) for g in range(G)])
    rep = lambda a, b: jnp.stack([a] * G + [b] * G)
    cmp = _compress(streams, rep(cmp_pos_k, cmp_pos_v), rep(cmp_w1_k, cmp_w1_v), rep(cmp_w2_k, cmp_w2_v))
    ones = jnp.ones((T, HEAD_DIM), BF16)
    v_aug = jnp.concatenate([a for h in range(2 * G) for a in (head(plain, 3 * SB_HEADS + h), ones)], axis=1)
    o_nsa = _nsa_attention(roped, v_aug, cmp[:G], cmp[G:], vc_gate,
                           ks_col=NSA_HEADS, kw_col=NSA_HEADS + G, vs_col=0, vw_col=G, gate_col=G)
    return _rms_mm_res_ln(o_sb, o_nsa, mix_norm_g, w_out.astype(BF16), x_f32, ln_g, ln_b, alpha)


def _memory_block(x_f32, x_bf16, mem, w_q, w_k, w_v, w_o, ln_g, ln_b, alpha):
    D = w_q.shape[1]
    q = _project(x_bf16, w_q.astype(BF16), BF16, 1024, 1024,
                 col_scale=jnp.full((D,), (D // MEM_HEADS) ** -0.5, F32))
    kv = _project(mem.astype(BF16), jnp.concatenate([w_k, w_v], axis=1).astype(BF16), BF16, 256, 1024)
    o = _cross_attention(q, kv, MEM_HEADS)
    return _mm_res_ln(o, w_o.astype(BF16), x_f32, ln_g, ln_b, alpha, 1.0, True)


def kernel(x, mem, ln1_g, ln1_b, ffn1_gate, ffn1_up, ffn1_down, w_in, cmp_pos_k, cmp_w1_k, cmp_w2_k, cmp_pos_v, cmp_w1_v, cmp_w2_v, mix_norm_g, w_out, ln2_g, ln2_b, mem_wq, mem_wk, mem_wv, mem_wo, ln3_g, ln3_b, ffn2_gate, ffn2_up, ffn2_down, ln4_g, ln4_b):
    n_layers = ffn1_gate.shape[0]
    alpha = (2 * n_layers) ** 0.25
    outs = []
    for bi in range(x.shape[0]):
        xf = x[bi]
        xb = xf.astype(BF16)
        for l in range(n_layers):
            xf, xb = _ffn(xf, xb, ffn1_gate[l], ffn1_up[l], ffn1_down[l], ln1_g[l], ln1_b[l], alpha, True)
            xf, xb = _mixer(xf, xb, w_in[l], cmp_pos_k[l], cmp_w1_k[l], cmp_w2_k[l], cmp_pos_v[l], cmp_w1_v[l],
                            cmp_w2_v[l], mix_norm_g[l], w_out[l], ln2_g[l], ln2_b[l], alpha)
            xf, xb = _memory_block(xf, xb, mem[bi], mem_wq[l], mem_wk[l], mem_wv[l], mem_wo[l], ln3_g[l], ln3_b[l],
                                   alpha)
            xf, xb = _ffn(xf, xb, ffn2_gate[l], ffn2_up[l], ffn2_down[l], ln4_g[l], ln4_b[l], alpha,
                          l + 1 < n_layers)
        outs.append(xf)
    return outs[0][None] if len(outs) == 1 else jnp.stack(outs)
```

```python
import functools

import numpy as np
import jax
import jax.numpy as jnp
from jax import lax
from jax.experimental import pallas as pl
from jax.experimental.pallas import tpu as pltpu

HEAD_DIM = 128
SB_HEADS = 8
NSA_HEADS = 8
NSA_KV_GROUPS = 2
NSA_REP = NSA_HEADS // NSA_KV_GROUPS
N_GATES = 3
CMP_BLOCK = 32
CMP_STRIDE = 16
SEL_BLOCK = 64
SEL_SHIFT = 6
SEL_TOPK = 16
WINDOW = 512
MEM_HEADS = 4
ROPE_THETA = 10000.0
LN_EPS = 1e-5
RMS_EPS = 1e-6
MASK_FILL = -1e30
LOG2E = 1.4426950408889634
SB_UNDERFLOW = -110.0
assert 1 << SEL_SHIFT == SEL_BLOCK

LANES = 128
SUBLANES = 8
FF_TILE = 512
FF_DOWN_STEPS = 4
VMEM_LIMIT = 56 * 1024 * 1024

BF16 = jnp.bfloat16
F32 = jnp.float32


def _params(sem):
    return pltpu.CompilerParams(dimension_semantics=sem, vmem_limit_bytes=VMEM_LIMIT)


def _dot(a, b):
    return jnp.dot(a, b, preferred_element_type=F32)


def _dot_nt(a, b):
    return lax.dot_general(a, b, (((1,), (1,)), ((), ())), preferred_element_type=F32)


def _split_dot(a, b):
    hi = a.astype(BF16)
    lo = (a - hi.astype(F32)).astype(BF16)
    return _dot(hi, b) + _dot(lo, b)


def _layer_norm(z, g, b):
    mu = jnp.mean(z, axis=-1, keepdims=True)
    zc = z - mu
    var = jnp.mean(zc * zc, axis=-1, keepdims=True)
    return zc * lax.rsqrt(var + LN_EPS) * g + b


def _proj_kernel(x_ref, w_ref, *rest, scaled, rope):
    rest = list(rest)
    o_ref = rest.pop()
    y = _dot(x_ref[...], w_ref[...])
    if scaled:
        y = y * rest.pop(0)[...]
    if rope:
        cos_ref, sin_ref = rest
        c = cos_ref[...]
        s = sin_ref[...]
        heads = []
        for h in range(y.shape[1] // HEAD_DIM):
            yh = y[:, h * HEAD_DIM:(h + 1) * HEAD_DIM]
            heads.append(yh * c + pltpu.roll(yh, HEAD_DIM // 2, 1) * s)
        y = jnp.concatenate(heads, axis=1) if len(heads) > 1 else heads[0]
    o_ref[...] = y.astype(o_ref.dtype)


def _project(x, w, out_dtype, tm, tn, col_scale=None, rope_tables=None):
    M, K = x.shape
    N = w.shape[1]
    tm = min(tm, M)
    assert M % tm == 0 and N % tn == 0
    in_specs = [pl.BlockSpec((tm, K), lambda i, j: (i, 0)),
                pl.BlockSpec((K, tn), lambda i, j: (0, j))]
    args = [x, w]
    if col_scale is not None:
        in_specs.append(pl.BlockSpec((1, tn), lambda i, j: (0, j)))
        args.append(col_scale.reshape(1, N))
    if rope_tables is not None:
        in_specs += [pl.BlockSpec((tm, HEAD_DIM), lambda i, j: (i, 0))] * 2
        args += list(rope_tables)
    return pl.pallas_call(
        functools.partial(_proj_kernel, scaled=col_scale is not None, rope=rope_tables is not None),
        out_shape=jax.ShapeDtypeStruct((M, N), out_dtype),
        grid=(M // tm, N // tn),
        in_specs=in_specs,
        out_specs=pl.BlockSpec((tm, tn), lambda i, j: (i, j)),
        compiler_params=_params(("parallel", "arbitrary")),
        name="proj_rope" if rope_tables is not None else "proj",
    )(*args)


def _swiglu_up_kernel(x_ref, wg_ref, wu_ref, o_ref, wg_bf16, wu_bf16):
    @pl.when(pl.program_id(1) == 0)
    def _():
        wg_bf16[...] = wg_ref[...].astype(BF16)
        wu_bf16[...] = wu_ref[...].astype(BF16)

    x = x_ref[...]
    g = _dot(x, wg_bf16[...])
    u = _dot(x, wu_bf16[...])
    o_ref[...] = (jax.nn.silu(g) * u).astype(o_ref.dtype)


def _swiglu_up(x, wg, wu, tn, n_blocks, tm=1024):
    M, K = x.shape
    tm = min(tm, M)
    assert M % tm == 0 and n_blocks * tn <= wg.shape[1] and wg.shape == wu.shape
    return pl.pallas_call(
        _swiglu_up_kernel,
        out_shape=jax.ShapeDtypeStruct((M, n_blocks * tn), BF16),
        grid=(n_blocks, M // tm),
        in_specs=[pl.BlockSpec((tm, K), lambda j, i: (i, 0)),
                  pl.BlockSpec((K, tn), lambda j, i: (0, j)),
                  pl.BlockSpec((K, tn), lambda j, i: (0, j))],
        out_specs=pl.BlockSpec((tm, tn), lambda j, i: (i, j)),
        scratch_shapes=[pltpu.VMEM((K, tn), BF16), pltpu.VMEM((K, tn), BF16)],
        compiler_params=_params(("parallel", "arbitrary")),
        name="swiglu_up",
    )(x, wg, wu)


def _row_chunks(tm, n_split):
    return [slice(c * tm // n_split, (c + 1) * tm // n_split) for c in range(n_split)]


def _res_ln_store(chunks, totals, res_ref, g_ref, b_ref, of_ref, ob_ref, alpha, coef):
    for c, y in zip(chunks, totals):
        out = _layer_norm(alpha * res_ref[c, :] + coef * y, g_ref[...], b_ref[...])
        of_ref[c, :] = out
        if ob_ref is not None:
            ob_ref[c, :] = out.astype(BF16)


def _mm_res_ln_kernel(*refs, alpha, coef, emit_bf16, n_k, n_split, has_tail):
    refs = list(refs)
    h_ref, w_ref = refs.pop(0), refs.pop(0)
    ht_ref, wt_ref = (refs.pop(0), refs.pop(0)) if has_tail else (None, None)
    res_ref, g_ref, b_ref, of_ref = refs.pop(0), refs.pop(0), refs.pop(0), refs.pop(0)
    ob_ref = refs.pop(0) if emit_bf16 else None
    acc_ref = refs.pop(0) if n_k > 1 else None
    chunks = _row_chunks(h_ref.shape[0], n_split)

    def partial_products():
        return [_dot(h_ref[c, :], w_ref[...]) for c in chunks]

    def finish(totals):
        if has_tail:
            totals = [y + _dot(ht_ref[c, :], wt_ref[...]) for c, y in zip(chunks, totals)]
        _res_ln_store(chunks, totals, res_ref, g_ref, b_ref, of_ref, ob_ref, alpha, coef)

    if n_k == 1:
        finish(partial_products())
        return
    k = pl.program_id(1)

    @pl.when(k == 0)
    def _():
        for c, y in zip(chunks, partial_products()):
            acc_ref[c, :] = y

    @pl.when((k > 0) & (k < n_k - 1))
    def _():
        for c, y in zip(chunks, partial_products()):
            acc_ref[c, :] += y

    @pl.when(k == n_k - 1)
    def _():
        finish([acc_ref[c, :] + y for c, y in zip(chunks, partial_products())])


def _mm_res_ln(h, w, res, g, b, alpha, coef, emit_bf16, tm=512, n_k=1, n_split=2, tail=None):
    M, K = h.shape
    N = w.shape[1]
    tm = min(tm, M)
    tk = K // n_k
    assert M % tm == 0 and K % n_k == 0 and K <= w.shape[0] and (n_k == 1 or tk % LANES == 0)
    row_block = pl.BlockSpec((tm, N), lambda i, k: (i, 0))
    vec = pl.BlockSpec((1, N), lambda i, k: (0, 0))
    in_specs = [pl.BlockSpec((tm, tk), lambda i, k: (i, k)), pl.BlockSpec((tk, N), lambda i, k: (k, 0))]
    args = [h, w]
    if tail is not None:
        kt = tail[0].shape[1]
        in_specs += [pl.BlockSpec((tm, kt), lambda i, k: (i, 0)), pl.BlockSpec((kt, N), lambda i, k: (0, 0))]
        args += list(tail)
    out = pl.pallas_call(
        functools.partial(_mm_res_ln_kernel, alpha=alpha, coef=coef, emit_bf16=emit_bf16,
                          n_k=n_k, n_split=n_split, has_tail=tail is not None),
        out_shape=[jax.ShapeDtypeStruct((M, N), F32)] + [jax.ShapeDtypeStruct((M, N), BF16)] * emit_bf16,
        grid=(M // tm, n_k),
        in_specs=in_specs + [row_block, vec, vec],
        out_specs=[row_block] * (1 + emit_bf16),
        scratch_shapes=[pltpu.VMEM((tm, N), F32)] if n_k > 1 else [],
        compiler_params=_params(("parallel", "arbitrary")),
        name="mm_res_ln",
    )(*args, res, g.reshape(1, N), b.reshape(1, N))
    return (out[0], out[1]) if emit_bf16 else (out[0], None)


def _rms_mm_res_ln_kernel(a_ref, b_ref, gain_ref, w_ref, res_ref, g_ref, beta_ref, of_ref, ob_ref, *,
                          alpha, n_split):
    chunks = _row_chunks(a_ref.shape[0], n_split)
    wa = a_ref.shape[1]
    gain = gain_ref[...]

    def rms(o, gn):
        return (o * lax.rsqrt(jnp.mean(o * o, axis=-1, keepdims=True) + RMS_EPS) * gn).astype(BF16)

    totals = [_dot(jnp.concatenate([rms(a_ref[c, :], gain[:, :wa]), rms(b_ref[c, :], gain[:, wa:])], axis=1),
                   w_ref[...]) for c in chunks]
    _res_ln_store(chunks, totals, res_ref, g_ref, beta_ref, of_ref, ob_ref, alpha, 1.0)


def _rms_mm_res_ln(o_a, o_b, gain, w, res, g, b, alpha, tm=512, n_split=2):
    M, wa = o_a.shape
    wb = o_b.shape[1]
    N = w.shape[1]
    tm = min(tm, M)
    assert M % tm == 0 and w.shape[0] == wa + wb
    row_block = pl.BlockSpec((tm, N), lambda i: (i, 0))
    vec = pl.BlockSpec((1, N), lambda i: (0, 0))
    return pl.pallas_call(
        functools.partial(_rms_mm_res_ln_kernel, alpha=alpha, n_split=n_split),
        out_shape=[jax.ShapeDtypeStruct((M, N), F32), jax.ShapeDtypeStruct((M, N), BF16)],
        grid=(M // tm,),
        in_specs=[pl.BlockSpec((tm, wa), lambda i: (i, 0)),
                  pl.BlockSpec((tm, wb), lambda i: (i, 0)),
                  pl.BlockSpec((1, wa + wb), lambda i: (0, 0)),
                  pl.BlockSpec((wa + wb, N), lambda i: (0, 0)),
                  row_block, vec, vec],
        out_specs=[row_block, row_block],
        compiler_params=_params(("parallel",)),
        name="rms_mm_res_ln",
    )(o_a, o_b, gain.reshape(1, wa + wb), w, res, g.reshape(1, N), b.reshape(1, N))


def _sb_kernel(q_ref, k_ref, v_ref, o_ref, acc_ref, c_ref, *, blk, heads):
    qi = pl.program_id(1)
    row = lax.broadcasted_iota(jnp.int32, (blk, blk), 0)
    col = lax.broadcasted_iota(jnp.int32, (blk, blk), 1)
    suffix = jnp.where(row > col, 1.0, 0.0).astype(BF16)
    strict = col < row

    cols = [slice(h * HEAD_DIM, (h + 1) * HEAD_DIM) for h in range(heads)]

    def step(kb_far, n_blocks, diagonal):
        ks = pl.multiple_of(kb_far * blk, blk)
        key_rows = [pl.ds(ks + b * blk, blk) for b in range(n_blocks)]
        log_beta, log_1m = {}, {}
        for p in [(b, h) for b in reversed(range(n_blocks)) for h in range(heads)]:
            y = _dot_nt(q_ref[:, cols[p[1]]], k_ref[key_rows[p[0]], cols[p[1]]])
            lb = jnp.minimum(y, 0.0) - jnp.log(1.0 + jnp.exp2(jnp.abs(y) * -LOG2E))
            l1 = lb - y
            if diagonal:
                l1 = jnp.where(strict, l1, 0.0)
            log_beta[p] = lb
            log_1m[p] = l1.astype(BF16)
        weights = {}
        for h in range(heads):
            c = c_ref[h]
            for b in reversed(range(n_blocks)):
                p = (b, h)
                log_stay = _dot(log_1m[p], suffix)
                w = jnp.exp(log_beta[p] + (log_stay + c))
                if diagonal:
                    w = jnp.where(strict, w, 0.0)
                weights[p] = w.astype(BF16)
                c = c + (log_stay[:, :1] + log_1m[p][:, :1].astype(F32))
            c_ref[h] = c
        for h in range(heads):
            w = jnp.concatenate([weights[(b, h)] for b in range(n_blocks)], axis=1)
            acc_ref[:, cols[h]] += _dot(w, v_ref[pl.ds(ks, n_blocks * blk), cols[h]])

    acc_ref[...] = jnp.zeros_like(acc_ref)
    c_ref[...] = jnp.zeros_like(c_ref)
    step(qi, 1, True)

    def more(carry):
        done, c_max = carry
        return (done < qi) & (c_max >= SB_UNDERFLOW)

    def body(carry):
        done, _ = carry
        step(qi - 1 - done, 1, False)
        return done + 1, jnp.max(c_ref[...])

    lax.while_loop(more, body, (jnp.int32(0), jnp.max(c_ref[...])))
    o_ref[...] = acc_ref[...]


def _sb_attention(qkv, n_heads, q_col, k_col, v_col, blk=256, heads=4):
    T = qkv.shape[0]
    blk = min(blk, T)
    width = heads * HEAD_DIM
    assert T % blk == 0 and n_heads % heads == 0
    assert q_col % heads == 0 and k_col % heads == 0 and v_col % heads == 0
    return pl.pallas_call(
        functools.partial(_sb_kernel, blk=blk, heads=heads),
        out_shape=jax.ShapeDtypeStruct((T, n_heads * HEAD_DIM), F32),
        grid=(n_heads // heads, T // blk),
        in_specs=[pl.BlockSpec((blk, width), lambda h, i: (i, q_col // heads + h)),
                  pl.BlockSpec((T, width), lambda h, i: (0, k_col // heads + h)),
                  pl.BlockSpec((T, width), lambda h, i: (0, v_col // heads + h))],
        out_specs=pl.BlockSpec((blk, width), lambda h, i: (i, h)),
        scratch_shapes=[pltpu.VMEM((blk, width), F32), pltpu.VMEM((heads, blk, 1), F32)],
        compiler_params=_params(("parallel", "arbitrary")),
        name="sb_attention",
    )(qkv, qkv, qkv)


def _compress_kernel(x_ref, pos_ref, w1_ref, w2_ref, o_ref):
    n = x_ref.shape[1]
    half = CMP_STRIDE * HEAD_DIM
    x = x_ref[0].astype(BF16)
    w1 = w1_ref[0].astype(BF16)
    a = _dot(x, w1[:half])
    b = _dot(x, w1[half:])
    pos = _dot(pos_ref[0].astype(BF16), w1)
    b_next = pltpu.roll(b, n - 1, 0)
    hid = jax.nn.gelu(a + b_next + pos[0:1])
    out = _dot(hid.astype(BF16), w2_ref[0].astype(BF16))
    valid = lax.broadcasted_iota(jnp.int32, out.shape, 0) < n - 1
    o_ref[0] = jnp.where(valid, out, 0.0).astype(o_ref.dtype)


def _compress(x_tok, pos_emb, w1, w2):
    S, T, _ = x_tok.shape
    n = T // CMP_STRIDE
    width = CMP_STRIDE * HEAD_DIM
    hidden = w1.shape[-1]
    x2 = x_tok.reshape(S, n, width)
    pos_flat = jnp.broadcast_to(pos_emb.reshape(S, 1, 2 * width), (S, SUBLANES, 2 * width))
    return pl.pallas_call(
        _compress_kernel,
        out_shape=jax.ShapeDtypeStruct((S, n, HEAD_DIM), BF16),
        grid=(S,),
        in_specs=[pl.BlockSpec((1, n, width), lambda s: (s, 0, 0)),
                  pl.BlockSpec((1, SUBLANES, 2 * width), lambda s: (s, 0, 0)),
                  pl.BlockSpec((1, 2 * width, hidden), lambda s: (s, 0, 0)),
                  pl.BlockSpec((1, hidden, HEAD_DIM), lambda s: (s, 0, 0))],
        out_specs=pl.BlockSpec((1, n, HEAD_DIM), lambda s: (s, 0, 0)),
        compiler_params=_params(("parallel",)),
        name="nsa_compress",
    )(x2, pos_flat, w1, w2)


def _nsa_kernel(q_ref, kc_ref, vc_ref, ks_ref, vs_ref, kw_ref, vw_ref, gate_ref, c2s_t_ref, o_ref,
                m_ref, mb_ref, acc_ref, s_ref, *, blk, tk):
    R = NSA_REP
    qi = pl.program_id(1)
    q0 = qi * blk
    n_cmp = kc_ref.shape[1]
    n_sel = c2s_t_ref.shape[0]
    rows = [slice(r * blk, (r + 1) * blk) for r in range(R)]
    qs = jnp.concatenate([q_ref[:, r * HEAD_DIM:(r + 1) * HEAD_DIM] for r in range(R)], axis=0)
    q_pos = q0 + lax.broadcasted_iota(jnp.int32, (blk, 1), 0)
    q_lane = q0 + lax.broadcasted_iota(jnp.int32, (1, blk), 1)

    def softmax_av(s, bias, v1):
        sc = s + bias
        p = jnp.exp(sc - jnp.max(sc, axis=-1, keepdims=True)).astype(BF16)
        pv = _dot(p, v1)
        return pv[:, :HEAD_DIM] / pv[:, HEAD_DIM:]

    kc = kc_ref[0]
    s_cmp = [_dot_nt(qs[rows[r]], kc) for r in range(R)]
    cmp_end = lax.broadcasted_iota(jnp.int32, (1, n_cmp), 1) * CMP_STRIDE + (CMP_BLOCK - 1)
    bias_c = jnp.where(cmp_end <= q_pos, 0.0, MASK_FILL)
    o_cmp = [jnp.where(q_pos >= CMP_BLOCK - 1, softmax_av(s_cmp[r], bias_c, vc_ref[0]), 0.0) for r in range(R)]

    wk = WINDOW + blk
    ws = pl.multiple_of(jnp.maximum(q0 - WINDOW, 0), blk)
    dist = q_pos - (ws + lax.broadcasted_iota(jnp.int32, (1, wk), 1))
    bias_w = jnp.where((dist >= 0) & (dist < WINDOW), 0.0, MASK_FILL)
    k_w = kw_ref[pl.ds(ws, wk), :]
    s_w = [_dot_nt(qs[rows[r]], k_w) for r in range(R)]

    st_all = _dot_nt(kc, qs)
    cmp_end_t = lax.broadcasted_iota(jnp.int32, (n_cmp, 1), 0) * CMP_STRIDE + (CMP_BLOCK - 1)
    bias_t = jnp.where(cmp_end_t <= q_lane, 0.0, MASK_FILL)
    p_sum = None
    for r in range(R):
        st = st_all[:, rows[r]] + bias_t
        p = jnp.exp(st - jnp.max(st, axis=0, keepdims=True))
        p = p / jnp.sum(p, axis=0, keepdims=True)
        p_sum = p if p_sum is None else p_sum + p
    hi = p_sum.astype(BF16)
    lo = (p_sum - hi.astype(F32)).astype(BF16)
    imp = _dot(c2s_t_ref[...], hi) + _dot(c2s_t_ref[...], lo)
    blk_id = lax.broadcasted_iota(jnp.int32, (n_sel, 1), 0)
    forced = (blk_id == 0) | (blk_id == (q_lane >> SEL_SHIFT))
    valid = blk_id * SEL_BLOCK <= q_lane
    work = jnp.where(forced, jnp.inf, jnp.where(valid, imp, -jnp.inf))
    blk_idf = blk_id.astype(F32)
    left = work
    o_win = []
    n_steps = min(SEL_TOPK, n_sel)
    for step in range(n_steps):
        top = jnp.max(left, axis=0, keepdims=True)
        first = jnp.min(jnp.where(left == top, blk_idf, float(n_sel)), axis=0, keepdims=True)
        left = jnp.where(blk_idf == first, -jnp.inf, left)
        while len(o_win) * n_steps < (step + 1) * R:
            o_win.append(softmax_av(s_w[len(o_win)], bias_w, vw_ref[pl.ds(ws, wk), :]))
    sel_bias = jnp.where(left != work, 0.0, MASK_FILL).T.astype(BF16)

    gates = jax.nn.sigmoid(gate_ref[...])

    def gate(r, branch):
        return gates[:, r * N_GATES + branch:r * N_GATES + branch + 1]

    o_ref[...] = jnp.concatenate([gate(r, 0) * o_cmp[r] + gate(r, 2) * o_win[r] for r in range(R)], axis=1)

    tok = lax.broadcasted_iota(jnp.int32, (n_sel, tk), 1)
    sel_row = lax.broadcasted_iota(jnp.int32, (n_sel, tk), 0)
    key_off = lax.broadcasted_iota(jnp.int32, (1, tk), 1)

    def key_rows(kb):
        return pl.ds(pl.multiple_of(kb * tk, tk), tk)

    def block_bias(kb):
        ks = kb * tk
        expand = jnp.where(((ks + tok) >> SEL_SHIFT) == sel_row, 1.0, 0.0).astype(BF16)
        bias = _dot(sel_bias, expand)
        return jnp.where(ks + key_off <= q_pos, bias, MASK_FILL)

    def put_scores(r, raw, bias):
        sc = raw + bias
        s_ref[rows[r], :] = sc
        mb_ref[rows[r], :] = jnp.max(sc, axis=-1, keepdims=True)

    def take_probs(r):
        m_old = m_ref[rows[r], :]
        m_new = jnp.maximum(m_old, mb_ref[rows[r], :])
        m_ref[rows[r], :] = m_new
        return jnp.exp(s_ref[rows[r], :] - m_new).astype(BF16), jnp.exp(m_old - m_new)

    def accumulate(r, p, alpha, v1):
        acc_ref[rows[r], :] = alpha * acc_ref[rows[r], :] + _dot(p, v1)

    m_ref[...] = jnp.full_like(m_ref, MASK_FILL)
    acc_ref[...] = jnp.zeros_like(acc_ref)
    kb_diag = qi // (tk // blk)
    bias = block_bias(0)
    for r in range(R):
        put_scores(r, _dot_nt(qs[rows[r]], ks_ref[key_rows(0), :]), bias)

    def sel_body(kb, carry):
        k_next = ks_ref[key_rows(kb + 1), :]
        bias_next = block_bias(kb + 1)
        v1 = vs_ref[key_rows(kb), :]
        for r in range(R):
            p, alpha = take_probs(r)
            put_scores(r, _dot_nt(qs[rows[r]], k_next), bias_next)
            accumulate(r, p, alpha, v1)
        return carry

    lax.fori_loop(0, kb_diag, sel_body, 0)
    v1 = vs_ref[key_rows(kb_diag), :]
    for r in range(R):
        p, alpha = take_probs(r)
        accumulate(r, p, alpha, v1)
    o_sel = acc_ref[:, :HEAD_DIM] / acc_ref[:, HEAD_DIM:]
    o_ref[...] += jnp.concatenate([gate(r, 1) * o_sel[rows[r]] for r in range(R)], axis=1)


def _nsa_attention(q_tok, v_aug, k_cmp, v_cmp, gates, ks_col, kw_col, vs_col, vw_col, gate_col,
                   blk=256, tk=1024):
    T = q_tok.shape[0]
    G = NSA_KV_GROUPS
    blk = min(blk, T)
    tk = min(tk, T)
    assert T % tk == 0 and tk % blk == 0 and WINDOW % blk == 0 and WINDOW + blk <= T
    n_cmp = T // CMP_STRIDE
    n_sel = T // SEL_BLOCK
    cmp_start = np.arange(n_cmp)[None, :] * CMP_STRIDE
    sel_start = np.arange(n_sel)[:, None] * SEL_BLOCK
    overlap = np.clip(np.minimum(cmp_start + CMP_BLOCK, sel_start + SEL_BLOCK)
                      - np.maximum(cmp_start, sel_start), 0, None)
    c2s_t = jnp.asarray(overlap.astype(np.float32) / CMP_BLOCK, dtype=BF16)
    width = NSA_REP * HEAD_DIM
    key_spec = lambda col: pl.BlockSpec((T, HEAD_DIM), lambda g, i: (0, col + g))
    val_spec = lambda col: pl.BlockSpec((T, 2 * HEAD_DIM), lambda g, i: (0, col + g))
    cmp_spec = lambda w: pl.BlockSpec((1, n_cmp, w), lambda g, i: (g, 0, 0))
    return pl.pallas_call(
        functools.partial(_nsa_kernel, blk=blk, tk=tk),
        out_shape=jax.ShapeDtypeStruct((T, G * width), F32),
        grid=(G, T // blk),
        in_specs=[pl.BlockSpec((blk, width), lambda g, i: (i, g)),
                  cmp_spec(HEAD_DIM), cmp_spec(2 * HEAD_DIM),
                  key_spec(ks_col), val_spec(vs_col), key_spec(kw_col), val_spec(vw_col),
                  pl.BlockSpec((blk, LANES), lambda g, i: (i, gate_col + g)),
                  pl.BlockSpec((n_sel, n_cmp), lambda g, i: (0, 0))],
        out_specs=pl.BlockSpec((blk, width), lambda g, i: (i, g)),
        scratch_shapes=[pltpu.VMEM((NSA_REP * blk, 1), F32),
                        pltpu.VMEM((NSA_REP * blk, 1), F32),
                        pltpu.VMEM((NSA_REP * blk, 2 * HEAD_DIM), F32),
                        pltpu.VMEM((NSA_REP * blk, tk), F32)],
        compiler_params=_params(("parallel", "arbitrary")),
        name="nsa_attention",
    )(q_tok, k_cmp, v_cmp, q_tok, v_aug, q_tok, v_aug, gates, c2s_t)


def _cross_kernel(q_ref, k_ref, v_ref, o_ref):
    s = _dot_nt(q_ref[...], k_ref[...])
    p = jnp.exp(s - jnp.max(s, axis=-1, keepdims=True))
    p = p / jnp.sum(p, axis=-1, keepdims=True)
    o_ref[...] = _dot(p.astype(BF16), v_ref[...]).astype(o_ref.dtype)


def _cross_attention(q, kv, n_heads, tm=1024):
    T, D = q.shape
    M = kv.shape[0]
    dh = D // n_heads
    tm = min(tm, T)
    return pl.pallas_call(
        _cross_kernel,
        out_shape=jax.ShapeDtypeStruct((T, D), BF16),
        grid=(T // tm, n_heads),
        in_specs=[pl.BlockSpec((tm, dh), lambda i, h: (i, h)),
                  pl.BlockSpec((M, dh), lambda i, h: (0, h)),
                  pl.BlockSpec((M, dh), lambda i, h: (0, n_heads + h))],
        out_specs=pl.BlockSpec((tm, dh), lambda i, h: (i, h)),
        compiler_params=_params(("parallel", "arbitrary")),
        name="cross_attention",
    )(q, kv, kv)


def _rope_tables(T):
    half = HEAD_DIM // 2
    inv_freq = ROPE_THETA ** (-np.arange(half, dtype=np.float64) / half)
    ang = np.arange(T, dtype=np.float64)[:, None] * inv_freq[None, :]
    cos, sin = np.cos(ang).astype(np.float32), np.sin(ang).astype(np.float32)
    return jnp.asarray(np.concatenate([cos, cos], axis=1)), jnp.asarray(np.concatenate([-sin, sin], axis=1))


def _pad_to(w, axis, mult):
    pad = -w.shape[axis] % mult
    if not pad:
        return w
    shape = list(w.shape)
    shape[axis] = pad
    return jnp.concatenate([w, jnp.zeros(shape, w.dtype)], axis=axis)


def _ffn(x_f32, x_bf16, w_gate, w_up, w_down, ln_g, ln_b, alpha, emit_bf16):
    d_ff = w_gate.shape[1]
    main = d_ff // FF_TILE * FF_TILE
    assert (d_ff - main) % LANES == 0 and main % (FF_DOWN_STEPS * LANES) == 0
    h = _swiglu_up(x_bf16, w_gate, w_up, FF_TILE, main // FF_TILE)
    wd = w_down.astype(BF16)
    tail = None
    if main < d_ff:
        tail = (_swiglu_up(x_bf16, w_gate[:, main:], w_up[:, main:], d_ff - main, 1), wd[main:])
    return _mm_res_ln(h, wd, x_f32, ln_g, ln_b, alpha, 0.5, emit_bf16, n_k=FF_DOWN_STEPS, tail=tail)


def _mixer(x_f32, x_bf16, w_in, cmp_pos_k, cmp_w1_k, cmp_w2_k, cmp_pos_v, cmp_w1_v, cmp_w2_v,
           mix_norm_g, w_out, ln_g, ln_b, alpha):
    T = x_f32.shape[0]
    G = NSA_KV_GROUPS
    sbw = SB_HEADS * HEAD_DIM
    nqw = NSA_HEADS * HEAD_DIM
    kvw = G * HEAD_DIM
    bounds = np.cumsum([0, sbw, sbw, sbw, nqw, kvw, kvw, kvw, kvw, kvw, kvw, NSA_HEADS * N_GATES])
    (w_sbq, w_sbk, w_sbv, w_nq, w_kc, w_vc, w_ks, w_vs, w_kw, w_vw, w_gate) = [
        w_in[:, bounds[i]:bounds[i + 1]] for i in range(11)]
    per_group = NSA_REP * N_GATES
    w_gate = jnp.concatenate([_pad_to(w_gate[:, g * per_group:(g + 1) * per_group], 1, LANES)
                              for g in range(G)], axis=1)
    tables = _rope_tables(T)
    cat = lambda ws: jnp.concatenate(ws, axis=1).astype(BF16)
    q_scale = lambda n_q, n_rest: jnp.concatenate([jnp.full((n_q,), HEAD_DIM ** -0.5, F32), jnp.ones((n_rest,), F32)])
    plain = _project(x_bf16, cat([w_sbq, w_sbk, w_sbv, w_vs, w_vw]), BF16, 1024, (3 * sbw + 2 * kvw) // 2,
                     col_scale=q_scale(sbw, 2 * sbw + 2 * kvw))
    roped = _project(x_bf16, cat([w_nq, w_ks, w_kw]), BF16, 1024, nqw + 2 * kvw,
                     col_scale=q_scale(nqw, 2 * kvw), rope_tables=tables)
    kc = _project(x_bf16, w_kc.astype(BF16), F32, 1024, kvw, rope_tables=tables)
    vc_gate = _project(x_bf16, cat([w_vc, w_gate]), F32, 1024, 2 * kvw)

    o_sb = _sb_attention(plain, SB_HEADS, 0, SB_HEADS, 2 * SB_HEADS)

    head = lambda a, g: a[:, g * HEAD_DIM:(g + 1) * HEAD_DIM]
    streams = jnp.stack([head(kc, g) for g in range(G)] + [head(vc_gate, g) for g in range(G)])
    rep = lambda a, b: jnp.stack([a] * G + [b] * G)
    cmp = _compress(streams, rep(cmp_pos_k, cmp_pos_v), rep(cmp_w1_k, cmp_w1_v), rep(cmp_w2_k, cmp_w2_v))
    ones = jnp.ones((T, HEAD_DIM), BF16)
    v_aug = jnp.concatenate([a for h in range(2 * G) for a in (head(plain, 3 * SB_HEADS + h), ones)], axis=1)
    v_cmp_aug = jnp.concatenate([cmp[G:], jnp.ones_like(cmp[G:])], axis=2)
    o_nsa = _nsa_attention(roped, v_aug, cmp[:G], v_cmp_aug, vc_gate,
                           ks_col=NSA_HEADS, kw_col=NSA_HEADS + G, vs_col=0, vw_col=G, gate_col=G)
    return _rms_mm_res_ln(o_sb, o_nsa, mix_norm_g, w_out.astype(BF16), x_f32, ln_g, ln_b, alpha)


def _memory_block(x_f32, x_bf16, mem, w_q, w_k, w_v, w_o, ln_g, ln_b, alpha):
    D = w_q.shape[1]
    q = _project(x_bf16, w_q.astype(BF16), BF16, 1024, D,
                 col_scale=jnp.full((D,), (D // MEM_HEADS) ** -0.5, F32))
    kv = _project(mem.astype(BF16), jnp.concatenate([w_k, w_v], axis=1).astype(BF16), BF16, 256, 1024)
    o = _cross_attention(q, kv, MEM_HEADS)
    return _mm_res_ln(o, w_o.astype(BF16), x_f32, ln_g, ln_b, alpha, 1.0, True)


def kernel(x, mem, ln1_g, ln1_b, ffn1_gate, ffn1_up, ffn1_down, w_in, cmp_pos_k, cmp_w1_k, cmp_w2_k, cmp_pos_v, cmp_w1_v, cmp_w2_v, mix_norm_g, w_out, ln2_g, ln2_b, mem_wq, mem_wk, mem_wv, mem_wo, ln3_g, ln3_b, ffn2_gate, ffn2_up, ffn2_down, ln4_g, ln4_b):
    n_layers = ffn1_gate.shape[0]
    alpha = (2 * n_layers) ** 0.25
    outs = []
    for bi in range(x.shape[0]):
        xf = x[bi]
        xb = xf.astype(BF16)
        for l in range(n_layers):
            xf, xb = _ffn(xf, xb, ffn1_gate[l], ffn1_up[l], ffn1_down[l], ln1_g[l], ln1_b[l], alpha, True)
            xf, xb = _mixer(xf, xb, w_in[l], cmp_pos_k[l], cmp_w1_k[l], cmp_w2_k[l], cmp_pos_v[l], cmp_w1_v[l],
                            cmp_w2_v[l], mix_norm_g[l], w_out[l], ln2_g[l], ln2_b[l], alpha)
            xf, xb = _memory_block(xf, xb, mem[bi], mem_wq[l], mem_wk[l], mem_wv[l], mem_wo[l], ln3_g[l], ln3_b[l],
                                   alpha)
            xf, xb = _ffn(xf, xb, ffn2_gate[l], ffn2_up[l], ffn2_down[l], ln4_g[l], ln4_b[l], alpha,
                          l + 1 < n_layers)
        outs.append(xf)
    return outs[0][None] if len(outs) == 1 else jnp.stack(outs)
```

```python
import functools

import numpy as np
import jax
import jax.numpy as jnp
from jax import lax
from jax.experimental import pallas as pl
from jax.experimental.pallas import tpu as pltpu

HEAD_DIM = 128
SB_HEADS = 8
NSA_HEADS = 8
NSA_KV_GROUPS = 2
NSA_REP = NSA_HEADS // NSA_KV_GROUPS
N_GATES = 3
CMP_BLOCK = 32
CMP_STRIDE = 16
SEL_BLOCK = 64
SEL_SHIFT = 6
SEL_TOPK = 16
WINDOW = 512
MEM_HEADS = 4
ROPE_THETA = 10000.0
LN_EPS = 1e-5
RMS_EPS = 1e-6
MASK_FILL = -1e30
LOG2E = 1.4426950408889634
SB_UNDERFLOW = -110.0
assert 1 << SEL_SHIFT == SEL_BLOCK

LANES = 128
SUBLANES = 8
FF_TILE = 512
FF_DOWN_STEPS = 4
VMEM_LIMIT = 56 * 1024 * 1024

BF16 = jnp.bfloat16
F32 = jnp.float32


def _params(sem):
    return pltpu.CompilerParams(dimension_semantics=sem, vmem_limit_bytes=VMEM_LIMIT)


def _dot(a, b):
    return jnp.dot(a, b, preferred_element_type=F32)


def _dot_nt(a, b):
    return lax.dot_general(a, b, (((1,), (1,)), ((), ())), preferred_element_type=F32)


def _split_dot(a, b):
    hi = a.astype(BF16)
    lo = (a - hi.astype(F32)).astype(BF16)
    return _dot(hi, b) + _dot(lo, b)


def _layer_norm(z, g, b):
    mu = jnp.mean(z, axis=-1, keepdims=True)
    zc = z - mu
    var = jnp.mean(zc * zc, axis=-1, keepdims=True)
    return zc * lax.rsqrt(var + LN_EPS) * g + b


def _proj_kernel(x_ref, w_ref, *rest, scaled, rope):
    rest = list(rest)
    o_ref = rest.pop()
    y = _dot(x_ref[...], w_ref[...])
    if scaled:
        y = y * rest.pop(0)[...]
    if rope:
        cos_ref, sin_ref = rest
        c = cos_ref[...]
        s = sin_ref[...]
        heads = []
        for h in range(y.shape[1] // HEAD_DIM):
            yh = y[:, h * HEAD_DIM:(h + 1) * HEAD_DIM]
            heads.append(yh * c + pltpu.roll(yh, HEAD_DIM // 2, 1) * s)
        y = jnp.concatenate(heads, axis=1) if len(heads) > 1 else heads[0]
    o_ref[...] = y.astype(o_ref.dtype)


def _project(x, w, out_dtype, tm, tn, col_scale=None, rope_tables=None):
    M, K = x.shape
    N = w.shape[1]
    tm = min(tm, M)
    assert M % tm == 0 and N % tn == 0
    in_specs = [pl.BlockSpec((tm, K), lambda i, j: (i, 0)),
                pl.BlockSpec((K, tn), lambda i, j: (0, j))]
    args = [x, w]
    if col_scale is not None:
        in_specs.append(pl.BlockSpec((1, tn), lambda i, j: (0, j)))
        args.append(col_scale.reshape(1, N))
    if rope_tables is not None:
        in_specs += [pl.BlockSpec((tm, HEAD_DIM), lambda i, j: (i, 0))] * 2
        args += list(rope_tables)
    return pl.pallas_call(
        functools.partial(_proj_kernel, scaled=col_scale is not None, rope=rope_tables is not None),
        out_shape=jax.ShapeDtypeStruct((M, N), out_dtype),
        grid=(M // tm, N // tn),
        in_specs=in_specs,
        out_specs=pl.BlockSpec((tm, tn), lambda i, j: (i, j)),
        compiler_params=_params(("parallel", "arbitrary")),
        name="proj_rope" if rope_tables is not None else "proj",
    )(*args)


def _swiglu_up_kernel(x_ref, wg_ref, wu_ref, o_ref, wg_bf16, wu_bf16):
    @pl.when(pl.program_id(1) == 0)
    def _():
        wg_bf16[...] = wg_ref[...].astype(BF16)
        wu_bf16[...] = wu_ref[...].astype(BF16)

    x = x_ref[...]
    g = _dot(x, wg_bf16[...])
    u = _dot(x, wu_bf16[...])
    o_ref[...] = (jax.nn.silu(g) * u).astype(o_ref.dtype)


def _swiglu_up(x, wg, wu, tn, n_blocks, tm=1024):
    M, K = x.shape
    tm = min(tm, M)
    assert M % tm == 0 and n_blocks * tn <= wg.shape[1] and wg.shape == wu.shape
    return pl.pallas_call(
        _swiglu_up_kernel,
        out_shape=jax.ShapeDtypeStruct((M, n_blocks * tn), BF16),
        grid=(n_blocks, M // tm),
        in_specs=[pl.BlockSpec((tm, K), lambda j, i: (i, 0)),
                  pl.BlockSpec((K, tn), lambda j, i: (0, j)),
                  pl.BlockSpec((K, tn), lambda j, i: (0, j))],
        out_specs=pl.BlockSpec((tm, tn), lambda j, i: (i, j)),
        scratch_shapes=[pltpu.VMEM((K, tn), BF16), pltpu.VMEM((K, tn), BF16)],
        compiler_params=_params(("parallel", "arbitrary")),
        name="swiglu_up",
    )(x, wg, wu)


def _row_chunks(tm, n_split):
    return [slice(c * tm // n_split, (c + 1) * tm // n_split) for c in range(n_split)]


def _res_ln_store(chunks, totals, res_ref, g_ref, b_ref, of_ref, ob_ref, alpha, coef):
    for c, y in zip(chunks, totals):
        out = _layer_norm(alpha * res_ref[c, :] + coef * y, g_ref[...], b_ref[...])
        of_ref[c, :] = out
        if ob_ref is not None:
            ob_ref[c, :] = out.astype(BF16)


def _mm_res_ln_kernel(*refs, alpha, coef, emit_bf16, n_k, n_split, has_tail):
    refs = list(refs)
    h_ref, w_ref = refs.pop(0), refs.pop(0)
    ht_ref, wt_ref = (refs.pop(0), refs.pop(0)) if has_tail else (None, None)
    res_ref, g_ref, b_ref, of_ref = refs.pop(0), refs.pop(0), refs.pop(0), refs.pop(0)
    ob_ref = refs.pop(0) if emit_bf16 else None
    acc_ref = refs.pop(0) if n_k > 1 else None
    chunks = _row_chunks(h_ref.shape[0], n_split)

    def partial_products():
        return [_dot(h_ref[c, :], w_ref[...]) for c in chunks]

    def finish(totals):
        if has_tail:
            totals = [y + _dot(ht_ref[c, :], wt_ref[...]) for c, y in zip(chunks, totals)]
        _res_ln_store(chunks, totals, res_ref, g_ref, b_ref, of_ref, ob_ref, alpha, coef)

    if n_k == 1:
        finish(partial_products())
        return
    k = pl.program_id(1)

    @pl.when(k == 0)
    def _():
        for c, y in zip(chunks, partial_products()):
            acc_ref[c, :] = y

    @pl.when((k > 0) & (k < n_k - 1))
    def _():
        for c, y in zip(chunks, partial_products()):
            acc_ref[c, :] += y

    @pl.when(k == n_k - 1)
    def _():
        finish([acc_ref[c, :] + y for c, y in zip(chunks, partial_products())])


def _mm_res_ln(h, w, res, g, b, alpha, coef, emit_bf16, tm=512, n_k=1, n_split=2, tail=None):
    M, K = h.shape
    N = w.shape[1]
    tm = min(tm, M)
    tk = K // n_k
    assert M % tm == 0 and K % n_k == 0 and K <= w.shape[0] and (n_k == 1 or tk % LANES == 0)
    row_block = pl.BlockSpec((tm, N), lambda i, k: (i, 0))
    vec = pl.BlockSpec((1, N), lambda i, k: (0, 0))
    in_specs = [pl.BlockSpec((tm, tk), lambda i, k: (i, k)), pl.BlockSpec((tk, N), lambda i, k: (k, 0))]
    args = [h, w]
    if tail is not None:
        kt = tail[0].shape[1]
        in_specs += [pl.BlockSpec((tm, kt), lambda i, k: (i, 0)), pl.BlockSpec((kt, N), lambda i, k: (0, 0))]
        args += list(tail)
    out = pl.pallas_call(
        functools.partial(_mm_res_ln_kernel, alpha=alpha, coef=coef, emit_bf16=emit_bf16,
                          n_k=n_k, n_split=n_split, has_tail=tail is not None),
        out_shape=[jax.ShapeDtypeStruct((M, N), F32)] + [jax.ShapeDtypeStruct((M, N), BF16)] * emit_bf16,
        grid=(M // tm, n_k),
        in_specs=in_specs + [row_block, vec, vec],
        out_specs=[row_block] * (1 + emit_bf16),
        scratch_shapes=[pltpu.VMEM((tm, N), F32)] if n_k > 1 else [],
        compiler_params=_params(("parallel", "arbitrary")),
        name="mm_res_ln",
    )(*args, res, g.reshape(1, N), b.reshape(1, N))
    return (out[0], out[1]) if emit_bf16 else (out[0], None)


def _rms_mm_res_ln_kernel(a_ref, b_ref, gain_ref, w_ref, res_ref, g_ref, beta_ref, of_ref, ob_ref, *,
                          alpha, n_split):
    chunks = _row_chunks(a_ref.shape[0], n_split)
    wa = a_ref.shape[1]
    gain = gain_ref[...]

    def rms(o, gn):
        return (o * lax.rsqrt(jnp.mean(o * o, axis=-1, keepdims=True) + RMS_EPS) * gn).astype(BF16)

    totals = [_dot(jnp.concatenate([rms(a_ref[c, :], gain[:, :wa]), rms(b_ref[c, :], gain[:, wa:])], axis=1),
                   w_ref[...]) for c in chunks]
    _res_ln_store(chunks, totals, res_ref, g_ref, beta_ref, of_ref, ob_ref, alpha, 1.0)


def _rms_mm_res_ln(o_a, o_b, gain, w, res, g, b, alpha, tm=512, n_split=2):
    M, wa = o_a.shape
    wb = o_b.shape[1]
    N = w.shape[1]
    tm = min(tm, M)
    assert M % tm == 0 and w.shape[0] == wa + wb
    row_block = pl.BlockSpec((tm, N), lambda i: (i, 0))
    vec = pl.BlockSpec((1, N), lambda i: (0, 0))
    return pl.pallas_call(
        functools.partial(_rms_mm_res_ln_kernel, alpha=alpha, n_split=n_split),
        out_shape=[jax.ShapeDtypeStruct((M, N), F32), jax.ShapeDtypeStruct((M, N), BF16)],
        grid=(M // tm,),
        in_specs=[pl.BlockSpec((tm, wa), lambda i: (i, 0)),
                  pl.BlockSpec((tm, wb), lambda i: (i, 0)),
                  pl.BlockSpec((1, wa + wb), lambda i: (0, 0)),
                  pl.BlockSpec((wa + wb, N), lambda i: (0, 0)),
                  row_block, vec, vec],
        out_specs=[row_block, row_block],
        compiler_params=_params(("parallel",)),
        name="rms_mm_res_ln",
    )(o_a, o_b, gain.reshape(1, wa + wb), w, res, g.reshape(1, N), b.reshape(1, N))


def _sb_kernel(q_ref, k_ref, v_ref, o_ref, acc_ref, c_ref, *, blk, heads):
    qi = pl.program_id(1)
    row = lax.broadcasted_iota(jnp.int32, (blk, blk), 0)
    col = lax.broadcasted_iota(jnp.int32, (blk, blk), 1)
    suffix = jnp.where(row > col, 1.0, 0.0).astype(BF16)
    strict = col < row

    cols = [slice(h * HEAD_DIM, (h + 1) * HEAD_DIM) for h in range(heads)]

    def step(kb_far, n_blocks, diagonal):
        ks = pl.multiple_of(kb_far * blk, blk)
        key_rows = [pl.ds(ks + b * blk, blk) for b in range(n_blocks)]
        log_beta, log_1m = {}, {}
        for p in [(b, h) for b in reversed(range(n_blocks)) for h in range(heads)]:
            y = _dot_nt(q_ref[:, cols[p[1]]], k_ref[key_rows[p[0]], cols[p[1]]])
            lb = jnp.minimum(y, 0.0) - jnp.log(1.0 + jnp.exp2(jnp.abs(y) * -LOG2E))
            l1 = lb - y
            if diagonal and p[0] == n_blocks - 1:
                l1 = jnp.where(strict, l1, 0.0)
            log_beta[p] = lb
            log_1m[p] = l1.astype(BF16)
        weights = {}
        for h in range(heads):
            c = c_ref[h]
            for b in reversed(range(n_blocks)):
                p = (b, h)
                log_stay = _dot(log_1m[p], suffix)
                w = jnp.exp(log_beta[p] + (log_stay + c))
                if diagonal and b == n_blocks - 1:
                    w = jnp.where(strict, w, 0.0)
                weights[p] = w.astype(BF16)
                c = c + (log_stay[:, :1] + log_1m[p][:, :1].astype(F32))
            c_ref[h] = c
        for h in range(heads):
            w = jnp.concatenate([weights[(b, h)] for b in range(n_blocks)], axis=1)
            acc_ref[:, cols[h]] += _dot(w, v_ref[pl.ds(ks, n_blocks * blk), cols[h]])

    acc_ref[...] = jnp.zeros_like(acc_ref)
    c_ref[...] = jnp.zeros_like(c_ref)

    @pl.when(qi == 0)
    def _():
        step(0, 1, True)

    @pl.when(qi > 0)
    def _():
        step(qi - 1, 2, True)

    def more(carry):
        done, c_max = carry
        return (done < qi) & (c_max >= SB_UNDERFLOW)

    def body(carry):
        done, _ = carry
        step(qi - 1 - done, 1, False)
        return done + 1, jnp.max(c_ref[...])

    lax.while_loop(more, body, (jnp.minimum(qi, 1), jnp.max(c_ref[...])))
    o_ref[...] = acc_ref[...]


def _sb_attention(qkv, n_heads, q_col, k_col, v_col, blk=256, heads=4):
    T = qkv.shape[0]
    blk = min(blk, T)
    width = heads * HEAD_DIM
    assert T % blk == 0 and n_heads % heads == 0
    assert q_col % heads == 0 and k_col % heads == 0 and v_col % heads == 0
    return pl.pallas_call(
        functools.partial(_sb_kernel, blk=blk, heads=heads),
        out_shape=jax.ShapeDtypeStruct((T, n_heads * HEAD_DIM), F32),
        grid=(n_heads // heads, T // blk),
        in_specs=[pl.BlockSpec((blk, width), lambda h, i: (i, q_col // heads + h)),
                  pl.BlockSpec((T, width), lambda h, i: (0, k_col // heads + h)),
                  pl.BlockSpec((T, width), lambda h, i: (0, v_col // heads + h))],
        out_specs=pl.BlockSpec((blk, width), lambda h, i: (i, h)),
        scratch_shapes=[pltpu.VMEM((blk, width), F32), pltpu.VMEM((heads, blk, 1), F32)],
        compiler_params=_params(("parallel", "arbitrary")),
        name="sb_attention",
    )(qkv, qkv, qkv)


def _compress_kernel(x_ref, pos_ref, w1_ref, w2_ref, o_ref):
    n = x_ref.shape[1]
    half = CMP_STRIDE * HEAD_DIM
    x = x_ref[0].astype(BF16)
    w1 = w1_ref[0].astype(BF16)
    a = _dot(x, w1[:half])
    b = _dot(x, w1[half:])
    pos = _dot(pos_ref[0].astype(BF16), w1)
    b_next = pltpu.roll(b, n - 1, 0)
    hid = jax.nn.gelu(a + b_next + pos[0:1])
    out = _dot(hid.astype(BF16), w2_ref[0].astype(BF16))
    valid = lax.broadcasted_iota(jnp.int32, out.shape, 0) < n - 1
    o_ref[0] = jnp.where(valid, out, 0.0).astype(o_ref.dtype)


def _compress(x_tok, pos_emb, w1, w2):
    S, T, _ = x_tok.shape
    n = T // CMP_STRIDE
    width = CMP_STRIDE * HEAD_DIM
    hidden = w1.shape[-1]
    x2 = x_tok.reshape(S, n, width)
    pos_flat = jnp.broadcast_to(pos_emb.reshape(S, 1, 2 * width), (S, SUBLANES, 2 * width))
    return pl.pallas_call(
        _compress_kernel,
        out_shape=jax.ShapeDtypeStruct((S, n, HEAD_DIM), BF16),
        grid=(S,),
        in_specs=[pl.BlockSpec((1, n, width), lambda s: (s, 0, 0)),
                  pl.BlockSpec((1, SUBLANES, 2 * width), lambda s: (s, 0, 0)),
                  pl.BlockSpec((1, 2 * width, hidden), lambda s: (s, 0, 0)),
                  pl.BlockSpec((1, hidden, HEAD_DIM), lambda s: (s, 0, 0))],
        out_specs=pl.BlockSpec((1, n, HEAD_DIM), lambda s: (s, 0, 0)),
        compiler_params=_params(("parallel",)),
        name="nsa_compress",
    )(x2, pos_flat, w1, w2)


def _nsa_kernel(q_ref, kc_ref, vc_ref, ks_ref, vs_ref, kw_ref, vw_ref, gate_ref, c2s_ref, o_ref,
                m_ref, mb_ref, acc_ref, s_ref, *, blk, tk):
    R = NSA_REP
    qi = pl.program_id(1)
    q0 = qi * blk
    n_cmp = kc_ref.shape[1]
    n_sel = c2s_ref.shape[1]
    rows = [slice(r * blk, (r + 1) * blk) for r in range(R)]
    qs = jnp.concatenate([q_ref[:, r * HEAD_DIM:(r + 1) * HEAD_DIM] for r in range(R)], axis=0)
    q_pos = q0 + lax.broadcasted_iota(jnp.int32, (blk, 1), 0)
    q_lane = q0 + lax.broadcasted_iota(jnp.int32, (1, blk), 1)

    def softmax_av(s, bias, v1):
        sc = s + bias
        p = jnp.exp(sc - jnp.max(sc, axis=-1, keepdims=True)).astype(BF16)
        pv = _dot(p, v1)
        return pv[:, :HEAD_DIM] / pv[:, HEAD_DIM:]

    kc = kc_ref[0]
    s_cmp = [_dot_nt(qs[rows[r]], kc) for r in range(R)]
    cmp_end = lax.broadcasted_iota(jnp.int32, (1, n_cmp), 1) * CMP_STRIDE + (CMP_BLOCK - 1)
    bias_c = jnp.where(cmp_end <= q_pos, 0.0, MASK_FILL)
    o_cmp, p_sum = [], None
    for r in range(R):
        sc = s_cmp[r] + bias_c
        p = jnp.exp(sc - jnp.max(sc, axis=-1, keepdims=True))
        inv_sum = 1.0 / jnp.sum(p, axis=-1, keepdims=True)
        o_cmp.append(jnp.where(q_pos >= CMP_BLOCK - 1, _dot(p.astype(BF16), vc_ref[0]) * inv_sum, 0.0))
        p_sum = p * inv_sum if p_sum is None else p_sum + p * inv_sum
    imp = _split_dot(p_sum, c2s_ref[...])

    wk = WINDOW + blk
    ws = pl.multiple_of(jnp.maximum(q0 - WINDOW, 0), blk)
    dist = q_pos - (ws + lax.broadcasted_iota(jnp.int32, (1, wk), 1))
    bias_w = jnp.where((dist >= 0) & (dist < WINDOW), 0.0, MASK_FILL)
    k_w = kw_ref[pl.ds(ws, wk), :]
    s_w = [_dot_nt(qs[rows[r]], k_w) for r in range(R)]

    blk_id = lax.broadcasted_iota(jnp.int32, (n_sel, 1), 0)
    forced = (blk_id == 0) | (blk_id == (q_lane >> SEL_SHIFT))
    valid = blk_id * SEL_BLOCK <= q_lane
    work = jnp.where(forced, jnp.inf, jnp.where(valid, imp.T, -jnp.inf))
    blk_idf = blk_id.astype(F32)
    left = work
    o_win = []
    n_steps = min(SEL_TOPK, n_sel)
    for step in range(n_steps):
        top = jnp.max(left, axis=0, keepdims=True)
        first = jnp.min(jnp.where(left == top, blk_idf, float(n_sel)), axis=0, keepdims=True)
        left = jnp.where(blk_idf == first, -jnp.inf, left)
        while len(o_win) * n_steps < (step + 1) * R:
            o_win.append(softmax_av(s_w[len(o_win)], bias_w, vw_ref[pl.ds(ws, wk), :]))
    sel_bias = jnp.where(left != work, 0.0, MASK_FILL).T.astype(BF16)

    gates = jax.nn.sigmoid(gate_ref[...])

    def gate(r, branch):
        return gates[:, r * N_GATES + branch:r * N_GATES + branch + 1]

    o_ref[...] = jnp.concatenate([gate(r, 0) * o_cmp[r] + gate(r, 2) * o_win[r] for r in range(R)], axis=1)

    tok = lax.broadcasted_iota(jnp.int32, (n_sel, tk), 1)
    sel_row = lax.broadcasted_iota(jnp.int32, (n_sel, tk), 0)
    key_off = lax.broadcasted_iota(jnp.int32, (1, tk), 1)

    def key_rows(kb):
        return pl.ds(pl.multiple_of(kb * tk, tk), tk)

    def block_bias(kb):
        ks = kb * tk
        expand = jnp.where(((ks + tok) >> SEL_SHIFT) == sel_row, 1.0, 0.0).astype(BF16)
        bias = _dot(sel_bias, expand)
        return jnp.where(ks + key_off <= q_pos, bias, MASK_FILL)

    def put_scores(r, raw, bias):
        sc = raw + bias
        s_ref[rows[r], :] = sc
        mb_ref[rows[r], :] = jnp.max(sc, axis=-1, keepdims=True)

    def take_probs(r):
        m_old = m_ref[rows[r], :]
        m_new = jnp.maximum(m_old, mb_ref[rows[r], :])
        m_ref[rows[r], :] = m_new
        return jnp.exp(s_ref[rows[r], :] - m_new).astype(BF16), jnp.exp(m_old - m_new)

    def accumulate(r, p, alpha, v1):
        acc_ref[rows[r], :] = alpha * acc_ref[rows[r], :] + _dot(p, v1)

    m_ref[...] = jnp.full_like(m_ref, MASK_FILL)
    acc_ref[...] = jnp.zeros_like(acc_ref)
    kb_diag = qi // (tk // blk)
    bias = block_bias(0)
    for r in range(R):
        put_scores(r, _dot_nt(qs[rows[r]], ks_ref[key_rows(0), :]), bias)

    def sel_body(kb, carry):
        k_next = ks_ref[key_rows(kb + 1), :]
        bias_next = block_bias(kb + 1)
        v1 = vs_ref[key_rows(kb), :]
        for r in range(R):
            p, alpha = take_probs(r)
            put_scores(r, _dot_nt(qs[rows[r]], k_next), bias_next)
            accumulate(r, p, alpha, v1)
        return carry

    lax.fori_loop(0, kb_diag, sel_body, 0)
    v1 = vs_ref[key_rows(kb_diag), :]
    for r in range(R):
        p, alpha = take_probs(r)
        accumulate(r, p, alpha, v1)
    o_sel = acc_ref[:, :HEAD_DIM] / acc_ref[:, HEAD_DIM:]
    o_ref[...] += jnp.concatenate([gate(r, 1) * o_sel[rows[r]] for r in range(R)], axis=1)


def _nsa_attention(q_tok, v_aug, k_cmp, v_cmp, gates, ks_col, kw_col, vs_col, vw_col, gate_col,
                   blk=256, tk=1024):
    T = q_tok.shape[0]
    G = NSA_KV_GROUPS
    blk = min(blk, T)
    tk = min(tk, T)
    assert T % tk == 0 and tk % blk == 0 and WINDOW % blk == 0 and WINDOW + blk <= T
    n_cmp = T // CMP_STRIDE
    n_sel = T // SEL_BLOCK
    cmp_start = np.arange(n_cmp)[:, None] * CMP_STRIDE
    sel_start = np.arange(n_sel)[None, :] * SEL_BLOCK
    overlap = np.clip(np.minimum(cmp_start + CMP_BLOCK, sel_start + SEL_BLOCK)
                      - np.maximum(cmp_start, sel_start), 0, None)
    c2s = jnp.asarray(overlap.astype(np.float32) / CMP_BLOCK, dtype=BF16)
    width = NSA_REP * HEAD_DIM
    key_spec = lambda col: pl.BlockSpec((T, HEAD_DIM), lambda g, i: (0, col + g))
    val_spec = lambda col: pl.BlockSpec((T, 2 * HEAD_DIM), lambda g, i: (0, col + g))
    cmp_spec = lambda w: pl.BlockSpec((1, n_cmp, w), lambda g, i: (g, 0, 0))
    return pl.pallas_call(
        functools.partial(_nsa_kernel, blk=blk, tk=tk),
        out_shape=jax.ShapeDtypeStruct((T, G * width), F32),
        grid=(G, T // blk),
        in_specs=[pl.BlockSpec((blk, width), lambda g, i: (i, g)),
                  cmp_spec(HEAD_DIM), cmp_spec(HEAD_DIM),
                  key_spec(ks_col), val_spec(vs_col), key_spec(kw_col), val_spec(vw_col),
                  pl.BlockSpec((blk, LANES), lambda g, i: (i, gate_col + g)),
                  pl.BlockSpec((n_cmp, n_sel), lambda g, i: (0, 0))],
        out_specs=pl.BlockSpec((blk, width), lambda g, i: (i, g)),
        scratch_shapes=[pltpu.VMEM((NSA_REP * blk, 1), F32),
                        pltpu.VMEM((NSA_REP * blk, 1), F32),
                        pltpu.VMEM((NSA_REP * blk, 2 * HEAD_DIM), F32),
                        pltpu.VMEM((NSA_REP * blk, tk), F32)],
        compiler_params=_params(("parallel", "arbitrary")),
        name="nsa_attention",
    )(q_tok, k_cmp, v_cmp, q_tok, v_aug, q_tok, v_aug, gates, c2s)


def _cross_kernel(q_ref, k_ref, v_ref, o_ref):
    s = _dot_nt(q_ref[...], k_ref[...])
    p = jnp.exp(s - jnp.max(s, axis=-1, keepdims=True))
    p = p / jnp.sum(p, axis=-1, keepdims=True)
    o_ref[...] = _dot(p.astype(BF16), v_ref[...]).astype(o_ref.dtype)


def _cross_attention(q, kv, n_heads, tm=1024):
    T, D = q.shape
    M = kv.shape[0]
    dh = D // n_heads
    tm = min(tm, T)
    return pl.pallas_call(
        _cross_kernel,
        out_shape=jax.ShapeDtypeStruct((T, D), BF16),
        grid=(T // tm, n_heads),
        in_specs=[pl.BlockSpec((tm, dh), lambda i, h: (i, h)),
                  pl.BlockSpec((M, dh), lambda i, h: (0, h)),
                  pl.BlockSpec((M, dh), lambda i, h: (0, n_heads + h))],
        out_specs=pl.BlockSpec((tm, dh), lambda i, h: (i, h)),
        compiler_params=_params(("parallel", "arbitrary")),
        name="cross_attention",
    )(q, kv, kv)


def _rope_tables(T):
    half = HEAD_DIM // 2
    inv_freq = ROPE_THETA ** (-np.arange(half, dtype=np.float64) / half)
    ang = np.arange(T, dtype=np.float64)[:, None] * inv_freq[None, :]
    cos, sin = np.cos(ang).astype(np.float32), np.sin(ang).astype(np.float32)
    return jnp.asarray(np.concatenate([cos, cos], axis=1)), jnp.asarray(np.concatenate([-sin, sin], axis=1))


def _pad_to(w, axis, mult):
    pad = -w.shape[axis] % mult
    if not pad:
        return w
    shape = list(w.shape)
    shape[axis] = pad
    return jnp.concatenate([w, jnp.zeros(shape, w.dtype)], axis=axis)


def _ffn(x_f32, x_bf16, w_gate, w_up, w_down, ln_g, ln_b, alpha, emit_bf16):
    d_ff = w_gate.shape[1]
    main = d_ff // FF_TILE * FF_TILE
    assert (d_ff - main) % LANES == 0 and main % (FF_DOWN_STEPS * LANES) == 0
    h = _swiglu_up(x_bf16, w_gate, w_up, FF_TILE, main // FF_TILE)
    wd = w_down.astype(BF16)
    tail = None
    if main < d_ff:
        tail = (_swiglu_up(x_bf16, w_gate[:, main:], w_up[:, main:], d_ff - main, 1), wd[main:])
    return _mm_res_ln(h, wd, x_f32, ln_g, ln_b, alpha, 0.5, emit_bf16, n_k=FF_DOWN_STEPS, tail=tail)


def _mixer(x_f32, x_bf16, w_in, cmp_pos_k, cmp_w1_k, cmp_w2_k, cmp_pos_v, cmp_w1_v, cmp_w2_v,
           mix_norm_g, w_out, ln_g, ln_b, alpha):
    T = x_f32.shape[0]
    G = NSA_KV_GROUPS
    sbw = SB_HEADS * HEAD_DIM
    nqw = NSA_HEADS * HEAD_DIM
    kvw = G * HEAD_DIM
    bounds = np.cumsum([0, sbw, sbw, sbw, nqw, kvw, kvw, kvw, kvw, kvw, kvw, NSA_HEADS * N_GATES])
    (w_sbq, w_sbk, w_sbv, w_nq, w_kc, w_vc, w_ks, w_vs, w_kw, w_vw, w_gate) = [
        w_in[:, bounds[i]:bounds[i + 1]] for i in range(11)]
    per_group = NSA_REP * N_GATES
    w_gate = jnp.concatenate([_pad_to(w_gate[:, g * per_group:(g + 1) * per_group], 1, LANES)
                              for g in range(G)], axis=1)
    tables = _rope_tables(T)
    cat = lambda ws: jnp.concatenate(ws, axis=1).astype(BF16)
    q_scale = lambda n_q, n_rest: jnp.concatenate([jnp.full((n_q,), HEAD_DIM ** -0.5, F32), jnp.ones((n_rest,), F32)])
    plain = _project(x_bf16, cat([w_sbq, w_sbk, w_sbv, w_vs, w_vw]), BF16, 1024, (3 * sbw + 2 * kvw) // 2,
                     col_scale=q_scale(sbw, 2 * sbw + 2 * kvw))
    roped = _project(x_bf16, cat([w_nq, w_ks, w_kw]), BF16, 1024, nqw + 2 * kvw,
                     col_scale=q_scale(nqw, 2 * kvw), rope_tables=tables)
    kc = _project(x_bf16, w_kc.astype(BF16), F32, 1024, kvw, rope_tables=tables)
    vc_gate = _project(x_bf16, cat([w_vc, w_gate]), F32, 1024, 2 * kvw)

    o_sb = _sb_attention(plain, SB_HEADS, 0, SB_HEADS, 2 * SB_HEADS)

    head = lambda a, g: a[:, g * HEAD_DIM:(g + 1) * HEAD_DIM]
    streams = jnp.stack([head(kc, g) for g in range(G)] + [head(vc_gate, g) for g in range(G)])
    rep = lambda a, b: jnp.stack([a] * G + [b] * G)
    cmp = _compress(streams, rep(cmp_pos_k, cmp_pos_v), rep(cmp_w1_k, cmp_w1_v), rep(cmp_w2_k, cmp_w2_v))
    ones = jnp.ones((T, HEAD_DIM), BF16)
    v_aug = jnp.concatenate([a for h in range(2 * G) for a in (head(plain, 3 * SB_HEADS + h), ones)], axis=1)
    o_nsa = _nsa_attention(roped, v_aug, cmp[:G], cmp[G:], vc_gate,
                           ks_col=NSA_HEADS, kw_col=NSA_HEADS + G, vs_col=0, vw_col=G, gate_col=G)
    return _rms_mm_res_ln(o_sb, o_nsa, mix_norm_g, w_out.astype(BF16), x_f32, ln_g, ln_b, alpha)


def _memory_block(x_f32, x_bf16, mem, w_q, w_k, w_v, w_o, ln_g, ln_b, alpha):
    D = w_q.shape[1]
    q = _project(x_bf16, w_q.astype(BF16), BF16, 1024, D,
                 col_scale=jnp.full((D,), (D // MEM_HEADS) ** -0.5, F32))
    kv = _project(mem.astype(BF16), jnp.concatenate([w_k, w_v], axis=1).astype(BF16), BF16, 256, 1024)
    o = _cross_attention(q, kv, MEM_HEADS)
    return _mm_res_ln(o, w_o.astype(BF16), x_f32, ln_g, ln_b, alpha, 1.0, True)


def kernel(x, mem, ln1_g, ln1_b, ffn1_gate, ffn1_up, ffn1_down, w_in, cmp_pos_k, cmp_w1_k, cmp_w2_k, cmp_pos_v, cmp_w1_v, cmp_w2_v, mix_norm_g, w_out, ln2_g, ln2_b, mem_wq, mem_wk, mem_wv, mem_wo, ln3_g, ln3_b, ffn2_gate, ffn2_up, ffn2_down, ln4_g, ln4_b):
    n_layers = ffn1_gate.shape[0]
    alpha = (2 * n_layers) ** 0.25
    outs = []
    for bi in range(x.shape[0]):
        xf = x[bi]
        xb = xf.astype(BF16)
        for l in range(n_layers):
            xf, xb = _ffn(xf, xb, ffn1_gate[l], ffn1_up[l], ffn1_down[l], ln1_g[l], ln1_b[l], alpha, True)
            xf, xb = _mixer(xf, xb, w_in[l], cmp_pos_k[l], cmp_w1_k[l], cmp_w2_k[l], cmp_pos_v[l], cmp_w1_v[l],
                            cmp_w2_v[l], mix_norm_g[l], w_out[l], ln2_g[l], ln2_b[l], alpha)
            xf, xb = _memory_block(xf, xb, mem[bi], mem_wq[l], mem_wk[l], mem_wv[l], mem_wo[l], ln3_g[l], ln3_b[l],
                                   alpha)
            xf, xb = _ffn(xf, xb, ffn2_gate[l], ffn2_up[l], ffn2_down[l], ln4_g[l], ln4_b[l], alpha,
                          l + 1 < n_layers)
        outs.append(xf)
    return outs[0][None] if len(outs) == 1 else jnp.stack(outs)
```

```python
import functools

import numpy as np
import jax
import jax.numpy as jnp
from jax import lax
from jax.experimental import pallas as pl
from jax.experimental.pallas import tpu as pltpu

HEAD_DIM = 128
SB_HEADS = 8
NSA_HEADS = 8
NSA_KV_GROUPS = 2
NSA_REP = NSA_HEADS // NSA_KV_GROUPS
N_GATES = 3
CMP_BLOCK = 32
CMP_STRIDE = 16
SEL_BLOCK = 64
SEL_SHIFT = 6
SEL_TOPK = 16
WINDOW = 512
MEM_HEADS = 4
ROPE_THETA = 10000.0
LN_EPS = 1e-5
RMS_EPS = 1e-6
MASK_FILL = -1e30
LOG2E = 1.4426950408889634
SB_UNDERFLOW = -110.0
assert 1 << SEL_SHIFT == SEL_BLOCK

LANES = 128
SUBLANES = 8
FF_TILE = 512
FF_DOWN_STEPS = 4
VMEM_LIMIT = 56 * 1024 * 1024

BF16 = jnp.bfloat16
F32 = jnp.float32


def _params(sem):
    return pltpu.CompilerParams(dimension_semantics=sem, vmem_limit_bytes=VMEM_LIMIT)


def _dot(a, b):
    return jnp.dot(a, b, preferred_element_type=F32)


def _dot_nt(a, b):
    return lax.dot_general(a, b, (((1,), (1,)), ((), ())), preferred_element_type=F32)


def _split_dot(a, b):
    hi = a.astype(BF16)
    lo = (a - hi.astype(F32)).astype(BF16)
    return _dot(hi, b) + _dot(lo, b)


def _layer_norm(z, g, b):
    mu = jnp.mean(z, axis=-1, keepdims=True)
    zc = z - mu
    var = jnp.mean(zc * zc, axis=-1, keepdims=True)
    return zc * lax.rsqrt(var + LN_EPS) * g + b


def _proj_kernel(x_ref, w_ref, *rest, scaled, rope, w_transposed):
    rest = list(rest)
    o_ref = rest.pop()
    y = (_dot_nt if w_transposed else _dot)(x_ref[...], w_ref[...])
    if scaled:
        y = y * rest.pop(0)[...]
    if rope:
        cos_ref, sin_ref = rest
        c = cos_ref[...]
        s = sin_ref[...]
        heads = []
        for h in range(y.shape[1] // HEAD_DIM):
            yh = y[:, h * HEAD_DIM:(h + 1) * HEAD_DIM]
            heads.append(yh * c + pltpu.roll(yh, HEAD_DIM // 2, 1) * s)
        y = jnp.concatenate(heads, axis=1) if len(heads) > 1 else heads[0]
    o_ref[...] = y.astype(o_ref.dtype)


def _project(x, w, out_dtype, tm, tn, col_scale=None, rope_tables=None, w_transposed=False):
    M, K = x.shape
    N = w.shape[0] if w_transposed else w.shape[1]
    tm = min(tm, M)
    assert M % tm == 0 and N % tn == 0
    in_specs = [pl.BlockSpec((tm, K), lambda i, j: (i, 0)),
                pl.BlockSpec((tn, K), lambda i, j: (j, 0)) if w_transposed
                else pl.BlockSpec((K, tn), lambda i, j: (0, j))]
    args = [x, w]
    if col_scale is not None:
        in_specs.append(pl.BlockSpec((1, tn), lambda i, j: (0, j)))
        args.append(col_scale.reshape(1, N))
    if rope_tables is not None:
        in_specs += [pl.BlockSpec((tm, HEAD_DIM), lambda i, j: (i, 0))] * 2
        args += list(rope_tables)
    return pl.pallas_call(
        functools.partial(_proj_kernel, scaled=col_scale is not None, rope=rope_tables is not None,
                          w_transposed=w_transposed),
        out_shape=jax.ShapeDtypeStruct((M, N), out_dtype),
        grid=(M // tm, N // tn),
        in_specs=in_specs,
        out_specs=pl.BlockSpec((tm, tn), lambda i, j: (i, j)),
        compiler_params=_params(("parallel", "arbitrary")),
        name="proj_rope" if rope_tables is not None else "proj",
    )(*args)


def _swiglu_up_kernel(x_ref, wg_ref, wu_ref, o_ref, wg_bf16, wu_bf16):
    @pl.when(pl.program_id(1) == 0)
    def _():
        wg_bf16[...] = wg_ref[...].astype(BF16)
        wu_bf16[...] = wu_ref[...].astype(BF16)

    x = x_ref[...]
    g = _dot(x, wg_bf16[...])
    u = _dot(x, wu_bf16[...])
    o_ref[...] = (jax.nn.silu(g) * u).astype(o_ref.dtype)


def _swiglu_up(x, wg, wu, tn, n_blocks, tm=1024):
    M, K = x.shape
    tm = min(tm, M)
    assert M % tm == 0 and n_blocks * tn <= wg.shape[1] and wg.shape == wu.shape
    return pl.pallas_call(
        _swiglu_up_kernel,
        out_shape=jax.ShapeDtypeStruct((M, n_blocks * tn), BF16),
        grid=(n_blocks, M // tm),
        in_specs=[pl.BlockSpec((tm, K), lambda j, i: (i, 0)),
                  pl.BlockSpec((K, tn), lambda j, i: (0, j)),
                  pl.BlockSpec((K, tn), lambda j, i: (0, j))],
        out_specs=pl.BlockSpec((tm, tn), lambda j, i: (i, j)),
        scratch_shapes=[pltpu.VMEM((K, tn), BF16), pltpu.VMEM((K, tn), BF16)],
        compiler_params=_params(("parallel", "arbitrary")),
        name="swiglu_up",
    )(x, wg, wu)


def _row_chunks(tm, n_split):
    return [slice(c * tm // n_split, (c + 1) * tm // n_split) for c in range(n_split)]


def _res_ln_store(chunks, totals, res_ref, g_ref, b_ref, of_ref, ob_ref, alpha, coef):
    for c, y in zip(chunks, totals):
        out = _layer_norm(alpha * res_ref[c, :] + coef * y, g_ref[...], b_ref[...])
        of_ref[c, :] = out
        if ob_ref is not None:
            ob_ref[c, :] = out.astype(BF16)


def _mm_res_ln_kernel(*refs, alpha, coef, emit_bf16, n_k, n_split, has_tail):
    refs = list(refs)
    h_ref, w_ref = refs.pop(0), refs.pop(0)
    ht_ref, wt_ref = (refs.pop(0), refs.pop(0)) if has_tail else (None, None)
    res_ref, g_ref, b_ref, of_ref = refs.pop(0), refs.pop(0), refs.pop(0), refs.pop(0)
    ob_ref = refs.pop(0) if emit_bf16 else None
    acc_ref = refs.pop(0) if n_k > 1 else None
    chunks = _row_chunks(h_ref.shape[0], n_split)

    def partial_products():
        return [_dot(h_ref[c, :], w_ref[...]) for c in chunks]

    def finish(totals):
        if has_tail:
            totals = [y + _dot(ht_ref[c, :], wt_ref[...]) for c, y in zip(chunks, totals)]
        _res_ln_store(chunks, totals, res_ref, g_ref, b_ref, of_ref, ob_ref, alpha, coef)

    if n_k == 1:
        finish(partial_products())
        return
    k = pl.program_id(1)

    @pl.when(k == 0)
    def _():
        for c, y in zip(chunks, partial_products()):
            acc_ref[c, :] = y

    @pl.when((k > 0) & (k < n_k - 1))
    def _():
        for c, y in zip(chunks, partial_products()):
            acc_ref[c, :] += y

    @pl.when(k == n_k - 1)
    def _():
        finish([acc_ref[c, :] + y for c, y in zip(chunks, partial_products())])


def _mm_res_ln(h, w, res, g, b, alpha, coef, emit_bf16, tm=512, n_k=1, n_split=2, tail=None):
    M, K = h.shape
    N = w.shape[1]
    tm = min(tm, M)
    tk = K // n_k
    assert M % tm == 0 and K % n_k == 0 and K <= w.shape[0] and (n_k == 1 or tk % LANES == 0)
    row_block = pl.BlockSpec((tm, N), lambda i, k: (i, 0))
    vec = pl.BlockSpec((1, N), lambda i, k: (0, 0))
    in_specs = [pl.BlockSpec((tm, tk), lambda i, k: (i, k)), pl.BlockSpec((tk, N), lambda i, k: (k, 0))]
    args = [h, w]
    if tail is not None:
        kt = tail[0].shape[1]
        in_specs += [pl.BlockSpec((tm, kt), lambda i, k: (i, 0)), pl.BlockSpec((kt, N), lambda i, k: (0, 0))]
        args += list(tail)
    out = pl.pallas_call(
        functools.partial(_mm_res_ln_kernel, alpha=alpha, coef=coef, emit_bf16=emit_bf16,
                          n_k=n_k, n_split=n_split, has_tail=tail is not None),
        out_shape=[jax.ShapeDtypeStruct((M, N), F32)] + [jax.ShapeDtypeStruct((M, N), BF16)] * emit_bf16,
        grid=(M // tm, n_k),
        in_specs=in_specs + [row_block, vec, vec],
        out_specs=[row_block] * (1 + emit_bf16),
        scratch_shapes=[pltpu.VMEM((tm, N), F32)] if n_k > 1 else [],
        compiler_params=_params(("parallel", "arbitrary")),
        name="mm_res_ln",
    )(*args, res, g.reshape(1, N), b.reshape(1, N))
    return (out[0], out[1]) if emit_bf16 else (out[0], None)


def _rms_mm_res_ln_kernel(a_ref, b_ref, gain_ref, w_ref, res_ref, g_ref, beta_ref, of_ref, ob_ref, *,
                          alpha, n_split):
    chunks = _row_chunks(a_ref.shape[0], n_split)
    wa = a_ref.shape[1]
    gain = gain_ref[...]

    def rms(o, gn):
        return (o * lax.rsqrt(jnp.mean(o * o, axis=-1, keepdims=True) + RMS_EPS) * gn).astype(BF16)

    totals = [_dot(jnp.concatenate([rms(a_ref[c, :], gain[:, :wa]), rms(b_ref[c, :], gain[:, wa:])], axis=1),
                   w_ref[...]) for c in chunks]
    _res_ln_store(chunks, totals, res_ref, g_ref, beta_ref, of_ref, ob_ref, alpha, 1.0)


def _rms_mm_res_ln(o_a, o_b, gain, w, res, g, b, alpha, tm=512, n_split=2):
    M, wa = o_a.shape
    wb = o_b.shape[1]
    N = w.shape[1]
    tm = min(tm, M)
    assert M % tm == 0 and w.shape[0] == wa + wb
    row_block = pl.BlockSpec((tm, N), lambda i: (i, 0))
    vec = pl.BlockSpec((1, N), lambda i: (0, 0))
    return pl.pallas_call(
        functools.partial(_rms_mm_res_ln_kernel, alpha=alpha, n_split=n_split),
        out_shape=[jax.ShapeDtypeStruct((M, N), F32), jax.ShapeDtypeStruct((M, N), BF16)],
        grid=(M // tm,),
        in_specs=[pl.BlockSpec((tm, wa), lambda i: (i, 0)),
                  pl.BlockSpec((tm, wb), lambda i: (i, 0)),
                  pl.BlockSpec((1, wa + wb), lambda i: (0, 0)),
                  pl.BlockSpec((wa + wb, N), lambda i: (0, 0)),
                  row_block, vec, vec],
        out_specs=[row_block, row_block],
        compiler_params=_params(("parallel",)),
        name="rms_mm_res_ln",
    )(o_a, o_b, gain.reshape(1, wa + wb), w, res, g.reshape(1, N), b.reshape(1, N))


def _sb_kernel(q_ref, k_ref, v_ref, o_ref, acc_ref, c_ref, *, blk, heads):
    qi = pl.program_id(1)
    row = lax.broadcasted_iota(jnp.int32, (blk, blk), 0)
    col = lax.broadcasted_iota(jnp.int32, (blk, blk), 1)
    suffix = jnp.where(row > col, 1.0, 0.0).astype(BF16)
    strict = col < row

    cols = [slice(h * HEAD_DIM, (h + 1) * HEAD_DIM) for h in range(heads)]

    def step(kb_far, n_blocks, diagonal):
        ks = pl.multiple_of(kb_far * blk, blk)
        key_rows = [pl.ds(ks + b * blk, blk) for b in range(n_blocks)]
        log_beta, log_1m = {}, {}
        for p in [(b, h) for b in reversed(range(n_blocks)) for h in range(heads)]:
            y = _dot_nt(q_ref[:, cols[p[1]]], k_ref[key_rows[p[0]], cols[p[1]]])
            lb = jnp.minimum(y, 0.0) - jnp.log(1.0 + jnp.exp2(jnp.abs(y) * -LOG2E))
            l1 = lb - y
            if diagonal and p[0] == n_blocks - 1:
                l1 = jnp.where(strict, l1, 0.0)
            log_beta[p] = lb
            log_1m[p] = l1.astype(BF16)
        weights = {}
        for h in range(heads):
            c = c_ref[h]
            for b in reversed(range(n_blocks)):
                p = (b, h)
                log_stay = _dot(log_1m[p], suffix)
                w = jnp.exp(log_beta[p] + (log_stay + c))
                if diagonal and b == n_blocks - 1:
                    w = jnp.where(strict, w, 0.0)
                weights[p] = w.astype(BF16)
                c = c + (log_stay[:, :1] + log_1m[p][:, :1].astype(F32))
            c_ref[h] = c
        for h in range(heads):
            w = jnp.concatenate([weights[(b, h)] for b in range(n_blocks)], axis=1)
            acc_ref[:, cols[h]] += _dot(w, v_ref[pl.ds(ks, n_blocks * blk), cols[h]])

    acc_ref[...] = jnp.zeros_like(acc_ref)
    c_ref[...] = jnp.zeros_like(c_ref)

    @pl.when(qi == 0)
    def _():
        step(0, 1, True)

    @pl.when(qi > 0)
    def _():
        step(qi - 1, 2, True)

    def more(carry):
        done, c_max = carry
        return (done < qi) & (c_max >= SB_UNDERFLOW)

    def body(carry):
        done, _ = carry
        step(qi - 1 - done, 1, False)
        return done + 1, jnp.max(c_ref[...])

    lax.while_loop(more, body, (jnp.minimum(qi, 1), jnp.max(c_ref[...])))
    o_ref[...] = acc_ref[...]


def _sb_attention(qkv, n_heads, q_col, k_col, v_col, blk=256, heads=4):
    T = qkv.shape[0]
    blk = min(blk, T)
    width = heads * HEAD_DIM
    assert T % blk == 0 and n_heads % heads == 0
    assert q_col % heads == 0 and k_col % heads == 0 and v_col % heads == 0
    return pl.pallas_call(
        functools.partial(_sb_kernel, blk=blk, heads=heads),
        out_shape=jax.ShapeDtypeStruct((T, n_heads * HEAD_DIM), F32),
        grid=(n_heads // heads, T // blk),
        in_specs=[pl.BlockSpec((blk, width), lambda h, i: (i, q_col // heads + h)),
                  pl.BlockSpec((T, width), lambda h, i: (0, k_col // heads + h)),
                  pl.BlockSpec((T, width), lambda h, i: (0, v_col // heads + h))],
        out_specs=pl.BlockSpec((blk, width), lambda h, i: (i, h)),
        scratch_shapes=[pltpu.VMEM((blk, width), F32), pltpu.VMEM((heads, blk, 1), F32)],
        compiler_params=_params(("parallel", "arbitrary")),
        name="sb_attention",
    )(qkv, qkv, qkv)


def _compress_kernel(x_ref, pos_ref, w1_ref, w2_ref, o_ref):
    n = x_ref.shape[1]
    half = CMP_STRIDE * HEAD_DIM
    x = x_ref[0].astype(BF16)
    w1 = w1_ref[0].astype(BF16)
    a = _dot(x, w1[:half])
    b = _dot(x, w1[half:])
    pos = _dot(pos_ref[0].astype(BF16), w1)
    b_next = pltpu.roll(b, n - 1, 0)
    hid = jax.nn.gelu(a + b_next + pos[0:1])
    out = _dot(hid.astype(BF16), w2_ref[0].astype(BF16))
    valid = lax.broadcasted_iota(jnp.int32, out.shape, 0) < n - 1
    o_ref[0] = jnp.where(valid, out, 0.0).astype(o_ref.dtype)


def _compress(x_tok, pos_emb, w1, w2):
    S, T, _ = x_tok.shape
    n = T // CMP_STRIDE
    width = CMP_STRIDE * HEAD_DIM
    hidden = w1.shape[-1]
    x2 = x_tok.reshape(S, n, width)
    pos_flat = jnp.broadcast_to(pos_emb.reshape(S, 1, 2 * width), (S, SUBLANES, 2 * width))
    return pl.pallas_call(
        _compress_kernel,
        out_shape=jax.ShapeDtypeStruct((S, n, HEAD_DIM), BF16),
        grid=(S,),
        in_specs=[pl.BlockSpec((1, n, width), lambda s: (s, 0, 0)),
                  pl.BlockSpec((1, SUBLANES, 2 * width), lambda s: (s, 0, 0)),
                  pl.BlockSpec((1, 2 * width, hidden), lambda s: (s, 0, 0)),
                  pl.BlockSpec((1, hidden, HEAD_DIM), lambda s: (s, 0, 0))],
        out_specs=pl.BlockSpec((1, n, HEAD_DIM), lambda s: (s, 0, 0)),
        compiler_params=_params(("parallel",)),
        name="nsa_compress",
    )(x2, pos_flat, w1, w2)


def _nsa_kernel(q_ref, kc_ref, vc_ref, ks_ref, vs_ref, kw_ref, vw_ref, gate_ref, c2s_ref, o_ref,
                m_ref, mb_ref, acc_ref, s_ref, *, blk, tk):
    R = NSA_REP
    qi = pl.program_id(1)
    q0 = qi * blk
    n_cmp = kc_ref.shape[1]
    n_sel = c2s_ref.shape[1]
    rows = [slice(r * blk, (r + 1) * blk) for r in range(R)]
    qs = jnp.concatenate([q_ref[:, r * HEAD_DIM:(r + 1) * HEAD_DIM] for r in range(R)], axis=0)
    q_pos = q0 + lax.broadcasted_iota(jnp.int32, (blk, 1), 0)
    q_lane = q0 + lax.broadcasted_iota(jnp.int32, (1, blk), 1)

    def softmax_av(s, bias, v1):
        sc = s + bias
        p = jnp.exp(sc - jnp.max(sc, axis=-1, keepdims=True)).astype(BF16)
        pv = _dot(p, v1)
        return pv[:, :HEAD_DIM] / pv[:, HEAD_DIM:]

    kc = kc_ref[0]
    s_cmp = [_dot_nt(qs[rows[r]], kc) for r in range(R)]
    cmp_end = lax.broadcasted_iota(jnp.int32, (1, n_cmp), 1) * CMP_STRIDE + (CMP_BLOCK - 1)
    bias_c = jnp.where(cmp_end <= q_pos, 0.0, MASK_FILL)
    o_cmp, p_sum = [], None
    for r in range(R):
        sc = s_cmp[r] + bias_c
        p = jnp.exp(sc - jnp.max(sc, axis=-1, keepdims=True))
        inv_sum = 1.0 / jnp.sum(p, axis=-1, keepdims=True)
        o_cmp.append(jnp.where(q_pos >= CMP_BLOCK - 1, _dot(p.astype(BF16), vc_ref[0]) * inv_sum, 0.0))
        p_sum = p * inv_sum if p_sum is None else p_sum + p * inv_sum
    imp = _split_dot(p_sum, c2s_ref[...])

    wk = WINDOW + blk
    ws = pl.multiple_of(jnp.maximum(q0 - WINDOW, 0), blk)
    dist = q_pos - (ws + lax.broadcasted_iota(jnp.int32, (1, wk), 1))
    bias_w = jnp.where((dist >= 0) & (dist < WINDOW), 0.0, MASK_FILL)
    k_w = kw_ref[pl.ds(ws, wk), :]
    s_w = [_dot_nt(qs[rows[r]], k_w) for r in range(R)]

    blk_id = lax.broadcasted_iota(jnp.int32, (n_sel, 1), 0)
    forced = (blk_id == 0) | (blk_id == (q_lane >> SEL_SHIFT))
    valid = blk_id * SEL_BLOCK <= q_lane
    work = jnp.where(forced, jnp.inf, jnp.where(valid, imp.T, -jnp.inf))
    blk_idf = blk_id.astype(F32)
    left = work
    o_win = []
    n_steps = min(SEL_TOPK, n_sel)
    for step in range(n_steps):
        top = jnp.max(left, axis=0, keepdims=True)
        first = jnp.min(jnp.where(left == top, blk_idf, float(n_sel)), axis=0, keepdims=True)
        left = jnp.where(blk_idf == first, -jnp.inf, left)
        while len(o_win) * n_steps < (step + 1) * R:
            o_win.append(softmax_av(s_w[len(o_win)], bias_w, vw_ref[pl.ds(ws, wk), :]))
    sel_bias = jnp.where(left != work, 0.0, MASK_FILL).T.astype(BF16)

    gates = jax.nn.sigmoid(gate_ref[...])

    def gate(r, branch):
        return gates[:, r * N_GATES + branch:r * N_GATES + branch + 1]

    o_ref[...] = jnp.concatenate([gate(r, 0) * o_cmp[r] + gate(r, 2) * o_win[r] for r in range(R)], axis=1)

    tok = lax.broadcasted_iota(jnp.int32, (n_sel, tk), 1)
    sel_row = lax.broadcasted_iota(jnp.int32, (n_sel, tk), 0)
    key_off = lax.broadcasted_iota(jnp.int32, (1, tk), 1)

    def key_rows(kb):
        return pl.ds(pl.multiple_of(kb * tk, tk), tk)

    def block_bias(kb):
        ks = kb * tk
        expand = jnp.where(((ks + tok) >> SEL_SHIFT) == sel_row, 1.0, 0.0).astype(BF16)
        bias = _dot(sel_bias, expand)
        return jnp.where(ks + key_off <= q_pos, bias, MASK_FILL)

    def put_scores(r, raw, bias):
        sc = raw + bias
        s_ref[rows[r], :] = sc
        mb_ref[rows[r], :] = jnp.max(sc, axis=-1, keepdims=True)

    def take_probs(r):
        m_old = m_ref[rows[r], :]
        m_new = jnp.maximum(m_old, mb_ref[rows[r], :])
        m_ref[rows[r], :] = m_new
        return jnp.exp(s_ref[rows[r], :] - m_new).astype(BF16), jnp.exp(m_old - m_new)

    def accumulate(r, p, alpha, v1):
        acc_ref[rows[r], :] = alpha * acc_ref[rows[r], :] + _dot(p, v1)

    m_ref[...] = jnp.full_like(m_ref, MASK_FILL)
    acc_ref[...] = jnp.zeros_like(acc_ref)
    kb_diag = qi // (tk // blk)
    bias = block_bias(0)
    for r in range(R):
        put_scores(r, _dot_nt(qs[rows[r]], ks_ref[key_rows(0), :]), bias)

    def sel_body(kb, carry):
        k_next = ks_ref[key_rows(kb + 1), :]
        bias_next = block_bias(kb + 1)
        v1 = vs_ref[key_rows(kb), :]
        for r in range(R):
            p, alpha = take_probs(r)
            put_scores(r, _dot_nt(qs[rows[r]], k_next), bias_next)
            accumulate(r, p, alpha, v1)
        return carry

    lax.fori_loop(0, kb_diag, sel_body, 0)
    v1 = vs_ref[key_rows(kb_diag), :]
    for r in range(R):
        p, alpha = take_probs(r)
        accumulate(r, p, alpha, v1)
    o_sel = acc_ref[:, :HEAD_DIM] / acc_ref[:, HEAD_DIM:]
    o_ref[...] += jnp.concatenate([gate(r, 1) * o_sel[rows[r]] for r in range(R)], axis=1)


def _nsa_attention(q_tok, v_aug, k_cmp, v_cmp, gates, ks_col, kw_col, vs_col, vw_col, gate_col,
                   blk=256, tk=1024):
    T = q_tok.shape[0]
    G = NSA_KV_GROUPS
    blk = min(blk, T)
    tk = min(tk, T)
    assert T % tk == 0 and tk % blk == 0 and WINDOW % blk == 0 and WINDOW + blk <= T
    n_cmp = T // CMP_STRIDE
    n_sel = T // SEL_BLOCK
    cmp_start = np.arange(n_cmp)[:, None] * CMP_STRIDE
    sel_start = np.arange(n_sel)[None, :] * SEL_BLOCK
    overlap = np.clip(np.minimum(cmp_start + CMP_BLOCK, sel_start + SEL_BLOCK)
                      - np.maximum(cmp_start, sel_start), 0, None)
    c2s = jnp.asarray(overlap.astype(np.float32) / CMP_BLOCK, dtype=BF16)
    width = NSA_REP * HEAD_DIM
    key_spec = lambda col: pl.BlockSpec((T, HEAD_DIM), lambda g, i: (0, col + g))
    val_spec = lambda col: pl.BlockSpec((T, 2 * HEAD_DIM), lambda g, i: (0, col + g))
    cmp_spec = lambda w: pl.BlockSpec((1, n_cmp, w), lambda g, i: (g, 0, 0))
    return pl.pallas_call(
        functools.partial(_nsa_kernel, blk=blk, tk=tk),
        out_shape=jax.ShapeDtypeStruct((T, G * width), F32),
        grid=(G, T // blk),
        in_specs=[pl.BlockSpec((blk, width), lambda g, i: (i, g)),
                  cmp_spec(HEAD_DIM), cmp_spec(HEAD_DIM),
                  key_spec(ks_col), val_spec(vs_col), key_spec(kw_col), val_spec(vw_col),
                  pl.BlockSpec((blk, LANES), lambda g, i: (i, gate_col + g)),
                  pl.BlockSpec((n_cmp, n_sel), lambda g, i: (0, 0))],
        out_specs=pl.BlockSpec((blk, width), lambda g, i: (i, g)),
        scratch_shapes=[pltpu.VMEM((NSA_REP * blk, 1), F32),
                        pltpu.VMEM((NSA_REP * blk, 1), F32),
                        pltpu.VMEM((NSA_REP * blk, 2 * HEAD_DIM), F32),
                        pltpu.VMEM((NSA_REP * blk, tk), F32)],
        compiler_params=_params(("parallel", "arbitrary")),
        name="nsa_attention",
    )(q_tok, k_cmp, v_cmp, q_tok, v_aug, q_tok, v_aug, gates, c2s)


def _cross_kernel(q_ref, kv_ref, o_ref, *, n_heads):
    D = q_ref.shape[1]
    dh = D // n_heads
    scores = [_dot_nt(q_ref[:, h * dh:(h + 1) * dh], kv_ref[:, h * dh:(h + 1) * dh]) for h in range(n_heads)]
    for h, s in enumerate(scores):
        p = jnp.exp(s - jnp.max(s, axis=-1, keepdims=True))
        p = p / jnp.sum(p, axis=-1, keepdims=True)
        o_ref[:, h * dh:(h + 1) * dh] = _dot(p.astype(BF16), kv_ref[:, D + h * dh:D + (h + 1) * dh]).astype(o_ref.dtype)


def _cross_attention(q, kv, n_heads, tm=1024):
    T, D = q.shape
    M = kv.shape[0]
    tm = min(tm, T)
    return pl.pallas_call(
        functools.partial(_cross_kernel, n_heads=n_heads),
        out_shape=jax.ShapeDtypeStruct((T, D), BF16),
        grid=(T // tm,),
        in_specs=[pl.BlockSpec((tm, D), lambda i: (i, 0)),
                  pl.BlockSpec((M, 2 * D), lambda i: (0, 0))],
        out_specs=pl.BlockSpec((tm, D), lambda i: (i, 0)),
        compiler_params=_params(("parallel",)),
        name="cross_attention",
    )(q, kv)


def _rope_tables(T):
    half = HEAD_DIM // 2
    inv_freq = ROPE_THETA ** (-np.arange(half, dtype=np.float64) / half)
    ang = np.arange(T, dtype=np.float64)[:, None] * inv_freq[None, :]
    cos, sin = np.cos(ang).astype(np.float32), np.sin(ang).astype(np.float32)
    return jnp.asarray(np.concatenate([cos, cos], axis=1)), jnp.asarray(np.concatenate([-sin, sin], axis=1))


def _pad_to(w, axis, mult):
    pad = -w.shape[axis] % mult
    if not pad:
        return w
    shape = list(w.shape)
    shape[axis] = pad
    return jnp.concatenate([w, jnp.zeros(shape, w.dtype)], axis=axis)


def _ffn(x_f32, x_bf16, w_gate, w_up, w_down, ln_g, ln_b, alpha, emit_bf16):
    d_ff = w_gate.shape[1]
    main = d_ff // FF_TILE * FF_TILE
    assert (d_ff - main) % LANES == 0 and main % (FF_DOWN_STEPS * LANES) == 0
    h = _swiglu_up(x_bf16, w_gate, w_up, FF_TILE, main // FF_TILE)
    wd = w_down.astype(BF16)
    tail = None
    if main < d_ff:
        tail = (_swiglu_up(x_bf16, w_gate[:, main:], w_up[:, main:], d_ff - main, 1), wd[main:])
    return _mm_res_ln(h, wd, x_f32, ln_g, ln_b, alpha, 0.5, emit_bf16, n_k=FF_DOWN_STEPS, tail=tail)


def _mixer(x_f32, x_bf16, w_in, cmp_pos_k, cmp_w1_k, cmp_w2_k, cmp_pos_v, cmp_w1_v, cmp_w2_v,
           mix_norm_g, w_out, ln_g, ln_b, alpha):
    T = x_f32.shape[0]
    G = NSA_KV_GROUPS
    sbw = SB_HEADS * HEAD_DIM
    nqw = NSA_HEADS * HEAD_DIM
    kvw = G * HEAD_DIM
    bounds = np.cumsum([0, sbw, sbw, sbw, nqw, kvw, kvw, kvw, kvw, kvw, kvw, NSA_HEADS * N_GATES])
    w_t = w_in.T
    (w_sbq, w_sbk, w_sbv, w_nq, w_kc, w_vc, w_ks, w_vs, w_kw, w_vw, w_gate) = [
        w_t[bounds[i]:bounds[i + 1]] for i in range(11)]
    per_group = NSA_REP * N_GATES
    w_gate = jnp.concatenate([_pad_to(w_gate[g * per_group:(g + 1) * per_group], 0, LANES)
                              for g in range(G)], axis=0)
    tables = _rope_tables(T)
    cat = lambda ws: jnp.concatenate(ws, axis=0).astype(BF16)
    q_scale = lambda n_q, n_rest: jnp.concatenate([jnp.full((n_q,), HEAD_DIM ** -0.5, F32), jnp.ones((n_rest,), F32)])
    plain = _project(x_bf16, cat([w_sbq, w_sbk, w_sbv, w_vs, w_vw]), BF16, 1024, (3 * sbw + 2 * kvw) // 2,
                     col_scale=q_scale(sbw, 2 * sbw + 2 * kvw), w_transposed=True)
    roped = _project(x_bf16, cat([w_nq, w_ks, w_kw]), BF16, 1024, nqw + 2 * kvw,
                     col_scale=q_scale(nqw, 2 * kvw), rope_tables=tables, w_transposed=True)
    kc = _project(x_bf16, w_kc.astype(BF16), F32, 1024, kvw, rope_tables=tables, w_transposed=True)
    vc_gate = _project(x_bf16, cat([w_vc, w_gate]), F32, 1024, 2 * kvw, w_transposed=True)

    o_sb = _sb_attention(plain, SB_HEADS, 0, SB_HEADS, 2 * SB_HEADS)

    head = lambda a, g: a[:, g * HEAD_DIM:(g + 1) * HEAD_DIM]
    streams = jnp.stack([head(kc, g) for g in range(G)] + [head(vc_gate, g) for g in range(G)])
    rep = lambda a, b: jnp.stack([a] * G + [b] * G)
    cmp = _compress(streams, rep(cmp_pos_k, cmp_pos_v), rep(cmp_w1_k, cmp_w1_v), rep(cmp_w2_k, cmp_w2_v))
    ones = jnp.ones((T, HEAD_DIM), BF16)
    v_aug = jnp.concatenate([a for h in range(2 * G) for a in (head(plain, 3 * SB_HEADS + h), ones)], axis=1)
    o_nsa = _nsa_attention(roped, v_aug, cmp[:G], cmp[G:], vc_gate,
                           ks_col=NSA_HEADS, kw_col=NSA_HEADS + G, vs_col=0, vw_col=G, gate_col=G)
    return _rms_mm_res_ln(o_sb, o_nsa, mix_norm_g, w_out.astype(BF16), x_f32, ln_g, ln_b, alpha)


def _memory_block(x_f32, x_bf16, mem, w_q, w_k, w_v, w_o, ln_g, ln_b, alpha):
    D = w_q.shape[1]
    q = _project(x_bf16, w_q.astype(BF16), BF16, 1024, D,
                 col_scale=jnp.full((D,), (D // MEM_HEADS) ** -0.5, F32))
    kv = _project(mem.astype(BF16), jnp.concatenate([w_k, w_v], axis=1).astype(BF16), BF16, 256, 1024)
    o = _cross_attention(q, kv, MEM_HEADS)
    return _mm_res_ln(o, w_o.astype(BF16), x_f32, ln_g, ln_b, alpha, 1.0, True)


def kernel(x, mem, ln1_g, ln1_b, ffn1_gate, ffn1_up, ffn1_down, w_in, cmp_pos_k, cmp_w1_k, cmp_w2_k, cmp_pos_v, cmp_w1_v, cmp_w2_v, mix_norm_g, w_out, ln2_g, ln2_b, mem_wq, mem_wk, mem_wv, mem_wo, ln3_g, ln3_b, ffn2_gate, ffn2_up, ffn2_down, ln4_g, ln4_b):
    n_layers = ffn1_gate.shape[0]
    alpha = (2 * n_layers) ** 0.25
    outs = []
    for bi in range(x.shape[0]):
        xf = x[bi]
        xb = xf.astype(BF16)
        for l in range(n_layers):
            xf, xb = _ffn(xf, xb, ffn1_gate[l], ffn1_up[l], ffn1_down[l], ln1_g[l], ln1_b[l], alpha, True)
            xf, xb = _mixer(xf, xb, w_in[l], cmp_pos_k[l], cmp_w1_k[l], cmp_w2_k[l], cmp_pos_v[l], cmp_w1_v[l],
                            cmp_w2_v[l], mix_norm_g[l], w_out[l], ln2_g[l], ln2_b[l], alpha)
            xf, xb = _memory_block(xf, xb, mem[bi], mem_wq[l], mem_wk[l], mem_wv[l], mem_wo[l], ln3_g[l], ln3_b[l],
                                   alpha)
            xf, xb = _ffn(xf, xb, ffn2_gate[l], ffn2_up[l], ffn2_down[l], ln4_g[l], ln4_b[l], alpha,
                          l + 1 < n_layers)
        outs.append(xf)
    return outs[0][None] if len(outs) == 1 else jnp.stack(outs)
```

```python
import functools

import numpy as np
import jax
import jax.numpy as jnp
from jax import lax
from jax.experimental import pallas as pl
from jax.experimental.pallas import tpu as pltpu

HEAD_DIM = 128
SB_HEADS = 8
NSA_HEADS = 8
NSA_KV_GROUPS = 2
NSA_REP = NSA_HEADS // NSA_KV_GROUPS
N_GATES = 3
CMP_BLOCK = 32
CMP_STRIDE = 16
SEL_BLOCK = 64
SEL_SHIFT = 6
SEL_TOPK = 16
WINDOW = 512
MEM_HEADS = 4
ROPE_THETA = 10000.0
LN_EPS = 1e-5
RMS_EPS = 1e-6
MASK_FILL = -1e30
LOG2E = 1.4426950408889634
SB_UNDERFLOW = -110.0
assert 1 << SEL_SHIFT == SEL_BLOCK

LANES = 128
SUBLANES = 8
VMEM_LIMIT = 56 * 1024 * 1024

ROW_TILE = 1024
LN_ROW_TILE = 512
FF_TILE = 512
FF_DOWN_STEPS = 4
SB_BLOCK = 256
SB_HEADS_PER_STEP = 4
NSA_QUERY_BLOCK = 256
NSA_KEY_BLOCK = 1024

BF16 = jnp.bfloat16
F32 = jnp.float32


def _params(sem):
    return pltpu.CompilerParams(dimension_semantics=sem, vmem_limit_bytes=VMEM_LIMIT)


def _dot(a, b):
    return jnp.dot(a, b, preferred_element_type=F32)


def _dot_nt(a, b):
    return lax.dot_general(a, b, (((1,), (1,)), ((), ())), preferred_element_type=F32)


def _split_dot(a, b):
    hi = a.astype(BF16)
    lo = (a - hi.astype(F32)).astype(BF16)
    return _dot(hi, b) + _dot(lo, b)


def _layer_norm(z, g, b):
    mu = jnp.mean(z, axis=-1, keepdims=True)
    zc = z - mu
    var = jnp.mean(zc * zc, axis=-1, keepdims=True)
    return zc * lax.rsqrt(var + LN_EPS) * g + b


def _proj_kernel(x_ref, w_ref, *rest, scaled, rope_heads, w_transposed):
    rest = list(rest)
    o_ref = rest.pop()
    y = (_dot_nt if w_transposed else _dot)(x_ref[...], w_ref[...])
    if scaled:
        y = y * rest.pop(0)[...]
    if rope_heads:
        cos_ref, sin_ref = rest
        c = cos_ref[...]
        s = sin_ref[...]
        heads = []
        for h in range(rope_heads):
            yh = y[:, h * HEAD_DIM:(h + 1) * HEAD_DIM]
            heads.append(yh * c + pltpu.roll(yh, HEAD_DIM // 2, 1) * s)
        if rope_heads * HEAD_DIM < y.shape[1]:
            heads.append(y[:, rope_heads * HEAD_DIM:])
        y = jnp.concatenate(heads, axis=1) if len(heads) > 1 else heads[0]
    o_ref[...] = y.astype(o_ref.dtype)


def _project(x, w, out_dtype, tm, tn, col_scale=None, rope_tables=None, rope_cols=None, w_transposed=False):
    M, K = x.shape
    N = w.shape[0] if w_transposed else w.shape[1]
    tm = min(tm, M)
    assert M % tm == 0 and N % tn == 0
    rope_heads = 0 if rope_tables is None else (tn if rope_cols is None else rope_cols) // HEAD_DIM
    in_specs = [pl.BlockSpec((tm, K), lambda i, j: (i, 0)),
                pl.BlockSpec((tn, K), lambda i, j: (j, 0)) if w_transposed
                else pl.BlockSpec((K, tn), lambda i, j: (0, j))]
    args = [x, w]
    if col_scale is not None:
        in_specs.append(pl.BlockSpec((1, tn), lambda i, j: (0, j)))
        args.append(col_scale.reshape(1, N))
    if rope_tables is not None:
        in_specs += [pl.BlockSpec((tm, HEAD_DIM), lambda i, j: (i, 0))] * 2
        args += list(rope_tables)
    return pl.pallas_call(
        functools.partial(_proj_kernel, scaled=col_scale is not None, rope_heads=rope_heads,
                          w_transposed=w_transposed),
        out_shape=jax.ShapeDtypeStruct((M, N), out_dtype),
        grid=(M // tm, N // tn),
        in_specs=in_specs,
        out_specs=pl.BlockSpec((tm, tn), lambda i, j: (i, j)),
        compiler_params=_params(("parallel", "arbitrary")),
        name="proj_rope" if rope_tables is not None else "proj",
    )(*args)


def _swiglu_up_kernel(x_ref, wg_ref, wu_ref, o_ref, wg_bf16, wu_bf16):
    @pl.when(pl.program_id(1) == 0)
    def _():
        wg_bf16[...] = wg_ref[...].astype(BF16)
        wu_bf16[...] = wu_ref[...].astype(BF16)

    x = x_ref[...]
    g = _dot(x, wg_bf16[...])
    u = _dot(x, wu_bf16[...])
    o_ref[...] = (jax.nn.silu(g) * u).astype(o_ref.dtype)


def _swiglu_up(x, wg, wu, tn, n_blocks, tm=ROW_TILE):
    M, K = x.shape
    tm = min(tm, M)
    assert M % tm == 0 and n_blocks * tn <= wg.shape[1] and wg.shape == wu.shape
    return pl.pallas_call(
        _swiglu_up_kernel,
        out_shape=jax.ShapeDtypeStruct((M, n_blocks * tn), BF16),
        grid=(n_blocks, M // tm),
        in_specs=[pl.BlockSpec((tm, K), lambda j, i: (i, 0)),
                  pl.BlockSpec((K, tn), lambda j, i: (0, j)),
                  pl.BlockSpec((K, tn), lambda j, i: (0, j))],
        out_specs=pl.BlockSpec((tm, tn), lambda j, i: (i, j)),
        scratch_shapes=[pltpu.VMEM((K, tn), BF16), pltpu.VMEM((K, tn), BF16)],
        compiler_params=_params(("parallel", "arbitrary")),
        name="swiglu_up",
    )(x, wg, wu)


def _row_chunks(tm, n_split):
    return [slice(c * tm // n_split, (c + 1) * tm // n_split) for c in range(n_split)]


def _res_ln_store(chunks, totals, res_ref, g_ref, b_ref, of_ref, ob_ref, alpha, coef):
    for c, y in zip(chunks, totals):
        out = _layer_norm(alpha * res_ref[c, :] + coef * y, g_ref[...], b_ref[...])
        of_ref[c, :] = out
        if ob_ref is not None:
            ob_ref[c, :] = out.astype(BF16)


def _mm_res_ln_kernel(*refs, alpha, coef, emit_bf16, n_k, n_split, has_tail):
    refs = list(refs)
    h_ref, w_ref = refs.pop(0), refs.pop(0)
    ht_ref, wt_ref = (refs.pop(0), refs.pop(0)) if has_tail else (None, None)
    res_ref, g_ref, b_ref, of_ref = refs.pop(0), refs.pop(0), refs.pop(0), refs.pop(0)
    ob_ref = refs.pop(0) if emit_bf16 else None
    acc_ref = refs.pop(0) if n_k > 1 else None
    chunks = _row_chunks(h_ref.shape[0], n_split)

    def partial_products():
        return [_dot(h_ref[c, :], w_ref[...]) for c in chunks]

    def finish(totals):
        if has_tail:
            totals = [y + _dot(ht_ref[c, :], wt_ref[...]) for c, y in zip(chunks, totals)]
        _res_ln_store(chunks, totals, res_ref, g_ref, b_ref, of_ref, ob_ref, alpha, coef)

    if n_k == 1:
        finish(partial_products())
        return
    k = pl.program_id(1)

    @pl.when(k == 0)
    def _():
        for c, y in zip(chunks, partial_products()):
            acc_ref[c, :] = y

    @pl.when((k > 0) & (k < n_k - 1))
    def _():
        for c, y in zip(chunks, partial_products()):
            acc_ref[c, :] += y

    @pl.when(k == n_k - 1)
    def _():
        finish([acc_ref[c, :] + y for c, y in zip(chunks, partial_products())])


def _mm_res_ln(h, w, res, g, b, alpha, coef, emit_bf16, tm=LN_ROW_TILE, n_k=1, n_split=2, tail=None):
    M, K = h.shape
    N = w.shape[1]
    tm = min(tm, M)
    tk = K // n_k
    assert M % tm == 0 and K % n_k == 0 and K <= w.shape[0] and (n_k == 1 or tk % LANES == 0)
    row_block = pl.BlockSpec((tm, N), lambda i, k: (i, 0))
    vec = pl.BlockSpec((1, N), lambda i, k: (0, 0))
    in_specs = [pl.BlockSpec((tm, tk), lambda i, k: (i, k)), pl.BlockSpec((tk, N), lambda i, k: (k, 0))]
    args = [h, w]
    if tail is not None:
        kt = tail[0].shape[1]
        in_specs += [pl.BlockSpec((tm, kt), lambda i, k: (i, 0)), pl.BlockSpec((kt, N), lambda i, k: (0, 0))]
        args += list(tail)
    out = pl.pallas_call(
        functools.partial(_mm_res_ln_kernel, alpha=alpha, coef=coef, emit_bf16=emit_bf16,
                          n_k=n_k, n_split=n_split, has_tail=tail is not None),
        out_shape=[jax.ShapeDtypeStruct((M, N), F32)] + [jax.ShapeDtypeStruct((M, N), BF16)] * emit_bf16,
        grid=(M // tm, n_k),
        in_specs=in_specs + [row_block, vec, vec],
        out_specs=[row_block] * (1 + emit_bf16),
        scratch_shapes=[pltpu.VMEM((tm, N), F32)] if n_k > 1 else [],
        compiler_params=_params(("parallel", "arbitrary")),
        name="mm_res_ln",
    )(*args, res, g.reshape(1, N), b.reshape(1, N))
    return (out[0], out[1]) if emit_bf16 else (out[0], None)


def _rms_mm_res_ln_kernel(a_ref, b_ref, gain_ref, w_ref, res_ref, g_ref, beta_ref, of_ref, ob_ref, *,
                          alpha, n_split):
    chunks = _row_chunks(a_ref.shape[0], n_split)
    wa = a_ref.shape[1]
    gain = gain_ref[...]

    def rms(o, gn):
        return (o * lax.rsqrt(jnp.mean(o * o, axis=-1, keepdims=True) + RMS_EPS) * gn).astype(BF16)

    totals = [_dot(jnp.concatenate([rms(a_ref[c, :], gain[:, :wa]), rms(b_ref[c, :], gain[:, wa:])], axis=1),
                   w_ref[...]) for c in chunks]
    _res_ln_store(chunks, totals, res_ref, g_ref, beta_ref, of_ref, ob_ref, alpha, 1.0)


def _rms_mm_res_ln(o_a, o_b, gain, w, res, g, b, alpha, tm=LN_ROW_TILE, n_split=2):
    M, wa = o_a.shape
    wb = o_b.shape[1]
    N = w.shape[1]
    tm = min(tm, M)
    assert M % tm == 0 and w.shape[0] == wa + wb
    row_block = pl.BlockSpec((tm, N), lambda i: (i, 0))
    vec = pl.BlockSpec((1, N), lambda i: (0, 0))
    return pl.pallas_call(
        functools.partial(_rms_mm_res_ln_kernel, alpha=alpha, n_split=n_split),
        out_shape=[jax.ShapeDtypeStruct((M, N), F32), jax.ShapeDtypeStruct((M, N), BF16)],
        grid=(M // tm,),
        in_specs=[pl.BlockSpec((tm, wa), lambda i: (i, 0)),
                  pl.BlockSpec((tm, wb), lambda i: (i, 0)),
                  pl.BlockSpec((1, wa + wb), lambda i: (0, 0)),
                  pl.BlockSpec((wa + wb, N), lambda i: (0, 0)),
                  row_block, vec, vec],
        out_specs=[row_block, row_block],
        compiler_params=_params(("parallel",)),
        name="rms_mm_res_ln",
    )(o_a, o_b, gain.reshape(1, wa + wb), w, res, g.reshape(1, N), b.reshape(1, N))


def _sb_kernel(q_ref, k_ref, v_ref, o_ref, acc_ref, c_ref, *, blk, heads):
    qi = pl.program_id(1)
    row = lax.broadcasted_iota(jnp.int32, (blk, blk), 0)
    col = lax.broadcasted_iota(jnp.int32, (blk, blk), 1)
    suffix = jnp.where(row > col, 1.0, 0.0).astype(BF16)
    strict = col < row

    cols = [slice(h * HEAD_DIM, (h + 1) * HEAD_DIM) for h in range(heads)]

    def step(kb_far, n_blocks, diagonal):
        ks = pl.multiple_of(kb_far * blk, blk)
        key_rows = [pl.ds(ks + b * blk, blk) for b in range(n_blocks)]
        log_beta, log_1m = {}, {}
        for p in [(b, h) for b in reversed(range(n_blocks)) for h in range(heads)]:
            y = _dot_nt(q_ref[:, cols[p[1]]], k_ref[key_rows[p[0]], cols[p[1]]])
            lb = jnp.minimum(y, 0.0) - jnp.log(1.0 + jnp.exp2(jnp.abs(y) * -LOG2E))
            l1 = lb - y
            if diagonal and p[0] == n_blocks - 1:
                l1 = jnp.where(strict, l1, 0.0)
            log_beta[p] = lb
            log_1m[p] = l1.astype(BF16)
        weights = {}
        for h in range(heads):
            c = c_ref[h]
            for b in reversed(range(n_blocks)):
                p = (b, h)
                log_stay = _dot(log_1m[p], suffix)
                w = jnp.exp(log_beta[p] + (log_stay + c))
                if diagonal and b == n_blocks - 1:
                    w = jnp.where(strict, w, 0.0)
                weights[p] = w.astype(BF16)
                c = c + (log_stay[:, :1] + log_1m[p][:, :1].astype(F32))
            c_ref[h] = c
        for h in range(heads):
            w = jnp.concatenate([weights[(b, h)] for b in range(n_blocks)], axis=1)
            acc_ref[:, cols[h]] += _dot(w, v_ref[pl.ds(ks, n_blocks * blk), cols[h]])

    acc_ref[...] = jnp.zeros_like(acc_ref)
    c_ref[...] = jnp.zeros_like(c_ref)

    @pl.when(qi == 0)
    def _():
        step(0, 1, True)

    @pl.when(qi > 0)
    def _():
        step(qi - 1, 2, True)

    def more(carry):
        done, c_max = carry
        return (done < qi) & (c_max >= SB_UNDERFLOW)

    def body(carry):
        done, _ = carry
        step(qi - 1 - done, 1, False)
        return done + 1, jnp.max(c_ref[...])

    lax.while_loop(more, body, (jnp.minimum(qi, 1), jnp.max(c_ref[...])))
    o_ref[...] = acc_ref[...]


def _sb_attention(qkv, n_heads, q_col, k_col, v_col, blk=SB_BLOCK, heads=SB_HEADS_PER_STEP):
    T = qkv.shape[0]
    blk = min(blk, T)
    width = heads * HEAD_DIM
    assert T % blk == 0 and n_heads % heads == 0
    assert q_col % heads == 0 and k_col % heads == 0 and v_col % heads == 0
    return pl.pallas_call(
        functools.partial(_sb_kernel, blk=blk, heads=heads),
        out_shape=jax.ShapeDtypeStruct((T, n_heads * HEAD_DIM), F32),
        grid=(n_heads // heads, T // blk),
        in_specs=[pl.BlockSpec((blk, width), lambda h, i: (i, q_col // heads + h)),
                  pl.BlockSpec((T, width), lambda h, i: (0, k_col // heads + h)),
                  pl.BlockSpec((T, width), lambda h, i: (0, v_col // heads + h))],
        out_specs=pl.BlockSpec((blk, width), lambda h, i: (i, h)),
        scratch_shapes=[pltpu.VMEM((blk, width), F32), pltpu.VMEM((heads, blk, 1), F32)],
        compiler_params=_params(("parallel", "arbitrary")),
        name="sb_attention",
    )(qkv, qkv, qkv)


def _compress_kernel(x_ref, pos_ref, w1_ref, w2_ref, o_ref):
    n = x_ref.shape[1]
    half = CMP_STRIDE * HEAD_DIM
    x = x_ref[0].astype(BF16)
    w1 = w1_ref[0].astype(BF16)
    a = _dot(x, w1[:half])
    b = _dot(x, w1[half:])
    pos = _dot(pos_ref[0].astype(BF16), w1)
    b_next = pltpu.roll(b, n - 1, 0)
    hid = jax.nn.gelu(a + b_next + pos[0:1])
    out = _dot(hid.astype(BF16), w2_ref[0].astype(BF16))
    valid = lax.broadcasted_iota(jnp.int32, out.shape, 0) < n - 1
    o_ref[0] = jnp.where(valid, out, 0.0).astype(o_ref.dtype)


def _compress(x_tok, pos_emb, w1, w2):
    S, T, _ = x_tok.shape
    n = T // CMP_STRIDE
    width = CMP_STRIDE * HEAD_DIM
    hidden = w1.shape[-1]
    x2 = x_tok.reshape(S, n, width)
    pos_flat = jnp.broadcast_to(pos_emb.reshape(S, 1, 2 * width), (S, SUBLANES, 2 * width))
    return pl.pallas_call(
        _compress_kernel,
        out_shape=jax.ShapeDtypeStruct((S, n, HEAD_DIM), BF16),
        grid=(S,),
        in_specs=[pl.BlockSpec((1, n, width), lambda s: (s, 0, 0)),
                  pl.BlockSpec((1, SUBLANES, 2 * width), lambda s: (s, 0, 0)),
                  pl.BlockSpec((1, 2 * width, hidden), lambda s: (s, 0, 0)),
                  pl.BlockSpec((1, hidden, HEAD_DIM), lambda s: (s, 0, 0))],
        out_specs=pl.BlockSpec((1, n, HEAD_DIM), lambda s: (s, 0, 0)),
        compiler_params=_params(("parallel",)),
        name="nsa_compress",
    )(x2, pos_flat, w1, w2)


def _nsa_kernel(q_ref, kc_ref, vc_ref, ks_ref, vs_ref, kw_ref, vw_ref, gate_ref, c2s_ref, o_ref,
                m_ref, mb_ref, acc_ref, s_ref, *, blk, tk):
    R = NSA_REP
    qi = pl.program_id(1)
    q0 = qi * blk
    n_cmp = kc_ref.shape[1]
    n_sel = c2s_ref.shape[1]
    rows = [slice(r * blk, (r + 1) * blk) for r in range(R)]
    qs = jnp.concatenate([q_ref[:, r * HEAD_DIM:(r + 1) * HEAD_DIM] for r in range(R)], axis=0)
    q_pos = q0 + lax.broadcasted_iota(jnp.int32, (blk, 1), 0)
    q_lane = q0 + lax.broadcasted_iota(jnp.int32, (1, blk), 1)

    def softmax_av(s, bias, v1):
        sc = s + bias
        p = jnp.exp(sc - jnp.max(sc, axis=-1, keepdims=True)).astype(BF16)
        pv = _dot(p, v1)
        return pv[:, :HEAD_DIM] / pv[:, HEAD_DIM:]

    kc = kc_ref[0]
    s_cmp = [_dot_nt(qs[rows[r]], kc) for r in range(R)]
    cmp_end = lax.broadcasted_iota(jnp.int32, (1, n_cmp), 1) * CMP_STRIDE + (CMP_BLOCK - 1)
    bias_c = jnp.where(cmp_end <= q_pos, 0.0, MASK_FILL)
    o_cmp, p_sum = [], None
    for r in range(R):
        sc = s_cmp[r] + bias_c
        p = jnp.exp(sc - jnp.max(sc, axis=-1, keepdims=True))
        inv_sum = 1.0 / jnp.sum(p, axis=-1, keepdims=True)
        o_cmp.append(jnp.where(q_pos >= CMP_BLOCK - 1, _dot(p.astype(BF16), vc_ref[0]) * inv_sum, 0.0))
        p_sum = p * inv_sum if p_sum is None else p_sum + p * inv_sum
    imp = _split_dot(p_sum, c2s_ref[...])

    wk = WINDOW + blk
    ws = pl.multiple_of(jnp.maximum(q0 - WINDOW, 0), blk)
    dist = q_pos - (ws + lax.broadcasted_iota(jnp.int32, (1, wk), 1))
    bias_w = jnp.where((dist >= 0) & (dist < WINDOW), 0.0, MASK_FILL)
    k_w = kw_ref[pl.ds(ws, wk), :]
    s_w = [_dot_nt(qs[rows[r]], k_w) for r in range(R)]

    blk_id = lax.broadcasted_iota(jnp.int32, (n_sel, 1), 0)
    forced = (blk_id == 0) | (blk_id == (q_lane >> SEL_SHIFT))
    valid = blk_id * SEL_BLOCK <= q_lane
    work = jnp.where(forced, jnp.inf, jnp.where(valid, imp.T, -jnp.inf))
    blk_idf = blk_id.astype(F32)
    left = work
    o_win = []
    n_steps = min(SEL_TOPK, n_sel)
    for step in range(n_steps):
        top = jnp.max(left, axis=0, keepdims=True)
        first = jnp.min(jnp.where(left == top, blk_idf, float(n_sel)), axis=0, keepdims=True)
        left = jnp.where(blk_idf == first, -jnp.inf, left)
        while len(o_win) * n_steps < (step + 1) * R:
            o_win.append(softmax_av(s_w[len(o_win)], bias_w, vw_ref[pl.ds(ws, wk), :]))
    sel_bias = jnp.where(left != work, 0.0, MASK_FILL).T.astype(BF16)

    gates = jax.nn.sigmoid(gate_ref[...])

    def gate(r, branch):
        return gates[:, r * N_GATES + branch:r * N_GATES + branch + 1]

    o_ref[...] = jnp.concatenate([gate(r, 0) * o_cmp[r] + gate(r, 2) * o_win[r] for r in range(R)], axis=1)

    tok = lax.broadcasted_iota(jnp.int32, (n_sel, tk), 1)
    sel_row = lax.broadcasted_iota(jnp.int32, (n_sel, tk), 0)
    key_off = lax.broadcasted_iota(jnp.int32, (1, tk), 1)

    def key_rows(kb):
        return pl.ds(pl.multiple_of(kb * tk, tk), tk)

    def block_bias(kb):
        ks = kb * tk
        expand = jnp.where(((ks + tok) >> SEL_SHIFT) == sel_row, 1.0, 0.0).astype(BF16)
        bias = _dot(sel_bias, expand)
        return jnp.where(ks + key_off <= q_pos, bias, MASK_FILL)

    def put_scores(r, raw, bias):
        sc = raw + bias
        s_ref[rows[r], :] = sc
        mb_ref[rows[r], :] = jnp.max(sc, axis=-1, keepdims=True)

    def take_probs(r):
        m_old = m_ref[rows[r], :]
        m_new = jnp.maximum(m_old, mb_ref[rows[r], :])
        m_ref[rows[r], :] = m_new
        return jnp.exp(s_ref[rows[r], :] - m_new).astype(BF16), jnp.exp(m_old - m_new)

    def accumulate(r, p, alpha, v1):
        acc_ref[rows[r], :] = alpha * acc_ref[rows[r], :] + _dot(p, v1)

    m_ref[...] = jnp.full_like(m_ref, MASK_FILL)
    acc_ref[...] = jnp.zeros_like(acc_ref)
    kb_diag = qi // (tk // blk)
    bias = block_bias(0)
    for r in range(R):
        put_scores(r, _dot_nt(qs[rows[r]], ks_ref[key_rows(0), :]), bias)

    def sel_body(kb, carry):
        k_next = ks_ref[key_rows(kb + 1), :]
        bias_next = block_bias(kb + 1)
        v1 = vs_ref[key_rows(kb), :]
        for r in range(R):
            p, alpha = take_probs(r)
            put_scores(r, _dot_nt(qs[rows[r]], k_next), bias_next)
            accumulate(r, p, alpha, v1)
        return carry

    lax.fori_loop(0, kb_diag, sel_body, 0)
    v1 = vs_ref[key_rows(kb_diag), :]
    for r in range(R):
        p, alpha = take_probs(r)
        accumulate(r, p, alpha, v1)
    o_sel = acc_ref[:, :HEAD_DIM] / acc_ref[:, HEAD_DIM:]
    o_ref[...] += jnp.concatenate([gate(r, 1) * o_sel[rows[r]] for r in range(R)], axis=1)


def _nsa_attention(q_tok, v_aug, k_cmp, v_cmp, gates, ks_col, kw_col, vs_col, vw_col, gate_col,
                   blk=NSA_QUERY_BLOCK, tk=NSA_KEY_BLOCK):
    T = q_tok.shape[0]
    G = NSA_KV_GROUPS
    blk = min(blk, T)
    tk = min(tk, T)
    assert T % tk == 0 and tk % blk == 0 and WINDOW % blk == 0 and WINDOW + blk <= T
    n_cmp = T // CMP_STRIDE
    n_sel = T // SEL_BLOCK
    cmp_start = np.arange(n_cmp)[:, None] * CMP_STRIDE
    sel_start = np.arange(n_sel)[None, :] * SEL_BLOCK
    overlap = np.clip(np.minimum(cmp_start + CMP_BLOCK, sel_start + SEL_BLOCK)
                      - np.maximum(cmp_start, sel_start), 0, None)
    c2s = jnp.asarray(overlap.astype(np.float32) / CMP_BLOCK, dtype=BF16)
    width = NSA_REP * HEAD_DIM
    key_spec = lambda col: pl.BlockSpec((T, HEAD_DIM), lambda g, i: (0, col + g))
    val_spec = lambda col: pl.BlockSpec((T, 2 * HEAD_DIM), lambda g, i: (0, col + g))
    cmp_spec = lambda w: pl.BlockSpec((1, n_cmp, w), lambda g, i: (g, 0, 0))
    return pl.pallas_call(
        functools.partial(_nsa_kernel, blk=blk, tk=tk),
        out_shape=jax.ShapeDtypeStruct((T, G * width), F32),
        grid=(G, T // blk),
        in_specs=[pl.BlockSpec((blk, width), lambda g, i: (i, g)),
                  cmp_spec(HEAD_DIM), cmp_spec(HEAD_DIM),
                  key_spec(ks_col), val_spec(vs_col), key_spec(kw_col), val_spec(vw_col),
                  pl.BlockSpec((blk, LANES), lambda g, i: (i, gate_col + g)),
                  pl.BlockSpec((n_cmp, n_sel), lambda g, i: (0, 0))],
        out_specs=pl.BlockSpec((blk, width), lambda g, i: (i, g)),
        scratch_shapes=[pltpu.VMEM((NSA_REP * blk, 1), F32),
                        pltpu.VMEM((NSA_REP * blk, 1), F32),
                        pltpu.VMEM((NSA_REP * blk, 2 * HEAD_DIM), F32),
                        pltpu.VMEM((NSA_REP * blk, tk), F32)],
        compiler_params=_params(("parallel", "arbitrary")),
        name="nsa_attention",
    )(q_tok, k_cmp, v_cmp, q_tok, v_aug, q_tok, v_aug, gates, c2s)


def _cross_kernel(q_ref, kv_ref, o_ref, *, n_heads):
    D = q_ref.shape[1]
    dh = D // n_heads
    scores = [_dot_nt(q_ref[:, h * dh:(h + 1) * dh], kv_ref[:, h * dh:(h + 1) * dh]) for h in range(n_heads)]
    for h, s in enumerate(scores):
        p = jnp.exp(s - jnp.max(s, axis=-1, keepdims=True))
        p = p / jnp.sum(p, axis=-1, keepdims=True)
        o_ref[:, h * dh:(h + 1) * dh] = _dot(p.astype(BF16), kv_ref[:, D + h * dh:D + (h + 1) * dh]).astype(o_ref.dtype)


def _cross_attention(q, kv, n_heads, tm=ROW_TILE):
    T, D = q.shape
    M = kv.shape[0]
    tm = min(tm, T)
    return pl.pallas_call(
        functools.partial(_cross_kernel, n_heads=n_heads),
        out_shape=jax.ShapeDtypeStruct((T, D), BF16),
        grid=(T // tm,),
        in_specs=[pl.BlockSpec((tm, D), lambda i: (i, 0)),
                  pl.BlockSpec((M, 2 * D), lambda i: (0, 0))],
        out_specs=pl.BlockSpec((tm, D), lambda i: (i, 0)),
        compiler_params=_params(("parallel",)),
        name="cross_attention",
    )(q, kv)


def _rope_tables(T):
    half = HEAD_DIM // 2
    inv_freq = ROPE_THETA ** (-np.arange(half, dtype=np.float64) / half)
    ang = np.arange(T, dtype=np.float64)[:, None] * inv_freq[None, :]
    cos, sin = np.cos(ang).astype(np.float32), np.sin(ang).astype(np.float32)
    return jnp.asarray(np.concatenate([cos, cos], axis=1)), jnp.asarray(np.concatenate([-sin, sin], axis=1))


def _pad_to(w, axis, mult):
    pad = -w.shape[axis] % mult
    if not pad:
        return w
    shape = list(w.shape)
    shape[axis] = pad
    return jnp.concatenate([w, jnp.zeros(shape, w.dtype)], axis=axis)


def _ffn(x_f32, x_bf16, w_gate, w_up, w_down, ln_g, ln_b, alpha, emit_bf16):
    d_ff = w_gate.shape[1]
    main = d_ff // FF_TILE * FF_TILE
    assert (d_ff - main) % LANES == 0 and main % (FF_DOWN_STEPS * LANES) == 0
    h = _swiglu_up(x_bf16, w_gate, w_up, FF_TILE, main // FF_TILE)
    wd = w_down.astype(BF16)
    tail = None
    if main < d_ff:
        tail = (_swiglu_up(x_bf16, w_gate[:, main:], w_up[:, main:], d_ff - main, 1), wd[main:])
    return _mm_res_ln(h, wd, x_f32, ln_g, ln_b, alpha, 0.5, emit_bf16, n_k=FF_DOWN_STEPS, tail=tail)


def _mixer(x_f32, x_bf16, w_in, cmp_pos_k, cmp_w1_k, cmp_w2_k, cmp_pos_v, cmp_w1_v, cmp_w2_v,
           mix_norm_g, w_out, ln_g, ln_b, alpha):
    T = x_f32.shape[0]
    G = NSA_KV_GROUPS
    sbw = SB_HEADS * HEAD_DIM
    nqw = NSA_HEADS * HEAD_DIM
    kvw = G * HEAD_DIM
    bounds = np.cumsum([0, sbw, sbw, sbw, nqw, kvw, kvw, kvw, kvw, kvw, kvw, NSA_HEADS * N_GATES])
    w_t = w_in.T
    (w_sbq, w_sbk, w_sbv, w_nq, w_kc, w_vc, w_ks, w_vs, w_kw, w_vw, w_gate) = [
        w_t[bounds[i]:bounds[i + 1]] for i in range(11)]
    per_group = NSA_REP * N_GATES
    w_gate = jnp.concatenate([_pad_to(w_gate[g * per_group:(g + 1) * per_group], 0, LANES)
                              for g in range(G)], axis=0)
    tables = _rope_tables(T)
    cat = lambda ws: jnp.concatenate(ws, axis=0).astype(BF16)
    q_scale = lambda n_q, n_rest: jnp.concatenate([jnp.full((n_q,), HEAD_DIM ** -0.5, F32), jnp.ones((n_rest,), F32)])
    plain = _project(x_bf16, cat([w_sbq, w_sbk, w_sbv, w_vs, w_vw]), BF16, ROW_TILE, (3 * sbw + 2 * kvw) // 2,
                     col_scale=q_scale(sbw, 2 * sbw + 2 * kvw), w_transposed=True)
    roped = _project(x_bf16, cat([w_nq, w_ks, w_kw]), BF16, ROW_TILE, nqw + 2 * kvw,
                     col_scale=q_scale(nqw, 2 * kvw), rope_tables=tables, w_transposed=True)
    cmp_gate = _project(x_bf16, cat([w_kc, w_vc, w_gate]), F32, ROW_TILE, 2 * kvw + G * LANES,
                        rope_tables=tables, rope_cols=kvw, w_transposed=True)

    o_sb = _sb_attention(plain, SB_HEADS, 0, SB_HEADS, 2 * SB_HEADS)

    head = lambda a, g: a[:, g * HEAD_DIM:(g + 1) * HEAD_DIM]
    streams = jnp.stack([head(cmp_gate, g) for g in range(2 * G)])
    rep = lambda a, b: jnp.stack([a] * G + [b] * G)
    cmp = _compress(streams, rep(cmp_pos_k, cmp_pos_v), rep(cmp_w1_k, cmp_w1_v), rep(cmp_w2_k, cmp_w2_v))
    ones = jnp.ones((T, HEAD_DIM), BF16)
    v_aug = jnp.concatenate([a for h in range(2 * G) for a in (head(plain, 3 * SB_HEADS + h), ones)], axis=1)
    o_nsa = _nsa_attention(roped, v_aug, cmp[:G], cmp[G:], cmp_gate,
                           ks_col=NSA_HEADS, kw_col=NSA_HEADS + G, vs_col=0, vw_col=G, gate_col=2 * G)
    return _rms_mm_res_ln(o_sb, o_nsa, mix_norm_g, w_out.astype(BF16), x_f32, ln_g, ln_b, alpha)


def _memory_block(x_f32, x_bf16, mem, w_q, w_k, w_v, w_o, ln_g, ln_b, alpha):
    D = w_q.shape[1]
    q = _project(x_bf16, w_q.astype(BF16), BF16, ROW_TILE, D,
                 col_scale=jnp.full((D,), (D // MEM_HEADS) ** -0.5, F32))
    kv = _project(mem.astype(BF16), jnp.concatenate([w_k, w_v], axis=1).astype(BF16), BF16, ROW_TILE, D)
    o = _cross_attention(q, kv, MEM_HEADS)
    return _mm_res_ln(o, w_o.astype(BF16), x_f32, ln_g, ln_b, alpha, 1.0, True)


def kernel(x, mem, ln1_g, ln1_b, ffn1_gate, ffn1_up, ffn1_down, w_in, cmp_pos_k, cmp_w1_k, cmp_w2_k, cmp_pos_v, cmp_w1_v, cmp_w2_v, mix_norm_g, w_out, ln2_g, ln2_b, mem_wq, mem_wk, mem_wv, mem_wo, ln3_g, ln3_b, ffn2_gate, ffn2_up, ffn2_down, ln4_g, ln4_b):
    n_layers = ffn1_gate.shape[0]
    alpha = (2 * n_layers) ** 0.25
    outs = []
    for bi in range(x.shape[0]):
        xf = x[bi]
        xb = xf.astype(BF16)
        for l in range(n_layers):
            xf, xb = _ffn(xf, xb, ffn1_gate[l], ffn1_up[l], ffn1_down[l], ln1_g[l], ln1_b[l], alpha, True)
            xf, xb = _mixer(xf, xb, w_in[l], cmp_pos_k[l], cmp_w1_k[l], cmp_w2_k[l], cmp_pos_v[l], cmp_w1_v[l],
                            cmp_w2_v[l], mix_norm_g[l], w_out[l], ln2_g[l], ln2_b[l], alpha)
            xf, xb = _memory_block(xf, xb, mem[bi], mem_wq[l], mem_wk[l], mem_wv[l], mem_wo[l], ln3_g[l], ln3_b[l],
                                   alpha)
            xf, xb = _ffn(xf, xb, ffn2_gate[l], ffn2_up[l], ffn2_down[l], ln4_g[l], ln4_b[l], alpha,
                          l + 1 < n_layers)
        outs.append(xf)
    return outs[0][None] if len(outs) == 1 else jnp.stack(outs)
```

```python
import functools

import numpy as np
import jax
import jax.numpy as jnp
from jax import lax
from jax.experimental import pallas as pl
from jax.experimental.pallas import tpu as pltpu

HEAD_DIM = 128
SB_HEADS = 8
NSA_HEADS = 8
NSA_KV_GROUPS = 2
NSA_REP = NSA_HEADS // NSA_KV_GROUPS
N_GATES = 3
CMP_BLOCK = 32
CMP_STRIDE = 16
SEL_BLOCK = 64
SEL_SHIFT = 6
SEL_TOPK = 16
WINDOW = 512
MEM_HEADS = 4
ROPE_THETA = 10000.0
LN_EPS = 1e-5
RMS_EPS = 1e-6
MASK_FILL = -1e30
LOG2E = 1.4426950408889634
SB_UNDERFLOW = -110.0
assert 1 << SEL_SHIFT == SEL_BLOCK

LANES = 128
SUBLANES = 8
VMEM_LIMIT = 56 * 1024 * 1024

ROW_TILE = 1024
LN_ROW_TILE = 512
LN_ROW_GROUP = 2
FF_TILE = 512
FF_DOWN_STEPS = 4
SB_BLOCK = 256
SB_HEADS_PER_STEP = 4
NSA_QUERY_BLOCK = 256
NSA_KEY_BLOCK = 1024

BF16 = jnp.bfloat16
F32 = jnp.float32


def _params(sem):
    return pltpu.CompilerParams(dimension_semantics=sem, vmem_limit_bytes=VMEM_LIMIT)


def _dot(a, b):
    return jnp.dot(a, b, preferred_element_type=F32)


def _dot_nt(a, b):
    return lax.dot_general(a, b, (((1,), (1,)), ((), ())), preferred_element_type=F32)


def _split_dot(a, b):
    hi = a.astype(BF16)
    lo = (a - hi.astype(F32)).astype(BF16)
    return _dot(hi, b) + _dot(lo, b)


def _layer_norm(z, g, b):
    mu = jnp.mean(z, axis=-1, keepdims=True)
    zc = z - mu
    var = jnp.mean(zc * zc, axis=-1, keepdims=True)
    return zc * lax.rsqrt(var + LN_EPS) * g + b


def _proj_kernel(x_ref, w_ref, *rest, scaled, rope_heads, w_transposed):
    rest = list(rest)
    o_ref = rest.pop()
    y = (_dot_nt if w_transposed else _dot)(x_ref[...], w_ref[...])
    if scaled:
        y = y * rest.pop(0)[...]
    if rope_heads:
        cos_ref, sin_ref = rest
        c = cos_ref[...]
        s = sin_ref[...]
        heads = []
        for h in range(rope_heads):
            yh = y[:, h * HEAD_DIM:(h + 1) * HEAD_DIM]
            heads.append(yh * c + pltpu.roll(yh, HEAD_DIM // 2, 1) * s)
        if rope_heads * HEAD_DIM < y.shape[1]:
            heads.append(y[:, rope_heads * HEAD_DIM:])
        y = jnp.concatenate(heads, axis=1) if len(heads) > 1 else heads[0]
    o_ref[...] = y.astype(o_ref.dtype)


def _project(x, w, out_dtype, tm, tn, col_scale=None, rope_tables=None, rope_cols=None, w_transposed=False):
    M, K = x.shape
    N = w.shape[0] if w_transposed else w.shape[1]
    tm = min(tm, M)
    assert M % tm == 0 and N % tn == 0
    rope_heads = 0 if rope_tables is None else (tn if rope_cols is None else rope_cols) // HEAD_DIM
    in_specs = [pl.BlockSpec((tm, K), lambda i, j: (i, 0)),
                pl.BlockSpec((tn, K), lambda i, j: (j, 0)) if w_transposed
                else pl.BlockSpec((K, tn), lambda i, j: (0, j))]
    args = [x, w]
    if col_scale is not None:
        in_specs.append(pl.BlockSpec((1, tn), lambda i, j: (0, j)))
        args.append(col_scale.reshape(1, N))
    if rope_tables is not None:
        in_specs += [pl.BlockSpec((tm, HEAD_DIM), lambda i, j: (i, 0))] * 2
        args += list(rope_tables)
    return pl.pallas_call(
        functools.partial(_proj_kernel, scaled=col_scale is not None, rope_heads=rope_heads,
                          w_transposed=w_transposed),
        out_shape=jax.ShapeDtypeStruct((M, N), out_dtype),
        grid=(M // tm, N // tn),
        in_specs=in_specs,
        out_specs=pl.BlockSpec((tm, tn), lambda i, j: (i, j)),
        compiler_params=_params(("parallel", "arbitrary")),
        name="proj_rope" if rope_tables is not None else "proj",
    )(*args)


def _swiglu_up_kernel(x_ref, wg_ref, wu_ref, o_ref, wg_bf16, wu_bf16):
    @pl.when(pl.program_id(1) == 0)
    def _():
        wg_bf16[...] = wg_ref[...].astype(BF16)
        wu_bf16[...] = wu_ref[...].astype(BF16)

    x = x_ref[...]
    g = _dot(x, wg_bf16[...])
    u = _dot(x, wu_bf16[...])
    o_ref[...] = (jax.nn.silu(g) * u).astype(o_ref.dtype)


def _swiglu_up(x, wg, wu, tn, n_blocks, tm=ROW_TILE):
    M, K = x.shape
    tm = min(tm, M)
    assert M % tm == 0 and n_blocks * tn <= wg.shape[1] and wg.shape == wu.shape
    return pl.pallas_call(
        _swiglu_up_kernel,
        out_shape=jax.ShapeDtypeStruct((M, n_blocks * tn), BF16),
        grid=(n_blocks, M // tm),
        in_specs=[pl.BlockSpec((tm, K), lambda j, i: (i, 0)),
                  pl.BlockSpec((K, tn), lambda j, i: (0, j)),
                  pl.BlockSpec((K, tn), lambda j, i: (0, j))],
        out_specs=pl.BlockSpec((tm, tn), lambda j, i: (i, j)),
        scratch_shapes=[pltpu.VMEM((K, tn), BF16), pltpu.VMEM((K, tn), BF16)],
        compiler_params=_params(("parallel", "arbitrary")),
        name="swiglu_up",
    )(x, wg, wu)


def _row_chunks(tm, n_split):
    return [slice(c * tm // n_split, (c + 1) * tm // n_split) for c in range(n_split)]


def _res_ln_store(chunks, totals, res_ref, g_ref, b_ref, of_ref, ob_ref, alpha, coef):
    for c, y in zip(chunks, totals):
        out = _layer_norm(alpha * res_ref[c, :] + coef * y, g_ref[...], b_ref[...])
        of_ref[c, :] = out
        if ob_ref is not None:
            ob_ref[c, :] = out.astype(BF16)


def _mm_res_ln_kernel(*refs, alpha, coef, emit_bf16, n_k, n_split, has_tail):
    refs = list(refs)
    h_ref, w_ref = refs.pop(0), refs.pop(0)
    ht_ref, wt_ref = (refs.pop(0), refs.pop(0)) if has_tail else (None, None)
    res_ref, g_ref, b_ref, of_ref = refs.pop(0), refs.pop(0), refs.pop(0), refs.pop(0)
    ob_ref = refs.pop(0) if emit_bf16 else None
    acc_ref = refs.pop(0) if n_k > 1 else None
    chunks = _row_chunks(h_ref.shape[0], n_split)

    def partial_products():
        return [_dot(h_ref[c, :], w_ref[...]) for c in chunks]

    def finish(totals):
        if has_tail:
            totals = [y + _dot(ht_ref[c, :], wt_ref[...]) for c, y in zip(chunks, totals)]
        _res_ln_store(chunks, totals, res_ref, g_ref, b_ref, of_ref, ob_ref, alpha, coef)

    if n_k == 1:
        finish(partial_products())
        return
    k = pl.program_id(1)
    r = pl.program_id(2)

    @pl.when(k == 0)
    def _():
        for c, y in zip(chunks, partial_products()):
            acc_ref[r, c, :] = y

    @pl.when((k > 0) & (k < n_k - 1))
    def _():
        for c, y in zip(chunks, partial_products()):
            acc_ref[r, c, :] += y

    @pl.when(k == n_k - 1)
    def _():
        finish([acc_ref[r, c, :] + y for c, y in zip(chunks, partial_products())])


def _mm_res_ln(h, w, res, g, b, alpha, coef, emit_bf16, tm=LN_ROW_TILE, n_k=1, n_split=2, tail=None):
    M, K = h.shape
    N = w.shape[1]
    tm = min(tm, M)
    tk = K // n_k
    group = LN_ROW_GROUP if n_k > 1 else 1
    assert M % (tm * group) == 0 and K % n_k == 0 and K <= w.shape[0] and (n_k == 1 or tk % LANES == 0)
    late_row = lambda i, k, r: (jnp.where(k == n_k - 1, group * i + r, group * i), 0)
    row_block = pl.BlockSpec((tm, N), late_row)
    vec = pl.BlockSpec((1, N), lambda i, k, r: (0, 0))
    in_specs = [pl.BlockSpec((tm, tk), lambda i, k, r: (group * i + r, k)),
                pl.BlockSpec((tk, N), lambda i, k, r: (k, 0))]
    args = [h, w]
    if tail is not None:
        kt = tail[0].shape[1]
        in_specs += [pl.BlockSpec((tm, kt), late_row), pl.BlockSpec((kt, N), lambda i, k, r: (0, 0))]
        args += list(tail)
    out = pl.pallas_call(
        functools.partial(_mm_res_ln_kernel, alpha=alpha, coef=coef, emit_bf16=emit_bf16,
                          n_k=n_k, n_split=n_split, has_tail=tail is not None),
        out_shape=[jax.ShapeDtypeStruct((M, N), F32)] + [jax.ShapeDtypeStruct((M, N), BF16)] * emit_bf16,
        grid=(M // (tm * group), n_k, group),
        in_specs=in_specs + [row_block, vec, vec],
        out_specs=[row_block] * (1 + emit_bf16),
        scratch_shapes=[pltpu.VMEM((group, tm, N), F32)] if n_k > 1 else [],
        compiler_params=_params(("parallel", "arbitrary", "arbitrary")),
        name="mm_res_ln",
    )(*args, res, g.reshape(1, N), b.reshape(1, N))
    return (out[0], out[1]) if emit_bf16 else (out[0], None)


def _rms_mm_res_ln_kernel(a_ref, b_ref, gain_ref, w_ref, res_ref, g_ref, beta_ref, of_ref, ob_ref, *,
                          alpha, n_split):
    chunks = _row_chunks(a_ref.shape[0], n_split)
    wa = a_ref.shape[1]
    gain = gain_ref[...]

    def rms(o, gn):
        return (o * lax.rsqrt(jnp.mean(o * o, axis=-1, keepdims=True) + RMS_EPS) * gn).astype(BF16)

    totals = [_dot(jnp.concatenate([rms(a_ref[c, :], gain[:, :wa]), rms(b_ref[c, :], gain[:, wa:])], axis=1),
                   w_ref[...]) for c in chunks]
    _res_ln_store(chunks, totals, res_ref, g_ref, beta_ref, of_ref, ob_ref, alpha, 1.0)


def _rms_mm_res_ln(o_a, o_b, gain, w, res, g, b, alpha, tm=LN_ROW_TILE, n_split=2):
    M, wa = o_a.shape
    wb = o_b.shape[1]
    N = w.shape[1]
    tm = min(tm, M)
    assert M % tm == 0 and w.shape[0] == wa + wb
    row_block = pl.BlockSpec((tm, N), lambda i: (i, 0))
    vec = pl.BlockSpec((1, N), lambda i: (0, 0))
    return pl.pallas_call(
        functools.partial(_rms_mm_res_ln_kernel, alpha=alpha, n_split=n_split),
        out_shape=[jax.ShapeDtypeStruct((M, N), F32), jax.ShapeDtypeStruct((M, N), BF16)],
        grid=(M // tm,),
        in_specs=[pl.BlockSpec((tm, wa), lambda i: (i, 0)),
                  pl.BlockSpec((tm, wb), lambda i: (i, 0)),
                  pl.BlockSpec((1, wa + wb), lambda i: (0, 0)),
                  pl.BlockSpec((wa + wb, N), lambda i: (0, 0)),
                  row_block, vec, vec],
        out_specs=[row_block, row_block],
        compiler_params=_params(("parallel",)),
        name="rms_mm_res_ln",
    )(o_a, o_b, gain.reshape(1, wa + wb), w, res, g.reshape(1, N), b.reshape(1, N))


def _sb_kernel(q_ref, k_ref, v_ref, o_ref, acc_ref, c_ref, *, blk, heads):
    qi = pl.program_id(1)
    row = lax.broadcasted_iota(jnp.int32, (blk, blk), 0)
    col = lax.broadcasted_iota(jnp.int32, (blk, blk), 1)
    suffix = jnp.where(row > col, 1.0, 0.0).astype(BF16)
    strict = col < row

    cols = [slice(h * HEAD_DIM, (h + 1) * HEAD_DIM) for h in range(heads)]

    def step(kb_far, n_blocks, diagonal):
        ks = pl.multiple_of(kb_far * blk, blk)
        key_rows = [pl.ds(ks + b * blk, blk) for b in range(n_blocks)]
        log_beta, log_1m = {}, {}
        for p in [(b, h) for b in reversed(range(n_blocks)) for h in range(heads)]:
            y = _dot_nt(q_ref[:, cols[p[1]]], k_ref[key_rows[p[0]], cols[p[1]]])
            lb = jnp.minimum(y, 0.0) - jnp.log(1.0 + jnp.exp2(jnp.abs(y) * -LOG2E))
            l1 = lb - y
            if diagonal and p[0] == n_blocks - 1:
                l1 = jnp.where(strict, l1, 0.0)
            log_beta[p] = lb
            log_1m[p] = l1.astype(BF16)
        weights = {}
        for h in range(heads):
            c = c_ref[h]
            for b in reversed(range(n_blocks)):
                p = (b, h)
                log_stay = _dot(log_1m[p], suffix)
                w = jnp.exp(log_beta[p] + (log_stay + c))
                if diagonal and b == n_blocks - 1:
                    w = jnp.where(strict, w, 0.0)
                weights[p] = w.astype(BF16)
                c = c + (log_stay[:, :1] + log_1m[p][:, :1].astype(F32))
            c_ref[h] = c
        for h in range(heads):
            w = jnp.concatenate([weights[(b, h)] for b in range(n_blocks)], axis=1)
            acc_ref[:, cols[h]] += _dot(w, v_ref[pl.ds(ks, n_blocks * blk), cols[h]])

    acc_ref[...] = jnp.zeros_like(acc_ref)
    c_ref[...] = jnp.zeros_like(c_ref)

    @pl.when(qi == 0)
    def _():
        step(0, 1, True)

    @pl.when(qi > 0)
    def _():
        step(qi - 1, 2, True)

    def more(carry):
        done, c_max = carry
        return (done < qi) & (c_max >= SB_UNDERFLOW)

    def body(carry):
        done, _ = carry
        step(qi - 1 - done, 1, False)
        return done + 1, jnp.max(c_ref[...])

    lax.while_loop(more, body, (jnp.minimum(qi, 1), jnp.max(c_ref[...])))
    o_ref[...] = acc_ref[...]


def _sb_attention(qkv, n_heads, q_col, k_col, v_col, blk=SB_BLOCK, heads=SB_HEADS_PER_STEP):
    T = qkv.shape[0]
    blk = min(blk, T)
    width = heads * HEAD_DIM
    assert T % blk == 0 and n_heads % heads == 0
    assert q_col % heads == 0 and k_col % heads == 0 and v_col % heads == 0
    return pl.pallas_call(
        functools.partial(_sb_kernel, blk=blk, heads=heads),
        out_shape=jax.ShapeDtypeStruct((T, n_heads * HEAD_DIM), F32),
        grid=(n_heads // heads, T // blk),
        in_specs=[pl.BlockSpec((blk, width), lambda h, i: (i, q_col // heads + h)),
                  pl.BlockSpec((T, width), lambda h, i: (0, k_col // heads + h)),
                  pl.BlockSpec((T, width), lambda h, i: (0, v_col // heads + h))],
        out_specs=pl.BlockSpec((blk, width), lambda h, i: (i, h)),
        scratch_shapes=[pltpu.VMEM((blk, width), F32), pltpu.VMEM((heads, blk, 1), F32)],
        compiler_params=_params(("parallel", "arbitrary")),
        name="sb_attention",
    )(qkv, qkv, qkv)


def _compress_kernel(x_ref, pos_ref, w1_ref, w2_ref, o_ref):
    n = x_ref.shape[1]
    half = CMP_STRIDE * HEAD_DIM
    x = x_ref[0].astype(BF16)
    w1 = w1_ref[0].astype(BF16)
    a = _dot(x, w1[:half])
    b = _dot(x, w1[half:])
    pos = _dot(pos_ref[0].astype(BF16), w1)
    b_next = pltpu.roll(b, n - 1, 0)
    hid = jax.nn.gelu(a + b_next + pos[0:1])
    out = _dot(hid.astype(BF16), w2_ref[0].astype(BF16))
    valid = lax.broadcasted_iota(jnp.int32, out.shape, 0) < n - 1
    o_ref[0] = jnp.where(valid, out, 0.0).astype(o_ref.dtype)


def _compress(x_tok, pos_emb, w1, w2):
    S, T, _ = x_tok.shape
    n = T // CMP_STRIDE
    width = CMP_STRIDE * HEAD_DIM
    hidden = w1.shape[-1]
    x2 = x_tok.reshape(S, n, width)
    pos_flat = jnp.broadcast_to(pos_emb.reshape(S, 1, 2 * width), (S, SUBLANES, 2 * width))
    return pl.pallas_call(
        _compress_kernel,
        out_shape=jax.ShapeDtypeStruct((S, n, HEAD_DIM), BF16),
        grid=(S,),
        in_specs=[pl.BlockSpec((1, n, width), lambda s: (s, 0, 0)),
                  pl.BlockSpec((1, SUBLANES, 2 * width), lambda s: (s, 0, 0)),
                  pl.BlockSpec((1, 2 * width, hidden), lambda s: (s, 0, 0)),
                  pl.BlockSpec((1, hidden, HEAD_DIM), lambda s: (s, 0, 0))],
        out_specs=pl.BlockSpec((1, n, HEAD_DIM), lambda s: (s, 0, 0)),
        compiler_params=_params(("parallel",)),
        name="nsa_compress",
    )(x2, pos_flat, w1, w2)


def _nsa_kernel(q_ref, kc_ref, vc_ref, ks_ref, vs_ref, kw_ref, vw_ref, gate_ref, c2s_ref, o_ref,
                m_ref, mb_ref, acc_ref, s_ref, *, blk, tk):
    R = NSA_REP
    qi = pl.program_id(1)
    q0 = qi * blk
    n_cmp = kc_ref.shape[1]
    n_sel = c2s_ref.shape[1]
    rows = [slice(r * blk, (r + 1) * blk) for r in range(R)]
    qs = jnp.concatenate([q_ref[:, r * HEAD_DIM:(r + 1) * HEAD_DIM] for r in range(R)], axis=0)
    q_pos = q0 + lax.broadcasted_iota(jnp.int32, (blk, 1), 0)
    q_lane = q0 + lax.broadcasted_iota(jnp.int32, (1, blk), 1)

    def softmax_av(s, bias, v1):
        sc = s + bias
        p = jnp.exp(sc - jnp.max(sc, axis=-1, keepdims=True)).astype(BF16)
        pv = _dot(p, v1)
        return pv[:, :HEAD_DIM] / pv[:, HEAD_DIM:]

    kc = kc_ref[0]
    s_cmp = [_dot_nt(qs[rows[r]], kc) for r in range(R)]
    cmp_end = lax.broadcasted_iota(jnp.int32, (1, n_cmp), 1) * CMP_STRIDE + (CMP_BLOCK - 1)
    bias_c = jnp.where(cmp_end <= q_pos, 0.0, MASK_FILL)
    o_cmp, p_sum = [], None
    for r in range(R):
        sc = s_cmp[r] + bias_c
        p = jnp.exp(sc - jnp.max(sc, axis=-1, keepdims=True))
        inv_sum = 1.0 / jnp.sum(p, axis=-1, keepdims=True)
        o_cmp.append(jnp.where(q_pos >= CMP_BLOCK - 1, _dot(p.astype(BF16), vc_ref[0]) * inv_sum, 0.0))
        p_sum = p * inv_sum if p_sum is None else p_sum + p * inv_sum
    imp = _split_dot(p_sum, c2s_ref[...])

    wk = WINDOW + blk
    ws = pl.multiple_of(jnp.maximum(q0 - WINDOW, 0), blk)
    dist = q_pos - (ws + lax.broadcasted_iota(jnp.int32, (1, wk), 1))
    bias_w = jnp.where((dist >= 0) & (dist < WINDOW), 0.0, MASK_FILL)
    k_w = kw_ref[pl.ds(ws, wk), :]
    s_w = [_dot_nt(qs[rows[r]], k_w) for r in range(R)]

    blk_id = lax.broadcasted_iota(jnp.int32, (n_sel, 1), 0)
    forced = (blk_id == 0) | (blk_id == (q_lane >> SEL_SHIFT))
    valid = blk_id * SEL_BLOCK <= q_lane
    work = jnp.where(forced, jnp.inf, jnp.where(valid, imp.T, -jnp.inf))
    blk_idf = blk_id.astype(F32)
    left = work
    o_win = []
    n_steps = min(SEL_TOPK, n_sel)
    for step in range(n_steps):
        top = jnp.max(left, axis=0, keepdims=True)
        first = jnp.min(jnp.where(left == top, blk_idf, float(n_sel)), axis=0, keepdims=True)
        left = jnp.where(blk_idf == first, -jnp.inf, left)
        while len(o_win) * n_steps < (step + 1) * R:
            o_win.append(softmax_av(s_w[len(o_win)], bias_w, vw_ref[pl.ds(ws, wk), :]))
    sel_bias = jnp.where(left != work, 0.0, MASK_FILL).T.astype(BF16)

    gates = jax.nn.sigmoid(gate_ref[...])

    def gate(r, branch):
        return gates[:, r * N_GATES + branch:r * N_GATES + branch + 1]

    o_ref[...] = jnp.concatenate([gate(r, 0) * o_cmp[r] + gate(r, 2) * o_win[r] for r in range(R)], axis=1)

    tok = lax.broadcasted_iota(jnp.int32, (n_sel, tk), 1)
    sel_row = lax.broadcasted_iota(jnp.int32, (n_sel, tk), 0)
    key_off = lax.broadcasted_iota(jnp.int32, (1, tk), 1)

    def key_rows(kb):
        return pl.ds(pl.multiple_of(kb * tk, tk), tk)

    def block_bias(kb):
        ks = kb * tk
        expand = jnp.where(((ks + tok) >> SEL_SHIFT) == sel_row, 1.0, 0.0).astype(BF16)
        bias = _dot(sel_bias, expand)
        return jnp.where(ks + key_off <= q_pos, bias, MASK_FILL)

    def put_scores(r, raw, bias):
        sc = raw + bias
        s_ref[rows[r], :] = sc
        mb_ref[rows[r], :] = jnp.max(sc, axis=-1, keepdims=True)

    def take_probs(r):
        m_old = m_ref[rows[r], :]
        m_new = jnp.maximum(m_old, mb_ref[rows[r], :])
        m_ref[rows[r], :] = m_new
        return jnp.exp(s_ref[rows[r], :] - m_new).astype(BF16), jnp.exp(m_old - m_new)

    def accumulate(r, p, alpha, v1):
        acc_ref[rows[r], :] = alpha * acc_ref[rows[r], :] + _dot(p, v1)

    m_ref[...] = jnp.full_like(m_ref, MASK_FILL)
    acc_ref[...] = jnp.zeros_like(acc_ref)
    kb_diag = qi // (tk // blk)
    bias = block_bias(0)
    for r in range(R):
        put_scores(r, _dot_nt(qs[rows[r]], ks_ref[key_rows(0), :]), bias)

    def sel_body(kb, carry):
        k_next = ks_ref[key_rows(kb + 1), :]
        bias_next = block_bias(kb + 1)
        v1 = vs_ref[key_rows(kb), :]
        for r in range(R):
            p, alpha = take_probs(r)
            put_scores(r, _dot_nt(qs[rows[r]], k_next), bias_next)
            accumulate(r, p, alpha, v1)
        return carry

    lax.fori_loop(0, kb_diag, sel_body, 0)
    v1 = vs_ref[key_rows(kb_diag), :]
    for r in range(R):
        p, alpha = take_probs(r)
        accumulate(r, p, alpha, v1)
    o_sel = acc_ref[:, :HEAD_DIM] / acc_ref[:, HEAD_DIM:]
    o_ref[...] += jnp.concatenate([gate(r, 1) * o_sel[rows[r]] for r in range(R)], axis=1)


def _nsa_attention(q_tok, v_aug, k_cmp, v_cmp, gates, ks_col, kw_col, vs_col, vw_col, gate_col,
                   blk=NSA_QUERY_BLOCK, tk=NSA_KEY_BLOCK):
    T = q_tok.shape[0]
    G = NSA_KV_GROUPS
    blk = min(blk, T)
    tk = min(tk, T)
    assert T % tk == 0 and tk % blk == 0 and WINDOW % blk == 0 and WINDOW + blk <= T
    n_cmp = T // CMP_STRIDE
    n_sel = T // SEL_BLOCK
    cmp_start = np.arange(n_cmp)[:, None] * CMP_STRIDE
    sel_start = np.arange(n_sel)[None, :] * SEL_BLOCK
    overlap = np.clip(np.minimum(cmp_start + CMP_BLOCK, sel_start + SEL_BLOCK)
                      - np.maximum(cmp_start, sel_start), 0, None)
    c2s = jnp.asarray(overlap.astype(np.float32) / CMP_BLOCK, dtype=BF16)
    width = NSA_REP * HEAD_DIM
    key_spec = lambda col: pl.BlockSpec((T, HEAD_DIM), lambda g, i: (0, col + g))
    val_spec = lambda col: pl.BlockSpec((T, 2 * HEAD_DIM), lambda g, i: (0, col + g))
    cmp_spec = lambda w: pl.BlockSpec((1, n_cmp, w), lambda g, i: (g, 0, 0))
    return pl.pallas_call(
        functools.partial(_nsa_kernel, blk=blk, tk=tk),
        out_shape=jax.ShapeDtypeStruct((T, G * width), F32),
        grid=(G, T // blk),
        in_specs=[pl.BlockSpec((blk, width), lambda g, i: (i, g)),
                  cmp_spec(HEAD_DIM), cmp_spec(HEAD_DIM),
                  key_spec(ks_col), val_spec(vs_col), key_spec(kw_col), val_spec(vw_col),
                  pl.BlockSpec((blk, LANES), lambda g, i: (i, gate_col + g)),
                  pl.BlockSpec((n_cmp, n_sel), lambda g, i: (0, 0))],
        out_specs=pl.BlockSpec((blk, width), lambda g, i: (i, g)),
        scratch_shapes=[pltpu.VMEM((NSA_REP * blk, 1), F32),
                        pltpu.VMEM((NSA_REP * blk, 1), F32),
                        pltpu.VMEM((NSA_REP * blk, 2 * HEAD_DIM), F32),
                        pltpu.VMEM((NSA_REP * blk, tk), F32)],
        compiler_params=_params(("parallel", "arbitrary")),
        name="nsa_attention",
    )(q_tok, k_cmp, v_cmp, q_tok, v_aug, q_tok, v_aug, gates, c2s)


def _cross_kernel(q_ref, kv_ref, o_ref, *, n_heads):
    D = q_ref.shape[1]
    dh = D // n_heads
    scores = [_dot_nt(q_ref[:, h * dh:(h + 1) * dh], kv_ref[:, h * dh:(h + 1) * dh]) for h in range(n_heads)]
    for h, s in enumerate(scores):
        p = jnp.exp(s - jnp.max(s, axis=-1, keepdims=True))
        p = p / jnp.sum(p, axis=-1, keepdims=True)
        o_ref[:, h * dh:(h + 1) * dh] = _dot(p.astype(BF16), kv_ref[:, D + h * dh:D + (h + 1) * dh]).astype(o_ref.dtype)


def _cross_attention(q, kv, n_heads, tm=ROW_TILE):
    T, D = q.shape
    M = kv.shape[0]
    tm = min(tm, T)
    return pl.pallas_call(
        functools.partial(_cross_kernel, n_heads=n_heads),
        out_shape=jax.ShapeDtypeStruct((T, D), BF16),
        grid=(T // tm,),
        in_specs=[pl.BlockSpec((tm, D), lambda i: (i, 0)),
                  pl.BlockSpec((M, 2 * D), lambda i: (0, 0))],
        out_specs=pl.BlockSpec((tm, D), lambda i: (i, 0)),
        compiler_params=_params(("parallel",)),
        name="cross_attention",
    )(q, kv)


def _rope_tables(T):
    half = HEAD_DIM // 2
    inv_freq = ROPE_THETA ** (-np.arange(half, dtype=np.float64) / half)
    ang = np.arange(T, dtype=np.float64)[:, None] * inv_freq[None, :]
    cos, sin = np.cos(ang).astype(np.float32), np.sin(ang).astype(np.float32)
    return jnp.asarray(np.concatenate([cos, cos], axis=1)), jnp.asarray(np.concatenate([-sin, sin], axis=1))


def _pad_to(w, axis, mult):
    pad = -w.shape[axis] % mult
    if not pad:
        return w
    shape = list(w.shape)
    shape[axis] = pad
    return jnp.concatenate([w, jnp.zeros(shape, w.dtype)], axis=axis)


def _ffn(x_f32, x_bf16, w_gate, w_up, w_down, ln_g, ln_b, alpha, emit_bf16):
    d_ff = w_gate.shape[1]
    main = d_ff // FF_TILE * FF_TILE
    assert (d_ff - main) % LANES == 0 and main % (FF_DOWN_STEPS * LANES) == 0
    h = _swiglu_up(x_bf16, w_gate, w_up, FF_TILE, main // FF_TILE)
    wd = w_down.astype(BF16)
    tail = None
    if main < d_ff:
        tail = (_swiglu_up(x_bf16, w_gate[:, main:], w_up[:, main:], d_ff - main, 1), wd[main:])
    return _mm_res_ln(h, wd, x_f32, ln_g, ln_b, alpha, 0.5, emit_bf16, n_k=FF_DOWN_STEPS, tail=tail)


def _mixer(x_f32, x_bf16, w_in, cmp_pos_k, cmp_w1_k, cmp_w2_k, cmp_pos_v, cmp_w1_v, cmp_w2_v,
           mix_norm_g, w_out, ln_g, ln_b, alpha):
    T = x_f32.shape[0]
    G = NSA_KV_GROUPS
    sbw = SB_HEADS * HEAD_DIM
    nqw = NSA_HEADS * HEAD_DIM
    kvw = G * HEAD_DIM
    bounds = np.cumsum([0, sbw, sbw, sbw, nqw, kvw, kvw, kvw, kvw, kvw, kvw, NSA_HEADS * N_GATES])
    w_t = w_in.T
    (w_sbq, w_sbk, w_sbv, w_nq, w_kc, w_vc, w_ks, w_vs, w_kw, w_vw, w_gate) = [
        w_t[bounds[i]:bounds[i + 1]] for i in range(11)]
    per_group = NSA_REP * N_GATES
    w_gate = jnp.concatenate([_pad_to(w_gate[g * per_group:(g + 1) * per_group], 0, LANES)
                              for g in range(G)], axis=0)
    tables = _rope_tables(T)
    cat = lambda ws: jnp.concatenate(ws, axis=0).astype(BF16)
    q_scale = lambda n_q, n_rest: jnp.concatenate([jnp.full((n_q,), HEAD_DIM ** -0.5, F32), jnp.ones((n_rest,), F32)])
    plain = _project(x_bf16, cat([w_sbq, w_sbk, w_sbv, w_vs, w_vw]), BF16, ROW_TILE, (3 * sbw + 2 * kvw) // 2,
                     col_scale=q_scale(sbw, 2 * sbw + 2 * kvw), w_transposed=True)
    roped = _project(x_bf16, cat([w_nq, w_ks, w_kw]), BF16, ROW_TILE, nqw + 2 * kvw,
                     col_scale=q_scale(nqw, 2 * kvw), rope_tables=tables, w_transposed=True)
    cmp_gate = _project(x_bf16, cat([w_kc, w_vc, w_gate]), F32, ROW_TILE, 2 * kvw + G * LANES,
                        rope_tables=tables, rope_cols=kvw, w_transposed=True)

    o_sb = _sb_attention(plain, SB_HEADS, 0, SB_HEADS, 2 * SB_HEADS)

    head = lambda a, g: a[:, g * HEAD_DIM:(g + 1) * HEAD_DIM]
    streams = jnp.stack([head(cmp_gate, g) for g in range(2 * G)])
    rep = lambda a, b: jnp.stack([a] * G + [b] * G)
    cmp = _compress(streams, rep(cmp_pos_k, cmp_pos_v), rep(cmp_w1_k, cmp_w1_v), rep(cmp_w2_k, cmp_w2_v))
    ones = jnp.ones((T, HEAD_DIM), BF16)
    v_aug = jnp.concatenate([a for h in range(2 * G) for a in (head(plain, 3 * SB_HEADS + h), ones)], axis=1)
    o_nsa = _nsa_attention(roped, v_aug, cmp[:G], cmp[G:], cmp_gate,
                           ks_col=NSA_HEADS, kw_col=NSA_HEADS + G, vs_col=0, vw_col=G, gate_col=2 * G)
    return _rms_mm_res_ln(o_sb, o_nsa, mix_norm_g, w_out.astype(BF16), x_f32, ln_g, ln_b, alpha)


def _memory_block(x_f32, x_bf16, mem, w_q, w_k, w_v, w_o, ln_g, ln_b, alpha):
    D = w_q.shape[1]
    q = _project(x_bf16, w_q.astype(BF16), BF16, ROW_TILE, D,
                 col_scale=jnp.full((D,), (D // MEM_HEADS) ** -0.5, F32))
    kv = _project(mem.astype(BF16), jnp.concatenate([w_k, w_v], axis=1).astype(BF16), BF16, ROW_TILE, D)
    o = _cross_attention(q, kv, MEM_HEADS)
    return _mm_res_ln(o, w_o.astype(BF16), x_f32, ln_g, ln_b, alpha, 1.0, True)


def kernel(x, mem, ln1_g, ln1_b, ffn1_gate, ffn1_up, ffn1_down, w_in, cmp_pos_k, cmp_w1_k, cmp_w2_k, cmp_pos_v, cmp_w1_v, cmp_w2_v, mix_norm_g, w_out, ln2_g, ln2_b, mem_wq, mem_wk, mem_wv, mem_wo, ln3_g, ln3_b, ffn2_gate, ffn2_up, ffn2_down, ln4_g, ln4_b):
    n_layers = ffn1_gate.shape[0]
    alpha = (2 * n_layers) ** 0.25
    outs = []
    for bi in range(x.shape[0]):
        xf = x[bi]
        xb = xf.astype(BF16)
        for l in range(n_layers):
            xf, xb = _ffn(xf, xb, ffn1_gate[l], ffn1_up[l], ffn1_down[l], ln1_g[l], ln1_b[l], alpha, True)
            xf, xb = _mixer(xf, xb, w_in[l], cmp_pos_k[l], cmp_w1_k[l], cmp_w2_k[l], cmp_pos_v[l], cmp_w1_v[l],
                            cmp_w2_v[l], mix_norm_g[l], w_out[l], ln2_g[l], ln2_b[l], alpha)
            xf, xb = _memory_block(xf, xb, mem[bi], mem_wq[l], mem_wk[l], mem_wv[l], mem_wo[l], ln3_g[l], ln3_b[l],
                                   alpha)
            xf, xb = _ffn(xf, xb, ffn2_gate[l], ffn2_up[l], ffn2_down[l], ln4_g[l], ln4_b[l], alpha,
                          l + 1 < n_layers)
        outs.append(xf)
    return outs[0][None] if len(outs) == 1 else jnp.stack(outs)
```

```python
import functools

import numpy as np
import jax
import jax.numpy as jnp
from jax import lax
from jax.experimental import pallas as pl
from jax.experimental.pallas import tpu as pltpu

HEAD_DIM = 128
SB_HEADS = 8
NSA_HEADS = 8
NSA_KV_GROUPS = 2
NSA_REP = NSA_HEADS // NSA_KV_GROUPS
N_GATES = 3
CMP_BLOCK = 32
CMP_STRIDE = 16
SEL_BLOCK = 64
SEL_SHIFT = 6
SEL_TOPK = 16
WINDOW = 512
MEM_HEADS = 4
ROPE_THETA = 10000.0
LN_EPS = 1e-5
RMS_EPS = 1e-6
MASK_FILL = -1e30
LOG2E = 1.4426950408889634
SB_UNDERFLOW = -110.0
assert 1 << SEL_SHIFT == SEL_BLOCK

LANES = 128
SUBLANES = 8
VMEM_LIMIT = 56 * 1024 * 1024

ROW_TILE = 1024
LN_ROW_TILE = 512
LN_ROW_GROUP = 2
FF_TILE = 512
FF_DOWN_STEPS = 4
SB_BLOCK = 256
SB_HEADS_PER_STEP = 4
NSA_QUERY_BLOCK = 256
NSA_KEY_BLOCK = 1024

BF16 = jnp.bfloat16
F32 = jnp.float32


def _params(sem):
    return pltpu.CompilerParams(dimension_semantics=sem, vmem_limit_bytes=VMEM_LIMIT)


def _dot(a, b):
    return jnp.dot(a, b, preferred_element_type=F32)


def _dot_nt(a, b):
    return lax.dot_general(a, b, (((1,), (1,)), ((), ())), preferred_element_type=F32)


def _split_dot(a, b):
    hi = a.astype(BF16)
    lo = (a - hi.astype(F32)).astype(BF16)
    return _dot(hi, b) + _dot(lo, b)


def _layer_norm(z, g, b):
    mu = jnp.mean(z, axis=-1, keepdims=True)
    zc = z - mu
    var = jnp.mean(zc * zc, axis=-1, keepdims=True)
    return zc * lax.rsqrt(var + LN_EPS) * g + b


def _proj_kernel(x_ref, w_ref, *rest, scaled, rope_heads, w_transposed):
    rest = list(rest)
    o_ref = rest.pop()
    y = (_dot_nt if w_transposed else _dot)(x_ref[...], w_ref[...])
    if scaled:
        y = y * rest.pop(0)[...]
    if rope_heads:
        cos_ref, sin_ref = rest
        c = cos_ref[...]
        s = sin_ref[...]
        heads = []
        for h in range(rope_heads):
            yh = y[:, h * HEAD_DIM:(h + 1) * HEAD_DIM]
            heads.append(yh * c + pltpu.roll(yh, HEAD_DIM // 2, 1) * s)
        if rope_heads * HEAD_DIM < y.shape[1]:
            heads.append(y[:, rope_heads * HEAD_DIM:])
        y = jnp.concatenate(heads, axis=1) if len(heads) > 1 else heads[0]
    o_ref[...] = y.astype(o_ref.dtype)


def _project(x, w, out_dtype, tm, tn, col_scale=None, rope_tables=None, rope_cols=None, w_transposed=False,
             n_cols=None):
    M, K = x.shape
    N = n_cols if n_cols is not None else (w.shape[0] if w_transposed else w.shape[1])
    tm = min(tm, M)
    assert M % tm == 0 and N % tn == 0
    rope_heads = 0 if rope_tables is None else (tn if rope_cols is None else rope_cols) // HEAD_DIM
    in_specs = [pl.BlockSpec((tm, K), lambda i, j: (i, 0)),
                pl.BlockSpec((tn, K), lambda i, j: (j, 0)) if w_transposed
                else pl.BlockSpec((K, tn), lambda i, j: (0, j))]
    args = [x, w]
    if col_scale is not None:
        in_specs.append(pl.BlockSpec((1, tn), lambda i, j: (0, j)))
        args.append(col_scale.reshape(1, N))
    if rope_tables is not None:
        in_specs += [pl.BlockSpec((tm, HEAD_DIM), lambda i, j: (i, 0))] * 2
        args += list(rope_tables)
    return pl.pallas_call(
        functools.partial(_proj_kernel, scaled=col_scale is not None, rope_heads=rope_heads,
                          w_transposed=w_transposed),
        out_shape=jax.ShapeDtypeStruct((M, N), out_dtype),
        grid=(M // tm, N // tn),
        in_specs=in_specs,
        out_specs=pl.BlockSpec((tm, tn), lambda i, j: (i, j)),
        compiler_params=_params(("parallel", "arbitrary")),
        name="proj_rope" if rope_tables is not None else "proj",
    )(*args)


def _swiglu_up_kernel(x_ref, wg_ref, wu_ref, o_ref, wg_bf16, wu_bf16):
    @pl.when(pl.program_id(1) == 0)
    def _():
        wg_bf16[...] = wg_ref[...].astype(BF16)
        wu_bf16[...] = wu_ref[...].astype(BF16)

    x = x_ref[...]
    if x.dtype != BF16:
        x = x.astype(BF16)
    g = _dot(x, wg_bf16[...])
    u = _dot(x, wu_bf16[...])
    o_ref[...] = (jax.nn.silu(g) * u).astype(o_ref.dtype)


def _swiglu_up(x, wg, wu, tn, n_blocks, tm=ROW_TILE):
    M, K = x.shape
    tm = min(tm, M)
    assert M % tm == 0 and n_blocks * tn <= wg.shape[1] and wg.shape == wu.shape
    return pl.pallas_call(
        _swiglu_up_kernel,
        out_shape=jax.ShapeDtypeStruct((M, n_blocks * tn), BF16),
        grid=(n_blocks, M // tm),
        in_specs=[pl.BlockSpec((tm, K), lambda j, i: (i, 0)),
                  pl.BlockSpec((K, tn), lambda j, i: (0, j)),
                  pl.BlockSpec((K, tn), lambda j, i: (0, j))],
        out_specs=pl.BlockSpec((tm, tn), lambda j, i: (i, j)),
        scratch_shapes=[pltpu.VMEM((K, tn), BF16), pltpu.VMEM((K, tn), BF16)],
        compiler_params=_params(("parallel", "arbitrary")),
        name="swiglu_up",
    )(x, wg, wu)


def _row_chunks(tm, n_split):
    return [slice(c * tm // n_split, (c + 1) * tm // n_split) for c in range(n_split)]


def _res_ln_store(chunks, totals, res_ref, g_ref, b_ref, of_ref, ob_ref, alpha, coef):
    for c, y in zip(chunks, totals):
        out = _layer_norm(alpha * res_ref[c, :] + coef * y, g_ref[...], b_ref[...])
        of_ref[c, :] = out
        if ob_ref is not None:
            ob_ref[c, :] = out.astype(BF16)


def _mm_res_ln_kernel(*refs, alpha, coef, emit_bf16, n_k, n_split, has_tail):
    refs = list(refs)
    h_ref, w_ref = refs.pop(0), refs.pop(0)
    ht_ref, wt_ref = (refs.pop(0), refs.pop(0)) if has_tail else (None, None)
    res_ref, g_ref, b_ref, of_ref = refs.pop(0), refs.pop(0), refs.pop(0), refs.pop(0)
    ob_ref = refs.pop(0) if emit_bf16 else None
    acc_ref = refs.pop(0) if n_k > 1 else None
    chunks = _row_chunks(h_ref.shape[0], n_split)

    def partial_products():
        return [_dot(h_ref[c, :], w_ref[...]) for c in chunks]

    def finish(totals):
        if has_tail:
            totals = [y + _dot(ht_ref[c, :], wt_ref[...]) for c, y in zip(chunks, totals)]
        _res_ln_store(chunks, totals, res_ref, g_ref, b_ref, of_ref, ob_ref, alpha, coef)

    if n_k == 1:
        finish(partial_products())
        return
    k = pl.program_id(1)
    r = pl.program_id(2)

    @pl.when(k == 0)
    def _():
        for c, y in zip(chunks, partial_products()):
            acc_ref[r, c, :] = y

    @pl.when((k > 0) & (k < n_k - 1))
    def _():
        for c, y in zip(chunks, partial_products()):
            acc_ref[r, c, :] += y

    @pl.when(k == n_k - 1)
    def _():
        finish([acc_ref[r, c, :] + y for c, y in zip(chunks, partial_products())])


def _mm_res_ln(h, w, res, g, b, alpha, coef, emit_bf16, tm=LN_ROW_TILE, n_k=1, n_split=2, tail=None):
    M, K = h.shape
    N = w.shape[1]
    tm = min(tm, M)
    tk = K // n_k
    group = LN_ROW_GROUP if n_k > 1 else 1
    assert M % (tm * group) == 0 and K % n_k == 0 and K <= w.shape[0] and (n_k == 1 or tk % LANES == 0)
    late_row = lambda i, k, r: (jnp.where(k == n_k - 1, group * i + r, group * i), 0)
    row_block = pl.BlockSpec((tm, N), late_row)
    vec = pl.BlockSpec((1, N), lambda i, k, r: (0, 0))
    in_specs = [pl.BlockSpec((tm, tk), lambda i, k, r: (group * i + r, k)),
                pl.BlockSpec((tk, N), lambda i, k, r: (k, 0))]
    args = [h, w]
    if tail is not None:
        kt = tail[0].shape[1]
        in_specs += [pl.BlockSpec((tm, kt), late_row), pl.BlockSpec((kt, N), lambda i, k, r: (0, 0))]
        args += list(tail)
    out = pl.pallas_call(
        functools.partial(_mm_res_ln_kernel, alpha=alpha, coef=coef, emit_bf16=emit_bf16,
                          n_k=n_k, n_split=n_split, has_tail=tail is not None),
        out_shape=[jax.ShapeDtypeStruct((M, N), F32)] + [jax.ShapeDtypeStruct((M, N), BF16)] * emit_bf16,
        grid=(M // (tm * group), n_k, group),
        in_specs=in_specs + [row_block, vec, vec],
        out_specs=[row_block] * (1 + emit_bf16),
        scratch_shapes=[pltpu.VMEM((group, tm, N), F32)] if n_k > 1 else [],
        compiler_params=_params(("parallel", "arbitrary", "arbitrary")),
        name="mm_res_ln",
    )(*args, res, g.reshape(1, N), b.reshape(1, N))
    return (out[0], out[1]) if emit_bf16 else (out[0], None)


def _rms_mm_res_ln_kernel(a_ref, b_ref, gain_ref, w_ref, res_ref, g_ref, beta_ref, of_ref, ob_ref, *,
                          alpha, n_split):
    chunks = _row_chunks(a_ref.shape[0], n_split)
    wa = a_ref.shape[1]
    gain = gain_ref[...]

    def rms(o, gn):
        return (o * lax.rsqrt(jnp.mean(o * o, axis=-1, keepdims=True) + RMS_EPS) * gn).astype(BF16)

    totals = [_dot(jnp.concatenate([rms(a_ref[c, :], gain[:, :wa]), rms(b_ref[c, :], gain[:, wa:])], axis=1),
                   w_ref[...]) for c in chunks]
    _res_ln_store(chunks, totals, res_ref, g_ref, beta_ref, of_ref, ob_ref, alpha, 1.0)


def _rms_mm_res_ln(o_a, o_b, gain, w, res, g, b, alpha, tm=LN_ROW_TILE, n_split=2):
    M, wa = o_a.shape
    wb = o_b.shape[1]
    N = w.shape[1]
    tm = min(tm, M)
    assert M % tm == 0 and w.shape[0] == wa + wb
    row_block = pl.BlockSpec((tm, N), lambda i: (i, 0))
    vec = pl.BlockSpec((1, N), lambda i: (0, 0))
    return pl.pallas_call(
        functools.partial(_rms_mm_res_ln_kernel, alpha=alpha, n_split=n_split),
        out_shape=[jax.ShapeDtypeStruct((M, N), F32), jax.ShapeDtypeStruct((M, N), BF16)],
        grid=(M // tm,),
        in_specs=[pl.BlockSpec((tm, wa), lambda i: (i, 0)),
                  pl.BlockSpec((tm, wb), lambda i: (i, 0)),
                  pl.BlockSpec((1, wa + wb), lambda i: (0, 0)),
                  pl.BlockSpec((wa + wb, N), lambda i: (0, 0)),
                  row_block, vec, vec],
        out_specs=[row_block, row_block],
        compiler_params=_params(("parallel",)),
        name="rms_mm_res_ln",
    )(o_a, o_b, gain.reshape(1, wa + wb), w, res, g.reshape(1, N), b.reshape(1, N))


def _sb_kernel(q_ref, k_ref, v_ref, o_ref, acc_ref, c_ref, *, blk, heads):
    qi = pl.program_id(1)
    row = lax.broadcasted_iota(jnp.int32, (blk, blk), 0)
    col = lax.broadcasted_iota(jnp.int32, (blk, blk), 1)
    suffix = jnp.where(row > col, 1.0, 0.0).astype(BF16)
    strict = col < row

    cols = [slice(h * HEAD_DIM, (h + 1) * HEAD_DIM) for h in range(heads)]

    def step(kb_far, n_blocks, diagonal):
        ks = pl.multiple_of(kb_far * blk, blk)
        key_rows = [pl.ds(ks + b * blk, blk) for b in range(n_blocks)]
        log_beta, log_1m = {}, {}
        for p in [(b, h) for b in reversed(range(n_blocks)) for h in range(heads)]:
            y = _dot_nt(q_ref[:, cols[p[1]]], k_ref[key_rows[p[0]], cols[p[1]]])
            lb = jnp.minimum(y, 0.0) - jnp.log(1.0 + jnp.exp2(jnp.abs(y) * -LOG2E))
            l1 = lb - y
            if diagonal and p[0] == n_blocks - 1:
                l1 = jnp.where(strict, l1, 0.0)
            log_beta[p] = lb
            log_1m[p] = l1.astype(BF16)
        weights = {}
        for h in range(heads):
            c = c_ref[h]
            for b in reversed(range(n_blocks)):
                p = (b, h)
                log_stay = _dot(log_1m[p], suffix)
                w = jnp.exp(log_beta[p] + (log_stay + c))
                if diagonal and b == n_blocks - 1:
                    w = jnp.where(strict, w, 0.0)
                weights[p] = w.astype(BF16)
                c = c + (log_stay[:, :1] + log_1m[p][:, :1].astype(F32))
            c_ref[h] = c
        for h in range(heads):
            w = jnp.concatenate([weights[(b, h)] for b in range(n_blocks)], axis=1)
            acc_ref[:, cols[h]] += _dot(w, v_ref[pl.ds(ks, n_blocks * blk), cols[h]])

    acc_ref[...] = jnp.zeros_like(acc_ref)
    c_ref[...] = jnp.zeros_like(c_ref)

    @pl.when(qi == 0)
    def _():
        step(0, 1, True)

    @pl.when(qi > 0)
    def _():
        step(qi - 1, 2, True)

    def more(carry):
        done, c_max = carry
        return (done < qi) & (c_max >= SB_UNDERFLOW)

    def body(carry):
        done, _ = carry
        step(qi - 1 - done, 1, False)
        return done + 1, jnp.max(c_ref[...])

    lax.while_loop(more, body, (jnp.minimum(qi, 1), jnp.max(c_ref[...])))
    o_ref[...] = acc_ref[...]


def _sb_attention(qkv, n_heads, q_col, k_col, v_col, blk=SB_BLOCK, heads=SB_HEADS_PER_STEP):
    T = qkv.shape[0]
    blk = min(blk, T)
    width = heads * HEAD_DIM
    assert T % blk == 0 and n_heads % heads == 0
    assert q_col % heads == 0 and k_col % heads == 0 and v_col % heads == 0
    return pl.pallas_call(
        functools.partial(_sb_kernel, blk=blk, heads=heads),
        out_shape=jax.ShapeDtypeStruct((T, n_heads * HEAD_DIM), F32),
        grid=(n_heads // heads, T // blk),
        in_specs=[pl.BlockSpec((blk, width), lambda h, i: (i, q_col // heads + h)),
                  pl.BlockSpec((T, width), lambda h, i: (0, k_col // heads + h)),
                  pl.BlockSpec((T, width), lambda h, i: (0, v_col // heads + h))],
        out_specs=pl.BlockSpec((blk, width), lambda h, i: (i, h)),
        scratch_shapes=[pltpu.VMEM((blk, width), F32), pltpu.VMEM((heads, blk, 1), F32)],
        compiler_params=_params(("parallel", "arbitrary")),
        name="sb_attention",
    )(qkv, qkv, qkv)


def _compress_kernel(x_ref, pos_ref, w1_ref, w2_ref, o_ref):
    n = x_ref.shape[1]
    half = CMP_STRIDE * HEAD_DIM
    x = x_ref[0].astype(BF16)
    w1 = w1_ref[0].astype(BF16)
    a = _dot(x, w1[:half])
    b = _dot(x, w1[half:])
    pos = _dot(pos_ref[0].astype(BF16), w1)
    b_next = pltpu.roll(b, n - 1, 0)
    hid = jax.nn.gelu(a + b_next + pos[0:1])
    out = _dot(hid.astype(BF16), w2_ref[0].astype(BF16))
    valid = lax.broadcasted_iota(jnp.int32, out.shape, 0) < n - 1
    o_ref[0] = jnp.where(valid, out, 0.0).astype(o_ref.dtype)


def _compress(x_tok, pos_emb, w1, w2):
    S, T, _ = x_tok.shape
    n = T // CMP_STRIDE
    width = CMP_STRIDE * HEAD_DIM
    hidden = w1.shape[-1]
    x2 = x_tok.reshape(S, n, width)
    pos_flat = jnp.broadcast_to(pos_emb.reshape(S, 1, 2 * width), (S, SUBLANES, 2 * width))
    return pl.pallas_call(
        _compress_kernel,
        out_shape=jax.ShapeDtypeStruct((S, n, HEAD_DIM), BF16),
        grid=(S,),
        in_specs=[pl.BlockSpec((1, n, width), lambda s: (s, 0, 0)),
                  pl.BlockSpec((1, SUBLANES, 2 * width), lambda s: (s, 0, 0)),
                  pl.BlockSpec((1, 2 * width, hidden), lambda s: (s, 0, 0)),
                  pl.BlockSpec((1, hidden, HEAD_DIM), lambda s: (s, 0, 0))],
        out_specs=pl.BlockSpec((1, n, HEAD_DIM), lambda s: (s, 0, 0)),
        compiler_params=_params(("parallel",)),
        name="nsa_compress",
    )(x2, pos_flat, w1, w2)


def _nsa_kernel(q_ref, kc_ref, vc_ref, ks_ref, vs_ref, kw_ref, vw_ref, gate_ref, c2s_ref, o_ref,
                m_ref, mb_ref, acc_ref, s_ref, *, blk, tk):
    R = NSA_REP
    qi = pl.program_id(1)
    q0 = qi * blk
    n_cmp = kc_ref.shape[1]
    n_sel = c2s_ref.shape[1]
    rows = [slice(r * blk, (r + 1) * blk) for r in range(R)]
    qs = jnp.concatenate([q_ref[:, r * HEAD_DIM:(r + 1) * HEAD_DIM] for r in range(R)], axis=0)
    q_pos = q0 + lax.broadcasted_iota(jnp.int32, (blk, 1), 0)
    q_lane = q0 + lax.broadcasted_iota(jnp.int32, (1, blk), 1)

    def softmax_av(s, bias, v1):
        sc = s + bias
        p = jnp.exp(sc - jnp.max(sc, axis=-1, keepdims=True)).astype(BF16)
        pv = _dot(p, v1)
        return pv[:, :HEAD_DIM] / pv[:, HEAD_DIM:]

    kc = kc_ref[0]
    s_cmp = [_dot_nt(qs[rows[r]], kc) for r in range(R)]
    cmp_end = lax.broadcasted_iota(jnp.int32, (1, n_cmp), 1) * CMP_STRIDE + (CMP_BLOCK - 1)
    bias_c = jnp.where(cmp_end <= q_pos, 0.0, MASK_FILL)
    o_cmp, p_sum = [], None
    for r in range(R):
        sc = s_cmp[r] + bias_c
        p = jnp.exp(sc - jnp.max(sc, axis=-1, keepdims=True))
        inv_sum = 1.0 / jnp.sum(p, axis=-1, keepdims=True)
        o_cmp.append(jnp.where(q_pos >= CMP_BLOCK - 1, _dot(p.astype(BF16), vc_ref[0]) * inv_sum, 0.0))
        p_sum = p * inv_sum if p_sum is None else p_sum + p * inv_sum
    imp = _split_dot(p_sum, c2s_ref[...])

    wk = WINDOW + blk
    ws = pl.multiple_of(jnp.maximum(q0 - WINDOW, 0), blk)
    dist = q_pos - (ws + lax.broadcasted_iota(jnp.int32, (1, wk), 1))
    bias_w = jnp.where((dist >= 0) & (dist < WINDOW), 0.0, MASK_FILL)
    k_w = kw_ref[pl.ds(ws, wk), :]
    s_w = [_dot_nt(qs[rows[r]], k_w) for r in range(R)]

    blk_id = lax.broadcasted_iota(jnp.int32, (n_sel, 1), 0)
    forced = (blk_id == 0) | (blk_id == (q_lane >> SEL_SHIFT))
    valid = blk_id * SEL_BLOCK <= q_lane
    work = jnp.where(forced, jnp.inf, jnp.where(valid, imp.T, -jnp.inf))
    blk_idf = blk_id.astype(F32)
    left = work
    o_win = []
    n_steps = min(SEL_TOPK, n_sel)
    for step in range(n_steps):
        top = jnp.max(left, axis=0, keepdims=True)
        first = jnp.min(jnp.where(left == top, blk_idf, float(n_sel)), axis=0, keepdims=True)
        left = jnp.where(blk_idf == first, -jnp.inf, left)
        while len(o_win) * n_steps < (step + 1) * R:
            o_win.append(softmax_av(s_w[len(o_win)], bias_w, vw_ref[pl.ds(ws, wk), :]))
    sel_bias = jnp.where(left != work, 0.0, MASK_FILL).T.astype(BF16)

    gates = jax.nn.sigmoid(gate_ref[...])

    def gate(r, branch):
        return gates[:, r * N_GATES + branch:r * N_GATES + branch + 1]

    o_ref[...] = jnp.concatenate([gate(r, 0) * o_cmp[r] + gate(r, 2) * o_win[r] for r in range(R)], axis=1)

    tok = lax.broadcasted_iota(jnp.int32, (n_sel, tk), 1)
    sel_row = lax.broadcasted_iota(jnp.int32, (n_sel, tk), 0)
    key_off = lax.broadcasted_iota(jnp.int32, (1, tk), 1)

    def key_rows(kb):
        return pl.ds(pl.multiple_of(kb * tk, tk), tk)

    def block_bias(kb):
        ks = kb * tk
        expand = jnp.where(((ks + tok) >> SEL_SHIFT) == sel_row, 1.0, 0.0).astype(BF16)
        bias = _dot(sel_bias, expand)
        return jnp.where(ks + key_off <= q_pos, bias, MASK_FILL)

    def put_scores(r, raw, bias):
        sc = raw + bias
        s_ref[rows[r], :] = sc
        mb_ref[rows[r], :] = jnp.max(sc, axis=-1, keepdims=True)

    def take_probs(r):
        m_old = m_ref[rows[r], :]
        m_new = jnp.maximum(m_old, mb_ref[rows[r], :])
        m_ref[rows[r], :] = m_new
        return jnp.exp(s_ref[rows[r], :] - m_new).astype(BF16), jnp.exp(m_old - m_new)

    def accumulate(r, p, alpha, v1):
        acc_ref[rows[r], :] = alpha * acc_ref[rows[r], :] + _dot(p, v1)

    m_ref[...] = jnp.full_like(m_ref, MASK_FILL)
    acc_ref[...] = jnp.zeros_like(acc_ref)
    kb_diag = qi // (tk // blk)
    bias = block_bias(0)
    for r in range(R):
        put_scores(r, _dot_nt(qs[rows[r]], ks_ref[key_rows(0), :]), bias)

    def sel_body(kb, carry):
        k_next = ks_ref[key_rows(kb + 1), :]
        bias_next = block_bias(kb + 1)
        v1 = vs_ref[key_rows(kb), :]
        for r in range(R):
            p, alpha = take_probs(r)
            put_scores(r, _dot_nt(qs[rows[r]], k_next), bias_next)
            accumulate(r, p, alpha, v1)
        return carry

    lax.fori_loop(0, kb_diag, sel_body, 0)
    v1 = vs_ref[key_rows(kb_diag), :]
    for r in range(R):
        p, alpha = take_probs(r)
        accumulate(r, p, alpha, v1)
    o_sel = acc_ref[:, :HEAD_DIM] / acc_ref[:, HEAD_DIM:]
    o_ref[...] += jnp.concatenate([gate(r, 1) * o_sel[rows[r]] for r in range(R)], axis=1)


def _nsa_attention(q_tok, v_aug, k_cmp, v_cmp, gates, ks_col, kw_col, vs_col, vw_col, gate_col,
                   blk=NSA_QUERY_BLOCK, tk=NSA_KEY_BLOCK):
    T = q_tok.shape[0]
    G = NSA_KV_GROUPS
    blk = min(blk, T)
    tk = min(tk, T)
    assert T % tk == 0 and tk % blk == 0 and WINDOW % blk == 0 and WINDOW + blk <= T
    n_cmp = T // CMP_STRIDE
    n_sel = T // SEL_BLOCK
    cmp_start = np.arange(n_cmp)[:, None] * CMP_STRIDE
    sel_start = np.arange(n_sel)[None, :] * SEL_BLOCK
    overlap = np.clip(np.minimum(cmp_start + CMP_BLOCK, sel_start + SEL_BLOCK)
                      - np.maximum(cmp_start, sel_start), 0, None)
    c2s = jnp.asarray(overlap.astype(np.float32) / CMP_BLOCK, dtype=BF16)
    width = NSA_REP * HEAD_DIM
    key_spec = lambda col: pl.BlockSpec((T, HEAD_DIM), lambda g, i: (0, col + g))
    val_spec = lambda col: pl.BlockSpec((T, 2 * HEAD_DIM), lambda g, i: (0, col + g))
    cmp_spec = lambda w: pl.BlockSpec((1, n_cmp, w), lambda g, i: (g, 0, 0))
    return pl.pallas_call(
        functools.partial(_nsa_kernel, blk=blk, tk=tk),
        out_shape=jax.ShapeDtypeStruct((T, G * width), F32),
        grid=(G, T // blk),
        in_specs=[pl.BlockSpec((blk, width), lambda g, i: (i, g)),
                  cmp_spec(HEAD_DIM), cmp_spec(HEAD_DIM),
                  key_spec(ks_col), val_spec(vs_col), key_spec(kw_col), val_spec(vw_col),
                  pl.BlockSpec((blk, LANES), lambda g, i: (i, gate_col + g)),
                  pl.BlockSpec((n_cmp, n_sel), lambda g, i: (0, 0))],
        out_specs=pl.BlockSpec((blk, width), lambda g, i: (i, g)),
        scratch_shapes=[pltpu.VMEM((NSA_REP * blk, 1), F32),
                        pltpu.VMEM((NSA_REP * blk, 1), F32),
                        pltpu.VMEM((NSA_REP * blk, 2 * HEAD_DIM), F32),
                        pltpu.VMEM((NSA_REP * blk, tk), F32)],
        compiler_params=_params(("parallel", "arbitrary")),
        name="nsa_attention",
    )(q_tok, k_cmp, v_cmp, q_tok, v_aug, q_tok, v_aug, gates, c2s)


def _cross_kernel(q_ref, kv_ref, o_ref, *, n_heads):
    D = q_ref.shape[1]
    dh = D // n_heads
    scores = [_dot_nt(q_ref[:, h * dh:(h + 1) * dh], kv_ref[:, h * dh:(h + 1) * dh]) for h in range(n_heads)]
    for h, s in enumerate(scores):
        p = jnp.exp(s - jnp.max(s, axis=-1, keepdims=True))
        p = p / jnp.sum(p, axis=-1, keepdims=True)
        o_ref[:, h * dh:(h + 1) * dh] = _dot(p.astype(BF16), kv_ref[:, D + h * dh:D + (h + 1) * dh]).astype(o_ref.dtype)


def _cross_attention(q, kv, n_heads, tm=ROW_TILE):
    T, D = q.shape
    M = kv.shape[0]
    tm = min(tm, T)
    return pl.pallas_call(
        functools.partial(_cross_kernel, n_heads=n_heads),
        out_shape=jax.ShapeDtypeStruct((T, D), BF16),
        grid=(T // tm,),
        in_specs=[pl.BlockSpec((tm, D), lambda i: (i, 0)),
                  pl.BlockSpec((M, 2 * D), lambda i: (0, 0))],
        out_specs=pl.BlockSpec((tm, D), lambda i: (i, 0)),
        compiler_params=_params(("parallel",)),
        name="cross_attention",
    )(q, kv)


def _rope_tables(T):
    half = HEAD_DIM // 2
    inv_freq = ROPE_THETA ** (-np.arange(half, dtype=np.float64) / half)
    ang = np.arange(T, dtype=np.float64)[:, None] * inv_freq[None, :]
    cos, sin = np.cos(ang).astype(np.float32), np.sin(ang).astype(np.float32)
    return jnp.asarray(np.concatenate([cos, cos], axis=1)), jnp.asarray(np.concatenate([-sin, sin], axis=1))


def _pad_to(w, axis, mult):
    pad = -w.shape[axis] % mult
    if not pad:
        return w
    shape = list(w.shape)
    shape[axis] = pad
    return jnp.concatenate([w, jnp.zeros(shape, w.dtype)], axis=axis)


def _ffn(x_f32, x_mm, w_gate, w_up, w_down, ln_g, ln_b, alpha, emit_bf16):
    d_ff = w_gate.shape[1]
    main = d_ff // FF_TILE * FF_TILE
    assert (d_ff - main) % LANES == 0 and main % (FF_DOWN_STEPS * LANES) == 0
    h = _swiglu_up(x_mm, w_gate, w_up, FF_TILE, main // FF_TILE)
    wd = w_down.astype(BF16)
    tail = None
    if main < d_ff:
        tail = (_swiglu_up(x_mm, w_gate[:, main:], w_up[:, main:], d_ff - main, 1), wd[main:])
    return _mm_res_ln(h, wd, x_f32, ln_g, ln_b, alpha, 0.5, emit_bf16, n_k=FF_DOWN_STEPS, tail=tail)


def _mixer(x_f32, x_bf16, w_in, cmp_pos_k, cmp_w1_k, cmp_w2_k, cmp_pos_v, cmp_w1_v, cmp_w2_v,
           mix_norm_g, w_out, ln_g, ln_b, alpha):
    T = x_f32.shape[0]
    G = NSA_KV_GROUPS
    sbw = SB_HEADS * HEAD_DIM
    nqw = NSA_HEADS * HEAD_DIM
    kvw = G * HEAD_DIM
    bounds = np.cumsum([0, sbw, sbw, sbw, nqw, kvw, kvw, kvw, kvw, kvw, kvw, NSA_HEADS * N_GATES])
    w_t = w_in.T.astype(BF16)
    (w_sbq, w_sbk, w_sbv, w_nq, w_kc, w_vc, w_ks, w_vs, w_kw, w_vw, w_gate) = [
        w_t[bounds[i]:bounds[i + 1]] for i in range(11)]
    per_group = NSA_REP * N_GATES
    w_gate = jnp.concatenate([_pad_to(w_gate[g * per_group:(g + 1) * per_group], 0, LANES)
                              for g in range(G)], axis=0)
    tables = _rope_tables(T)
    cat = lambda ws: jnp.concatenate(ws, axis=0)
    q_scale = lambda n_q, n_rest: jnp.concatenate([jnp.full((n_q,), HEAD_DIM ** -0.5, F32), jnp.ones((n_rest,), F32)])
    sb_tok = _project(x_bf16, w_t, BF16, ROW_TILE, 3 * sbw // 2, n_cols=3 * sbw,
                      col_scale=q_scale(sbw, 2 * sbw), w_transposed=True)
    nsa_tok = _project(x_bf16, cat([w_nq, w_ks, w_kw, w_vs, w_vw]), BF16, ROW_TILE, nqw + 4 * kvw,
                       col_scale=q_scale(nqw, 4 * kvw), rope_tables=tables, rope_cols=nqw + 2 * kvw,
                       w_transposed=True)
    cmp_gate = _project(x_bf16, cat([w_kc, w_vc, w_gate]), F32, ROW_TILE, 2 * kvw + G * LANES,
                        rope_tables=tables, rope_cols=kvw, w_transposed=True)

    o_sb = _sb_attention(sb_tok, SB_HEADS, 0, SB_HEADS, 2 * SB_HEADS)

    head = lambda a, g: a[:, g * HEAD_DIM:(g + 1) * HEAD_DIM]
    streams = jnp.stack([head(cmp_gate, g) for g in range(2 * G)])
    rep = lambda a, b: jnp.stack([a] * G + [b] * G)
    cmp = _compress(streams, rep(cmp_pos_k, cmp_pos_v), rep(cmp_w1_k, cmp_w1_v), rep(cmp_w2_k, cmp_w2_v))
    ones = jnp.ones((T, HEAD_DIM), BF16)
    v_aug = jnp.concatenate([a for h in range(2 * G) for a in (head(nsa_tok, NSA_HEADS + 2 * G + h), ones)], axis=1)
    o_nsa = _nsa_attention(nsa_tok, v_aug, cmp[:G], cmp[G:], cmp_gate,
                           ks_col=NSA_HEADS, kw_col=NSA_HEADS + G, vs_col=0, vw_col=G, gate_col=2 * G)
    return _rms_mm_res_ln(o_sb, o_nsa, mix_norm_g, w_out.astype(BF16), x_f32, ln_g, ln_b, alpha)


def _memory_block(x_f32, x_bf16, mem, w_q, w_k, w_v, w_o, ln_g, ln_b, alpha):
    D = w_q.shape[1]
    q = _project(x_bf16, w_q.astype(BF16), BF16, ROW_TILE, D,
                 col_scale=jnp.full((D,), (D // MEM_HEADS) ** -0.5, F32))
    kv = _project(mem.astype(BF16), jnp.concatenate([w_k, w_v], axis=1).astype(BF16), BF16, ROW_TILE, D)
    o = _cross_attention(q, kv, MEM_HEADS)
    return _mm_res_ln(o, w_o.astype(BF16), x_f32, ln_g, ln_b, alpha, 1.0, True)


def kernel(x, mem, ln1_g, ln1_b, ffn1_gate, ffn1_up, ffn1_down, w_in, cmp_pos_k, cmp_w1_k, cmp_w2_k, cmp_pos_v, cmp_w1_v, cmp_w2_v, mix_norm_g, w_out, ln2_g, ln2_b, mem_wq, mem_wk, mem_wv, mem_wo, ln3_g, ln3_b, ffn2_gate, ffn2_up, ffn2_down, ln4_g, ln4_b):
    n_layers = ffn1_gate.shape[0]
    alpha = (2 * n_layers) ** 0.25
    outs = []
    for bi in range(x.shape[0]):
        xf = x[bi]
        xb = xf
        for l in range(n_layers):
            xf, xb = _ffn(xf, xb, ffn1_gate[l], ffn1_up[l], ffn1_down[l], ln1_g[l], ln1_b[l], alpha, True)
            xf, xb = _mixer(xf, xb, w_in[l], cmp_pos_k[l], cmp_w1_k[l], cmp_w2_k[l], cmp_pos_v[l], cmp_w1_v[l],
                            cmp_w2_v[l], mix_norm_g[l], w_out[l], ln2_g[l], ln2_b[l], alpha)
            xf, xb = _memory_block(xf, xb, mem[bi], mem_wq[l], mem_wk[l], mem_wv[l], mem_wo[l], ln3_g[l], ln3_b[l],
                                   alpha)
            xf, xb = _ffn(xf, xb, ffn2_gate[l], ffn2_up[l], ffn2_down[l], ln4_g[l], ln4_b[l], alpha,
                          l + 1 < n_layers)
        outs.append(xf)
    return outs[0][None] if len(outs) == 1 else jnp.stack(outs)
```

```python
import functools

import numpy as np
import jax
import jax.numpy as jnp
from jax import lax
from jax.experimental import pallas as pl
from jax.experimental.pallas import tpu as pltpu

HEAD_DIM = 128
SB_HEADS = 8
NSA_HEADS = 8
NSA_KV_GROUPS = 2
NSA_REP = NSA_HEADS // NSA_KV_GROUPS
N_GATES = 3
CMP_BLOCK = 32
CMP_STRIDE = 16
SEL_BLOCK = 64
SEL_SHIFT = 6
SEL_TOPK = 16
WINDOW = 512
MEM_HEADS = 4
ROPE_THETA = 10000.0
LN_EPS = 1e-5
RMS_EPS = 1e-6
MASK_FILL = -1e30
LOG2E = 1.4426950408889634
SB_UNDERFLOW = -110.0
assert 1 << SEL_SHIFT == SEL_BLOCK

LANES = 128
SUBLANES = 8
VMEM_LIMIT = 56 * 1024 * 1024

ROW_TILE = 1024
LN_ROW_TILE = 512
LN_ROW_GROUP = 2
FF_TILE = 512
FF_DOWN_STEPS = 4
SB_BLOCK = 256
SB_HEADS_PER_STEP = 4
NSA_QUERY_BLOCK = 256
NSA_KEY_BLOCK = 1024

BF16 = jnp.bfloat16
F32 = jnp.float32


def _params(sem):
    return pltpu.CompilerParams(dimension_semantics=sem, vmem_limit_bytes=VMEM_LIMIT)


def _dot(a, b):
    return jnp.dot(a, b, preferred_element_type=F32)


def _dot_nt(a, b):
    return lax.dot_general(a, b, (((1,), (1,)), ((), ())), preferred_element_type=F32)


def _split_dot(a, b):
    hi = a.astype(BF16)
    lo = (a - hi.astype(F32)).astype(BF16)
    return _dot(hi, b) + _dot(lo, b)


def _layer_norm(z, g, b):
    mu = jnp.mean(z, axis=-1, keepdims=True)
    zc = z - mu
    var = jnp.mean(zc * zc, axis=-1, keepdims=True)
    return zc * lax.rsqrt(var + LN_EPS) * g + b


def _proj_kernel(x_ref, w_ref, *rest, scaled, rope_heads, w_transposed):
    rest = list(rest)
    o_ref = rest.pop()
    y = (_dot_nt if w_transposed else _dot)(x_ref[...], w_ref[...])
    if scaled:
        y = y * rest.pop(0)[...]
    if rope_heads:
        cos_ref, sin_ref = rest
        c = cos_ref[...]
        s = sin_ref[...]
        heads = []
        for h in range(rope_heads):
            yh = y[:, h * HEAD_DIM:(h + 1) * HEAD_DIM]
            heads.append(yh * c + pltpu.roll(yh, HEAD_DIM // 2, 1) * s)
        if rope_heads * HEAD_DIM < y.shape[1]:
            heads.append(y[:, rope_heads * HEAD_DIM:])
        y = jnp.concatenate(heads, axis=1) if len(heads) > 1 else heads[0]
    o_ref[...] = y.astype(o_ref.dtype)


def _project(x, w, out_dtype, tm, tn, col_scale=None, rope_tables=None, rope_cols=None, w_transposed=False,
             n_cols=None):
    M, K = x.shape
    N = n_cols if n_cols is not None else (w.shape[0] if w_transposed else w.shape[1])
    tm = min(tm, M)
    assert M % tm == 0 and N % tn == 0
    rope_heads = 0 if rope_tables is None else (tn if rope_cols is None else rope_cols) // HEAD_DIM
    in_specs = [pl.BlockSpec((tm, K), lambda i, j: (i, 0)),
                pl.BlockSpec((tn, K), lambda i, j: (j, 0)) if w_transposed
                else pl.BlockSpec((K, tn), lambda i, j: (0, j))]
    args = [x, w]
    if col_scale is not None:
        in_specs.append(pl.BlockSpec((1, tn), lambda i, j: (0, j)))
        args.append(col_scale.reshape(1, N))
    if rope_tables is not None:
        in_specs += [pl.BlockSpec((tm, HEAD_DIM), lambda i, j: (i, 0))] * 2
        args += list(rope_tables)
    return pl.pallas_call(
        functools.partial(_proj_kernel, scaled=col_scale is not None, rope_heads=rope_heads,
                          w_transposed=w_transposed),
        out_shape=jax.ShapeDtypeStruct((M, N), out_dtype),
        grid=(M // tm, N // tn),
        in_specs=in_specs,
        out_specs=pl.BlockSpec((tm, tn), lambda i, j: (i, j)),
        compiler_params=_params(("parallel", "arbitrary")),
        name="proj_rope" if rope_tables is not None else "proj",
    )(*args)


def _swiglu_up_kernel(x_ref, wg_ref, wu_ref, o_ref, wg_bf16, wu_bf16):
    @pl.when(pl.program_id(1) == 0)
    def _():
        wg_bf16[...] = wg_ref[...].astype(BF16)
        wu_bf16[...] = wu_ref[...].astype(BF16)

    x = x_ref[...]
    if x.dtype != BF16:
        x = x.astype(BF16)
    g = _dot(x, wg_bf16[...])
    u = _dot(x, wu_bf16[...])
    o_ref[...] = (jax.nn.silu(g) * u).astype(o_ref.dtype)


def _swiglu_up(x, wg, wu, tn, n_blocks, tm=ROW_TILE):
    M, K = x.shape
    tm = min(tm, M)
    assert M % tm == 0 and n_blocks * tn <= wg.shape[1] and wg.shape == wu.shape
    return pl.pallas_call(
        _swiglu_up_kernel,
        out_shape=jax.ShapeDtypeStruct((M, n_blocks * tn), BF16),
        grid=(n_blocks, M // tm),
        in_specs=[pl.BlockSpec((tm, K), lambda j, i: (i, 0)),
                  pl.BlockSpec((K, tn), lambda j, i: (0, j)),
                  pl.BlockSpec((K, tn), lambda j, i: (0, j))],
        out_specs=pl.BlockSpec((tm, tn), lambda j, i: (i, j)),
        scratch_shapes=[pltpu.VMEM((K, tn), BF16), pltpu.VMEM((K, tn), BF16)],
        compiler_params=_params(("parallel", "arbitrary")),
        name="swiglu_up",
    )(x, wg, wu)


def _row_chunks(tm, n_split):
    return [slice(c * tm // n_split, (c + 1) * tm // n_split) for c in range(n_split)]


def _res_ln_store(chunks, totals, res_ref, g_ref, b_ref, of_ref, ob_ref, alpha, coef):
    for c, y in zip(chunks, totals):
        out = _layer_norm(alpha * res_ref[c, :] + coef * y, g_ref[...], b_ref[...])
        of_ref[c, :] = out
        if ob_ref is not None:
            ob_ref[c, :] = out.astype(BF16)


def _mm_res_ln_kernel(*refs, alpha, coef, emit_bf16, n_k, n_split, has_tail):
    refs = list(refs)
    h_ref, w_ref = refs.pop(0), refs.pop(0)
    ht_ref, wt_ref = (refs.pop(0), refs.pop(0)) if has_tail else (None, None)
    res_ref, g_ref, b_ref, of_ref = refs.pop(0), refs.pop(0), refs.pop(0), refs.pop(0)
    ob_ref = refs.pop(0) if emit_bf16 else None
    acc_ref = refs.pop(0) if n_k > 1 else None
    chunks = _row_chunks(h_ref.shape[0], n_split)

    def partial_products():
        return [_dot(h_ref[c, :], w_ref[...]) for c in chunks]

    def finish(totals):
        if has_tail:
            totals = [y + _dot(ht_ref[c, :], wt_ref[...]) for c, y in zip(chunks, totals)]
        _res_ln_store(chunks, totals, res_ref, g_ref, b_ref, of_ref, ob_ref, alpha, coef)

    if n_k == 1:
        finish(partial_products())
        return
    k = pl.program_id(1)
    r = pl.program_id(2)

    @pl.when(k == 0)
    def _():
        for c, y in zip(chunks, partial_products()):
            acc_ref[r, c, :] = y

    @pl.when((k > 0) & (k < n_k - 1))
    def _():
        for c, y in zip(chunks, partial_products()):
            acc_ref[r, c, :] += y

    @pl.when(k == n_k - 1)
    def _():
        finish([acc_ref[r, c, :] + y for c, y in zip(chunks, partial_products())])


def _mm_res_ln(h, w, res, g, b, alpha, coef, emit_bf16, tm=LN_ROW_TILE, n_k=1, n_split=2, tail=None):
    M, K = h.shape
    N = w.shape[1]
    tm = min(tm, M)
    tk = K // n_k
    group = LN_ROW_GROUP if n_k > 1 else 1
    assert M % (tm * group) == 0 and K % n_k == 0 and K <= w.shape[0] and (n_k == 1 or tk % LANES == 0)
    late_row = lambda i, k, r: (jnp.where(k == n_k - 1, group * i + r, group * i), 0)
    row_block = pl.BlockSpec((tm, N), late_row)
    vec = pl.BlockSpec((1, N), lambda i, k, r: (0, 0))
    in_specs = [pl.BlockSpec((tm, tk), lambda i, k, r: (group * i + r, k)),
                pl.BlockSpec((tk, N), lambda i, k, r: (k, 0))]
    args = [h, w]
    if tail is not None:
        kt = tail[0].shape[1]
        in_specs += [pl.BlockSpec((tm, kt), late_row), pl.BlockSpec((kt, N), lambda i, k, r: (0, 0))]
        args += list(tail)
    out = pl.pallas_call(
        functools.partial(_mm_res_ln_kernel, alpha=alpha, coef=coef, emit_bf16=emit_bf16,
                          n_k=n_k, n_split=n_split, has_tail=tail is not None),
        out_shape=[jax.ShapeDtypeStruct((M, N), F32)] + [jax.ShapeDtypeStruct((M, N), BF16)] * emit_bf16,
        grid=(M // (tm * group), n_k, group),
        in_specs=in_specs + [row_block, vec, vec],
        out_specs=[row_block] * (1 + emit_bf16),
        scratch_shapes=[pltpu.VMEM((group, tm, N), F32)] if n_k > 1 else [],
        compiler_params=_params(("parallel", "arbitrary", "arbitrary")),
        name="mm_res_ln",
    )(*args, res, g.reshape(1, N), b.reshape(1, N))
    return (out[0], out[1]) if emit_bf16 else (out[0], None)


def _rms_mm_res_ln_kernel(a_ref, b_ref, gain_ref, w_ref, res_ref, g_ref, beta_ref, of_ref, ob_ref, *,
                          alpha, n_split):
    chunks = _row_chunks(a_ref.shape[0], n_split)
    wa = a_ref.shape[1]
    gain = gain_ref[...]

    def rms(o, gn):
        return (o * lax.rsqrt(jnp.mean(o * o, axis=-1, keepdims=True) + RMS_EPS) * gn).astype(BF16)

    totals = [_dot(jnp.concatenate([rms(a_ref[c, :], gain[:, :wa]), rms(b_ref[c, :], gain[:, wa:])], axis=1),
                   w_ref[...]) for c in chunks]
    _res_ln_store(chunks, totals, res_ref, g_ref, beta_ref, of_ref, ob_ref, alpha, 1.0)


def _rms_mm_res_ln(o_a, o_b, gain, w, res, g, b, alpha, tm=LN_ROW_TILE, n_split=2):
    M, wa = o_a.shape
    wb = o_b.shape[1]
    N = w.shape[1]
    tm = min(tm, M)
    assert M % tm == 0 and w.shape[0] == wa + wb
    row_block = pl.BlockSpec((tm, N), lambda i: (i, 0))
    vec = pl.BlockSpec((1, N), lambda i: (0, 0))
    return pl.pallas_call(
        functools.partial(_rms_mm_res_ln_kernel, alpha=alpha, n_split=n_split),
        out_shape=[jax.ShapeDtypeStruct((M, N), F32), jax.ShapeDtypeStruct((M, N), BF16)],
        grid=(M // tm,),
        in_specs=[pl.BlockSpec((tm, wa), lambda i: (i, 0)),
                  pl.BlockSpec((tm, wb), lambda i: (i, 0)),
                  pl.BlockSpec((1, wa + wb), lambda i: (0, 0)),
                  pl.BlockSpec((wa + wb, N), lambda i: (0, 0)),
                  row_block, vec, vec],
        out_specs=[row_block, row_block],
        compiler_params=_params(("parallel",)),
        name="rms_mm_res_ln",
    )(o_a, o_b, gain.reshape(1, wa + wb), w, res, g.reshape(1, N), b.reshape(1, N))


def _sb_kernel(q_ref, k_ref, v_ref, o_ref, acc_ref, c_ref, *, blk, heads):
    qi = pl.program_id(1)
    row = lax.broadcasted_iota(jnp.int32, (blk, blk), 0)
    col = lax.broadcasted_iota(jnp.int32, (blk, blk), 1)
    suffix = jnp.where(row > col, 1.0, 0.0).astype(BF16)
    strict = col < row

    cols = [slice(h * HEAD_DIM, (h + 1) * HEAD_DIM) for h in range(heads)]

    def step(kb_far, n_blocks, diagonal):
        ks = pl.multiple_of(kb_far * blk, blk)
        key_rows = [pl.ds(ks + b * blk, blk) for b in range(n_blocks)]
        log_beta, log_1m = {}, {}
        for p in [(b, h) for b in reversed(range(n_blocks)) for h in range(heads)]:
            y = _dot_nt(q_ref[:, cols[p[1]]], k_ref[key_rows[p[0]], cols[p[1]]])
            lb = jnp.minimum(y, 0.0) - jnp.log(1.0 + jnp.exp2(jnp.abs(y) * -LOG2E))
            l1 = lb - y
            if diagonal and p[0] == n_blocks - 1:
                l1 = jnp.where(strict, l1, 0.0)
            log_beta[p] = lb
            log_1m[p] = l1.astype(BF16)
        weights = {}
        for h in range(heads):
            c = c_ref[h]
            for b in reversed(range(n_blocks)):
                p = (b, h)
                log_stay = _dot(log_1m[p], suffix)
                w = jnp.exp(log_beta[p] + (log_stay + c))
                if diagonal and b == n_blocks - 1:
                    w = jnp.where(strict, w, 0.0)
                weights[p] = w.astype(BF16)
                c = c + (log_stay[:, :1] + log_1m[p][:, :1].astype(F32))
            c_ref[h] = c
        for h in range(heads):
            w = jnp.concatenate([weights[(b, h)] for b in range(n_blocks)], axis=1)
            acc_ref[:, cols[h]] += _dot(w, v_ref[pl.ds(ks, n_blocks * blk), cols[h]])

    acc_ref[...] = jnp.zeros_like(acc_ref)
    c_ref[...] = jnp.zeros_like(c_ref)

    @pl.when(qi == 0)
    def _():
        step(0, 1, True)

    @pl.when(qi > 0)
    def _():
        step(qi - 1, 2, True)

    def more(carry):
        done, c_max = carry
        return (done < qi) & (c_max >= SB_UNDERFLOW)

    def body(carry):
        done, _ = carry
        step(qi - 1 - done, 1, False)
        return done + 1, jnp.max(c_ref[...])

    lax.while_loop(more, body, (jnp.minimum(qi, 1), jnp.max(c_ref[...])))
    o_ref[...] = acc_ref[...]


def _sb_attention(qkv, n_heads, q_col, k_col, v_col, blk=SB_BLOCK, heads=SB_HEADS_PER_STEP):
    T = qkv.shape[0]
    blk = min(blk, T)
    width = heads * HEAD_DIM
    assert T % blk == 0 and n_heads % heads == 0
    assert q_col % heads == 0 and k_col % heads == 0 and v_col % heads == 0
    return pl.pallas_call(
        functools.partial(_sb_kernel, blk=blk, heads=heads),
        out_shape=jax.ShapeDtypeStruct((T, n_heads * HEAD_DIM), F32),
        grid=(n_heads // heads, T // blk),
        in_specs=[pl.BlockSpec((blk, width), lambda h, i: (i, q_col // heads + h)),
                  pl.BlockSpec((T, width), lambda h, i: (0, k_col // heads + h)),
                  pl.BlockSpec((T, width), lambda h, i: (0, v_col // heads + h))],
        out_specs=pl.BlockSpec((blk, width), lambda h, i: (i, h)),
        scratch_shapes=[pltpu.VMEM((blk, width), F32), pltpu.VMEM((heads, blk, 1), F32)],
        compiler_params=_params(("parallel", "arbitrary")),
        name="sb_attention",
    )(qkv, qkv, qkv)


def _compress_kernel(x_ref, pos_ref, w1_ref, w2_ref, o_ref):
    n = x_ref.shape[1]
    half = CMP_STRIDE * HEAD_DIM
    x = x_ref[0].astype(BF16)
    w1 = w1_ref[0].astype(BF16)
    a = _dot(x, w1[:half])
    b = _dot(x, w1[half:])
    pos = _dot(pos_ref[0].astype(BF16), w1)
    b_next = pltpu.roll(b, n - 1, 0)
    hid = jax.nn.gelu(a + b_next + pos[0:1])
    out = _dot(hid.astype(BF16), w2_ref[0].astype(BF16))
    valid = lax.broadcasted_iota(jnp.int32, out.shape, 0) < n - 1
    o_ref[0] = jnp.where(valid, out, 0.0).astype(o_ref.dtype)


def _compress(x_tok, pos_emb, w1, w2):
    S, T, _ = x_tok.shape
    n = T // CMP_STRIDE
    width = CMP_STRIDE * HEAD_DIM
    hidden = w1.shape[-1]
    x2 = x_tok.reshape(S, n, width)
    pos_flat = jnp.broadcast_to(pos_emb.reshape(S, 1, 2 * width), (S, SUBLANES, 2 * width))
    return pl.pallas_call(
        _compress_kernel,
        out_shape=jax.ShapeDtypeStruct((S, n, HEAD_DIM), BF16),
        grid=(S,),
        in_specs=[pl.BlockSpec((1, n, width), lambda s: (s, 0, 0)),
                  pl.BlockSpec((1, SUBLANES, 2 * width), lambda s: (s, 0, 0)),
                  pl.BlockSpec((1, 2 * width, hidden), lambda s: (s, 0, 0)),
                  pl.BlockSpec((1, hidden, HEAD_DIM), lambda s: (s, 0, 0))],
        out_specs=pl.BlockSpec((1, n, HEAD_DIM), lambda s: (s, 0, 0)),
        compiler_params=_params(("parallel",)),
        name="nsa_compress",
    )(x2, pos_flat, w1, w2)


def _nsa_kernel(q_ref, kc_ref, vc_ref, ks_ref, vs_ref, kw_ref, vw_ref, gate_ref, c2s_ref, o_ref,
                m_ref, mb_ref, acc_ref, s_ref, *, blk, tk):
    R = NSA_REP
    qi = pl.program_id(1)
    q0 = qi * blk
    n_cmp = kc_ref.shape[1]
    n_sel = c2s_ref.shape[1]
    rows = [slice(r * blk, (r + 1) * blk) for r in range(R)]
    qs = jnp.concatenate([q_ref[:, r * HEAD_DIM:(r + 1) * HEAD_DIM] for r in range(R)], axis=0)
    q_pos = q0 + lax.broadcasted_iota(jnp.int32, (blk, 1), 0)
    q_lane = q0 + lax.broadcasted_iota(jnp.int32, (1, blk), 1)

    def with_ones(v):
        return jnp.concatenate([v, jnp.ones_like(v)], axis=1)

    def softmax_av(s, bias, v1):
        sc = s + bias
        p = jnp.exp(sc - jnp.max(sc, axis=-1, keepdims=True)).astype(BF16)
        pv = _dot(p, v1)
        return pv[:, :HEAD_DIM] / pv[:, HEAD_DIM:]

    kc = kc_ref[0]
    s_cmp = [_dot_nt(qs[rows[r]], kc) for r in range(R)]
    cmp_end = lax.broadcasted_iota(jnp.int32, (1, n_cmp), 1) * CMP_STRIDE + (CMP_BLOCK - 1)
    bias_c = jnp.where(cmp_end <= q_pos, 0.0, MASK_FILL)
    o_cmp, p_sum = [], None
    for r in range(R):
        sc = s_cmp[r] + bias_c
        p = jnp.exp(sc - jnp.max(sc, axis=-1, keepdims=True))
        inv_sum = 1.0 / jnp.sum(p, axis=-1, keepdims=True)
        o_cmp.append(jnp.where(q_pos >= CMP_BLOCK - 1, _dot(p.astype(BF16), vc_ref[0]) * inv_sum, 0.0))
        p_sum = p * inv_sum if p_sum is None else p_sum + p * inv_sum
    imp = _split_dot(p_sum, c2s_ref[...])

    wk = WINDOW + blk
    ws = pl.multiple_of(jnp.maximum(q0 - WINDOW, 0), blk)
    dist = q_pos - (ws + lax.broadcasted_iota(jnp.int32, (1, wk), 1))
    bias_w = jnp.where((dist >= 0) & (dist < WINDOW), 0.0, MASK_FILL)
    k_w = kw_ref[pl.ds(ws, wk), :]
    s_w = [_dot_nt(qs[rows[r]], k_w) for r in range(R)]

    blk_id = lax.broadcasted_iota(jnp.int32, (n_sel, 1), 0)
    forced = (blk_id == 0) | (blk_id == (q_lane >> SEL_SHIFT))
    valid = blk_id * SEL_BLOCK <= q_lane
    work = jnp.where(forced, jnp.inf, jnp.where(valid, imp.T, -jnp.inf))
    blk_idf = blk_id.astype(F32)
    left = work
    o_win = []
    v1_w = with_ones(vw_ref[pl.ds(ws, wk), :])
    n_steps = min(SEL_TOPK, n_sel)
    for step in range(n_steps):
        top = jnp.max(left, axis=0, keepdims=True)
        first = jnp.min(jnp.where(left == top, blk_idf, float(n_sel)), axis=0, keepdims=True)
        left = jnp.where(blk_idf == first, -jnp.inf, left)
        while len(o_win) * n_steps < (step + 1) * R:
            o_win.append(softmax_av(s_w[len(o_win)], bias_w, v1_w))
    sel_bias = jnp.where(left != work, 0.0, MASK_FILL).T.astype(BF16)

    gates = jax.nn.sigmoid(gate_ref[...])

    def gate(r, branch):
        return gates[:, r * N_GATES + branch:r * N_GATES + branch + 1]

    o_ref[...] = jnp.concatenate([gate(r, 0) * o_cmp[r] + gate(r, 2) * o_win[r] for r in range(R)], axis=1)

    tok = lax.broadcasted_iota(jnp.int32, (n_sel, tk), 1)
    sel_row = lax.broadcasted_iota(jnp.int32, (n_sel, tk), 0)
    key_off = lax.broadcasted_iota(jnp.int32, (1, tk), 1)

    def key_rows(kb):
        return pl.ds(pl.multiple_of(kb * tk, tk), tk)

    def block_bias(kb):
        ks = kb * tk
        expand = jnp.where(((ks + tok) >> SEL_SHIFT) == sel_row, 1.0, 0.0).astype(BF16)
        bias = _dot(sel_bias, expand)
        return jnp.where(ks + key_off <= q_pos, bias, MASK_FILL)

    def put_scores(r, raw, bias):
        sc = raw + bias
        s_ref[rows[r], :] = sc
        mb_ref[rows[r], :] = jnp.max(sc, axis=-1, keepdims=True)

    def take_probs(r):
        m_old = m_ref[rows[r], :]
        m_new = jnp.maximum(m_old, mb_ref[rows[r], :])
        m_ref[rows[r], :] = m_new
        return jnp.exp(s_ref[rows[r], :] - m_new).astype(BF16), jnp.exp(m_old - m_new)

    def accumulate(r, p, alpha, v1):
        acc_ref[rows[r], :] = alpha * acc_ref[rows[r], :] + _dot(p, v1)

    m_ref[...] = jnp.full_like(m_ref, MASK_FILL)
    acc_ref[...] = jnp.zeros_like(acc_ref)
    kb_diag = qi // (tk // blk)
    bias = block_bias(0)
    for r in range(R):
        put_scores(r, _dot_nt(qs[rows[r]], ks_ref[key_rows(0), :]), bias)

    def sel_body(kb, carry):
        k_next = ks_ref[key_rows(kb + 1), :]
        bias_next = block_bias(kb + 1)
        v1 = with_ones(vs_ref[key_rows(kb), :])
        for r in range(R):
            p, alpha = take_probs(r)
            put_scores(r, _dot_nt(qs[rows[r]], k_next), bias_next)
            accumulate(r, p, alpha, v1)
        return carry

    lax.fori_loop(0, kb_diag, sel_body, 0)
    v1 = with_ones(vs_ref[key_rows(kb_diag), :])
    for r in range(R):
        p, alpha = take_probs(r)
        accumulate(r, p, alpha, v1)
    o_sel = acc_ref[:, :HEAD_DIM] / acc_ref[:, HEAD_DIM:]
    o_ref[...] += jnp.concatenate([gate(r, 1) * o_sel[rows[r]] for r in range(R)], axis=1)


def _nsa_attention(tok, k_cmp, v_cmp, gates, ks_col, kw_col, vs_col, vw_col, gate_col,
                   blk=NSA_QUERY_BLOCK, tk=NSA_KEY_BLOCK):
    T = tok.shape[0]
    G = NSA_KV_GROUPS
    blk = min(blk, T)
    tk = min(tk, T)
    assert T % tk == 0 and tk % blk == 0 and WINDOW % blk == 0 and WINDOW + blk <= T
    n_cmp = T // CMP_STRIDE
    n_sel = T // SEL_BLOCK
    cmp_start = np.arange(n_cmp)[:, None] * CMP_STRIDE
    sel_start = np.arange(n_sel)[None, :] * SEL_BLOCK
    overlap = np.clip(np.minimum(cmp_start + CMP_BLOCK, sel_start + SEL_BLOCK)
                      - np.maximum(cmp_start, sel_start), 0, None)
    c2s = jnp.asarray(overlap.astype(np.float32) / CMP_BLOCK, dtype=BF16)
    width = NSA_REP * HEAD_DIM
    tok_spec = lambda col: pl.BlockSpec((T, HEAD_DIM), lambda g, i: (0, col + g))
    cmp_spec = lambda w: pl.BlockSpec((1, n_cmp, w), lambda g, i: (g, 0, 0))
    return pl.pallas_call(
        functools.partial(_nsa_kernel, blk=blk, tk=tk),
        out_shape=jax.ShapeDtypeStruct((T, G * width), F32),
        grid=(G, T // blk),
        in_specs=[pl.BlockSpec((blk, width), lambda g, i: (i, g)),
                  cmp_spec(HEAD_DIM), cmp_spec(HEAD_DIM),
                  tok_spec(ks_col), tok_spec(vs_col), tok_spec(kw_col), tok_spec(vw_col),
                  pl.BlockSpec((blk, LANES), lambda g, i: (i, gate_col + g)),
                  pl.BlockSpec((n_cmp, n_sel), lambda g, i: (0, 0))],
        out_specs=pl.BlockSpec((blk, width), lambda g, i: (i, g)),
        scratch_shapes=[pltpu.VMEM((NSA_REP * blk, 1), F32),
                        pltpu.VMEM((NSA_REP * blk, 1), F32),
                        pltpu.VMEM((NSA_REP * blk, 2 * HEAD_DIM), F32),
                        pltpu.VMEM((NSA_REP * blk, tk), F32)],
        compiler_params=_params(("parallel", "arbitrary")),
        name="nsa_attention",
    )(tok, k_cmp, v_cmp, tok, tok, tok, tok, gates, c2s)


def _cross_kernel(q_ref, kv_ref, o_ref, *, n_heads):
    D = q_ref.shape[1]
    dh = D // n_heads
    scores = [_dot_nt(q_ref[:, h * dh:(h + 1) * dh], kv_ref[:, h * dh:(h + 1) * dh]) for h in range(n_heads)]
    for h, s in enumerate(scores):
        p = jnp.exp(s - jnp.max(s, axis=-1, keepdims=True))
        p = p / jnp.sum(p, axis=-1, keepdims=True)
        o_ref[:, h * dh:(h + 1) * dh] = _dot(p.astype(BF16), kv_ref[:, D + h * dh:D + (h + 1) * dh]).astype(o_ref.dtype)


def _cross_attention(q, kv, n_heads, tm=ROW_TILE):
    T, D = q.shape
    M = kv.shape[0]
    tm = min(tm, T)
    return pl.pallas_call(
        functools.partial(_cross_kernel, n_heads=n_heads),
        out_shape=jax.ShapeDtypeStruct((T, D), BF16),
        grid=(T // tm,),
        in_specs=[pl.BlockSpec((tm, D), lambda i: (i, 0)),
                  pl.BlockSpec((M, 2 * D), lambda i: (0, 0))],
        out_specs=pl.BlockSpec((tm, D), lambda i: (i, 0)),
        compiler_params=_params(("parallel",)),
        name="cross_attention",
    )(q, kv)


def _rope_tables(T):
    half = HEAD_DIM // 2
    inv_freq = ROPE_THETA ** (-np.arange(half, dtype=np.float64) / half)
    ang = np.arange(T, dtype=np.float64)[:, None] * inv_freq[None, :]
    cos, sin = np.cos(ang).astype(np.float32), np.sin(ang).astype(np.float32)
    return jnp.asarray(np.concatenate([cos, cos], axis=1)), jnp.asarray(np.concatenate([-sin, sin], axis=1))


def _pad_to(w, axis, mult):
    pad = -w.shape[axis] % mult
    if not pad:
        return w
    shape = list(w.shape)
    shape[axis] = pad
    return jnp.concatenate([w, jnp.zeros(shape, w.dtype)], axis=axis)


def _ffn(x_f32, x_mm, w_gate, w_up, w_down, ln_g, ln_b, alpha, emit_bf16):
    d_ff = w_gate.shape[1]
    main = d_ff // FF_TILE * FF_TILE
    assert (d_ff - main) % LANES == 0 and main % (FF_DOWN_STEPS * LANES) == 0
    h = _swiglu_up(x_mm, w_gate, w_up, FF_TILE, main // FF_TILE)
    wd = w_down.astype(BF16)
    tail = None
    if main < d_ff:
        tail = (_swiglu_up(x_mm, w_gate[:, main:], w_up[:, main:], d_ff - main, 1), wd[main:])
    return _mm_res_ln(h, wd, x_f32, ln_g, ln_b, alpha, 0.5, emit_bf16, n_k=FF_DOWN_STEPS, tail=tail)


def _mixer(x_f32, x_bf16, w_in, cmp_pos_k, cmp_w1_k, cmp_w2_k, cmp_pos_v, cmp_w1_v, cmp_w2_v,
           mix_norm_g, w_out, ln_g, ln_b, alpha):
    T = x_f32.shape[0]
    G = NSA_KV_GROUPS
    sbw = SB_HEADS * HEAD_DIM
    nqw = NSA_HEADS * HEAD_DIM
    kvw = G * HEAD_DIM
    bounds = np.cumsum([0, sbw, sbw, sbw, nqw, kvw, kvw, kvw, kvw, kvw, kvw, NSA_HEADS * N_GATES])
    w_t = w_in.T.astype(BF16)
    (w_sbq, w_sbk, w_sbv, w_nq, w_kc, w_vc, w_ks, w_vs, w_kw, w_vw, w_gate) = [
        w_t[bounds[i]:bounds[i + 1]] for i in range(11)]
    per_group = NSA_REP * N_GATES
    w_gate = jnp.concatenate([_pad_to(w_gate[g * per_group:(g + 1) * per_group], 0, LANES)
                              for g in range(G)], axis=0)
    tables = _rope_tables(T)
    cat = lambda ws: jnp.concatenate(ws, axis=0)
    q_scale = lambda n_q, n_rest: jnp.concatenate([jnp.full((n_q,), HEAD_DIM ** -0.5, F32), jnp.ones((n_rest,), F32)])
    sb_tok = _project(x_bf16, w_t, BF16, ROW_TILE, 3 * sbw // 2, n_cols=3 * sbw,
                      col_scale=q_scale(sbw, 2 * sbw), w_transposed=True)
    nsa_tok = _project(x_bf16, cat([w_nq, w_ks, w_kw, w_vs, w_vw]), BF16, ROW_TILE, nqw + 4 * kvw,
                       col_scale=q_scale(nqw, 4 * kvw), rope_tables=tables, rope_cols=nqw + 2 * kvw,
                       w_transposed=True)
    cmp_gate = _project(x_bf16, cat([w_kc, w_vc, w_gate]), F32, ROW_TILE, 2 * kvw + G * LANES,
                        rope_tables=tables, rope_cols=kvw, w_transposed=True)

    o_sb = _sb_attention(sb_tok, SB_HEADS, 0, SB_HEADS, 2 * SB_HEADS)

    head = lambda a, g: a[:, g * HEAD_DIM:(g + 1) * HEAD_DIM]
    streams = jnp.stack([head(cmp_gate, g) for g in range(2 * G)])
    rep = lambda a, b: jnp.stack([a] * G + [b] * G)
    cmp = _compress(streams, rep(cmp_pos_k, cmp_pos_v), rep(cmp_w1_k, cmp_w1_v), rep(cmp_w2_k, cmp_w2_v))
    o_nsa = _nsa_attention(nsa_tok, cmp[:G], cmp[G:], cmp_gate, ks_col=NSA_HEADS, kw_col=NSA_HEADS + G,
                           vs_col=NSA_HEADS + 2 * G, vw_col=NSA_HEADS + 3 * G, gate_col=2 * G)
    return _rms_mm_res_ln(o_sb, o_nsa, mix_norm_g, w_out.astype(BF16), x_f32, ln_g, ln_b, alpha)


def _memory_block(x_f32, x_bf16, mem, w_q, w_k, w_v, w_o, ln_g, ln_b, alpha):
    D = w_q.shape[1]
    q = _project(x_bf16, w_q.astype(BF16), BF16, ROW_TILE, D,
                 col_scale=jnp.full((D,), (D // MEM_HEADS) ** -0.5, F32))
    kv = _project(mem.astype(BF16), jnp.concatenate([w_k, w_v], axis=1).astype(BF16), BF16, ROW_TILE, D)
    o = _cross_attention(q, kv, MEM_HEADS)
    return _mm_res_ln(o, w_o.astype(BF16), x_f32, ln_g, ln_b, alpha, 1.0, True)


def kernel(x, mem, ln1_g, ln1_b, ffn1_gate, ffn1_up, ffn1_down, w_in, cmp_pos_k, cmp_w1_k, cmp_w2_k, cmp_pos_v, cmp_w1_v, cmp_w2_v, mix_norm_g, w_out, ln2_g, ln2_b, mem_wq, mem_wk, mem_wv, mem_wo, ln3_g, ln3_b, ffn2_gate, ffn2_up, ffn2_down, ln4_g, ln4_b):
    n_layers = ffn1_gate.shape[0]
    alpha = (2 * n_layers) ** 0.25
    outs = []
    for bi in range(x.shape[0]):
        xf = x[bi]
        xb = xf
        for l in range(n_layers):
            xf, xb = _ffn(xf, xb, ffn1_gate[l], ffn1_up[l], ffn1_down[l], ln1_g[l], ln1_b[l], alpha, True)
            xf, xb = _mixer(xf, xb, w_in[l], cmp_pos_k[l], cmp_w1_k[l], cmp_w2_k[l], cmp_pos_v[l], cmp_w1_v[l],
                            cmp_w2_v[l], mix_norm_g[l], w_out[l], ln2_g[l], ln2_b[l], alpha)
            xf, xb = _memory_block(xf, xb, mem[bi], mem_wq[l], mem_wk[l], mem_wv[l], mem_wo[l], ln3_g[l], ln3_b[l],
                                   alpha)
            xf, xb = _ffn(xf, xb, ffn2_gate[l], ffn2_up[l], ffn2_down[l], ln4_g[l], ln4_b[l], alpha,
                          l + 1 < n_layers)
        outs.append(xf)
    return outs[0][None] if len(outs) == 1 else jnp.stack(outs)
```

```python
import functools

import numpy as np
import jax
import jax.numpy as jnp
from jax import lax
from jax.experimental import pallas as pl
from jax.experimental.pallas import tpu as pltpu

HEAD_DIM = 128
SB_HEADS = 8
NSA_HEADS = 8
NSA_KV_GROUPS = 2
NSA_REP = NSA_HEADS // NSA_KV_GROUPS
N_GATES = 3
CMP_BLOCK = 32
CMP_STRIDE = 16
SEL_BLOCK = 64
SEL_SHIFT = 6
SEL_TOPK = 16
WINDOW = 512
MEM_HEADS = 4
ROPE_THETA = 10000.0
LN_EPS = 1e-5
RMS_EPS = 1e-6
MASK_FILL = -1e30
LOG2E = 1.4426950408889634
SB_UNDERFLOW = -110.0
assert 1 << SEL_SHIFT == SEL_BLOCK

LANES = 128
SUBLANES = 8
VMEM_LIMIT = 56 * 1024 * 1024

ROW_TILE = 1024
LN_ROW_TILE = 512
LN_ROW_GROUP = 2
FF_TILE = 512
FF_DOWN_STEPS = 4
SB_BLOCK = 256
SB_HEADS_PER_STEP = 4
NSA_QUERY_BLOCK = 512
NSA_KEY_BLOCK = 1024

BF16 = jnp.bfloat16
F32 = jnp.float32


def _params(sem):
    return pltpu.CompilerParams(dimension_semantics=sem, vmem_limit_bytes=VMEM_LIMIT)


def _dot(a, b):
    return jnp.dot(a, b, preferred_element_type=F32)


def _dot_nt(a, b):
    return lax.dot_general(a, b, (((1,), (1,)), ((), ())), preferred_element_type=F32)


def _split_dot(a, b):
    hi = a.astype(BF16)
    lo = (a - hi.astype(F32)).astype(BF16)
    return _dot(hi, b) + _dot(lo, b)


def _layer_norm(z, g, b):
    mu = jnp.mean(z, axis=-1, keepdims=True)
    zc = z - mu
    var = jnp.mean(zc * zc, axis=-1, keepdims=True)
    return zc * lax.rsqrt(var + LN_EPS) * g + b


def _proj_kernel(x_ref, w_ref, *rest, scaled, rope_heads, w_transposed):
    rest = list(rest)
    o_ref = rest.pop()
    y = (_dot_nt if w_transposed else _dot)(x_ref[...], w_ref[...])
    if scaled:
        y = y * rest.pop(0)[...]
    if rope_heads:
        cos_ref, sin_ref = rest
        c = cos_ref[...]
        s = sin_ref[...]
        heads = []
        for h in range(rope_heads):
            yh = y[:, h * HEAD_DIM:(h + 1) * HEAD_DIM]
            heads.append(yh * c + pltpu.roll(yh, HEAD_DIM // 2, 1) * s)
        if rope_heads * HEAD_DIM < y.shape[1]:
            heads.append(y[:, rope_heads * HEAD_DIM:])
        y = jnp.concatenate(heads, axis=1) if len(heads) > 1 else heads[0]
    o_ref[...] = y.astype(o_ref.dtype)


def _project(x, w, out_dtype, tm, tn, col_scale=None, rope_tables=None, rope_cols=None, w_transposed=False,
             n_cols=None):
    M, K = x.shape
    N = n_cols if n_cols is not None else (w.shape[0] if w_transposed else w.shape[1])
    tm = min(tm, M)
    assert M % tm == 0 and N % tn == 0
    rope_heads = 0 if rope_tables is None else (tn if rope_cols is None else rope_cols) // HEAD_DIM
    in_specs = [pl.BlockSpec((tm, K), lambda i, j: (i, 0)),
                pl.BlockSpec((tn, K), lambda i, j: (j, 0)) if w_transposed
                else pl.BlockSpec((K, tn), lambda i, j: (0, j))]
    args = [x, w]
    if col_scale is not None:
        in_specs.append(pl.BlockSpec((1, tn), lambda i, j: (0, j)))
        args.append(col_scale.reshape(1, N))
    if rope_tables is not None:
        in_specs += [pl.BlockSpec((tm, HEAD_DIM), lambda i, j: (i, 0))] * 2
        args += list(rope_tables)
    return pl.pallas_call(
        functools.partial(_proj_kernel, scaled=col_scale is not None, rope_heads=rope_heads,
                          w_transposed=w_transposed),
        out_shape=jax.ShapeDtypeStruct((M, N), out_dtype),
        grid=(M // tm, N // tn),
        in_specs=in_specs,
        out_specs=pl.BlockSpec((tm, tn), lambda i, j: (i, j)),
        compiler_params=_params(("parallel", "arbitrary")),
        name="proj_rope" if rope_tables is not None else "proj",
    )(*args)


def _swiglu_up_kernel(x_ref, wg_ref, wu_ref, o_ref, wg_bf16, wu_bf16):
    @pl.when(pl.program_id(1) == 0)
    def _():
        wg_bf16[...] = wg_ref[...].astype(BF16)
        wu_bf16[...] = wu_ref[...].astype(BF16)

    x = x_ref[...]
    if x.dtype != BF16:
        x = x.astype(BF16)
    g = _dot(x, wg_bf16[...])
    u = _dot(x, wu_bf16[...])
    o_ref[...] = (jax.nn.silu(g) * u).astype(o_ref.dtype)


def _swiglu_up(x, wg, wu, tn, n_blocks, tm=ROW_TILE):
    M, K = x.shape
    tm = min(tm, M)
    assert M % tm == 0 and n_blocks * tn <= wg.shape[1] and wg.shape == wu.shape
    return pl.pallas_call(
        _swiglu_up_kernel,
        out_shape=jax.ShapeDtypeStruct((M, n_blocks * tn), BF16),
        grid=(n_blocks, M // tm),
        in_specs=[pl.BlockSpec((tm, K), lambda j, i: (i, 0)),
                  pl.BlockSpec((K, tn), lambda j, i: (0, j)),
                  pl.BlockSpec((K, tn), lambda j, i: (0, j))],
        out_specs=pl.BlockSpec((tm, tn), lambda j, i: (i, j)),
        scratch_shapes=[pltpu.VMEM((K, tn), BF16), pltpu.VMEM((K, tn), BF16)],
        compiler_params=_params(("parallel", "arbitrary")),
        name="swiglu_up",
    )(x, wg, wu)


def _row_chunks(tm, n_split):
    return [slice(c * tm // n_split, (c + 1) * tm // n_split) for c in range(n_split)]


def _res_ln_store(chunks, totals, res_ref, g_ref, b_ref, of_ref, ob_ref, alpha, coef):
    for c, y in zip(chunks, totals):
        out = _layer_norm(alpha * res_ref[c, :] + coef * y, g_ref[...], b_ref[...])
        of_ref[c, :] = out
        if ob_ref is not None:
            ob_ref[c, :] = out.astype(BF16)


def _mm_res_ln_kernel(*refs, alpha, coef, emit_bf16, n_k, n_split, has_tail):
    refs = list(refs)
    h_ref, w_ref = refs.pop(0), refs.pop(0)
    ht_ref, wt_ref = (refs.pop(0), refs.pop(0)) if has_tail else (None, None)
    res_ref, g_ref, b_ref, of_ref = refs.pop(0), refs.pop(0), refs.pop(0), refs.pop(0)
    ob_ref = refs.pop(0) if emit_bf16 else None
    acc_ref = refs.pop(0) if n_k > 1 else None
    chunks = _row_chunks(h_ref.shape[0], n_split)

    def partial_products():
        return [_dot(h_ref[c, :], w_ref[...]) for c in chunks]

    def finish(totals):
        if has_tail:
            totals = [y + _dot(ht_ref[c, :], wt_ref[...]) for c, y in zip(chunks, totals)]
        _res_ln_store(chunks, totals, res_ref, g_ref, b_ref, of_ref, ob_ref, alpha, coef)

    if n_k == 1:
        finish(partial_products())
        return
    k = pl.program_id(1)
    r = pl.program_id(2)

    @pl.when(k == 0)
    def _():
        for c, y in zip(chunks, partial_products()):
            acc_ref[r, c, :] = y

    @pl.when((k > 0) & (k < n_k - 1))
    def _():
        for c, y in zip(chunks, partial_products()):
            acc_ref[r, c, :] += y

    @pl.when(k == n_k - 1)
    def _():
        finish([acc_ref[r, c, :] + y for c, y in zip(chunks, partial_products())])


def _mm_res_ln(h, w, res, g, b, alpha, coef, emit_bf16, tm=LN_ROW_TILE, n_k=1, n_split=2, tail=None):
    M, K = h.shape
    N = w.shape[1]
    tm = min(tm, M)
    tk = K // n_k
    group = LN_ROW_GROUP if n_k > 1 else 1
    assert M % (tm * group) == 0 and K % n_k == 0 and K <= w.shape[0] and (n_k == 1 or tk % LANES == 0)
    late_row = lambda i, k, r: (jnp.where(k == n_k - 1, group * i + r, group * i), 0)
    row_block = pl.BlockSpec((tm, N), late_row)
    vec = pl.BlockSpec((1, N), lambda i, k, r: (0, 0))
    in_specs = [pl.BlockSpec((tm, tk), lambda i, k, r: (group * i + r, k)),
                pl.BlockSpec((tk, N), lambda i, k, r: (k, 0))]
    args = [h, w]
    if tail is not None:
        kt = tail[0].shape[1]
        in_specs += [pl.BlockSpec((tm, kt), late_row), pl.BlockSpec((kt, N), lambda i, k, r: (0, 0))]
        args += list(tail)
    out = pl.pallas_call(
        functools.partial(_mm_res_ln_kernel, alpha=alpha, coef=coef, emit_bf16=emit_bf16,
                          n_k=n_k, n_split=n_split, has_tail=tail is not None),
        out_shape=[jax.ShapeDtypeStruct((M, N), F32)] + [jax.ShapeDtypeStruct((M, N), BF16)] * emit_bf16,
        grid=(M // (tm * group), n_k, group),
        in_specs=in_specs + [row_block, vec, vec],
        out_specs=[row_block] * (1 + emit_bf16),
        scratch_shapes=[pltpu.VMEM((group, tm, N), F32)] if n_k > 1 else [],
        compiler_params=_params(("parallel", "arbitrary", "arbitrary")),
        name="mm_res_ln",
    )(*args, res, g.reshape(1, N), b.reshape(1, N))
    return (out[0], out[1]) if emit_bf16 else (out[0], None)


def _rms_mm_res_ln_kernel(a_ref, b_ref, gain_ref, w_ref, res_ref, g_ref, beta_ref, of_ref, ob_ref, *,
                          alpha, n_split):
    chunks = _row_chunks(a_ref.shape[0], n_split)
    wa = a_ref.shape[1]
    gain = gain_ref[...]

    def rms(o, gn):
        return (o * lax.rsqrt(jnp.mean(o * o, axis=-1, keepdims=True) + RMS_EPS) * gn).astype(BF16)

    totals = [_dot(jnp.concatenate([rms(a_ref[c, :], gain[:, :wa]), rms(b_ref[c, :], gain[:, wa:])], axis=1),
                   w_ref[...]) for c in chunks]
    _res_ln_store(chunks, totals, res_ref, g_ref, beta_ref, of_ref, ob_ref, alpha, 1.0)


def _rms_mm_res_ln(o_a, o_b, gain, w, res, g, b, alpha, tm=LN_ROW_TILE, n_split=2):
    M, wa = o_a.shape
    wb = o_b.shape[1]
    N = w.shape[1]
    tm = min(tm, M)
    assert M % tm == 0 and w.shape[0] == wa + wb
    row_block = pl.BlockSpec((tm, N), lambda i: (i, 0))
    vec = pl.BlockSpec((1, N), lambda i: (0, 0))
    return pl.pallas_call(
        functools.partial(_rms_mm_res_ln_kernel, alpha=alpha, n_split=n_split),
        out_shape=[jax.ShapeDtypeStruct((M, N), F32), jax.ShapeDtypeStruct((M, N), BF16)],
        grid=(M // tm,),
        in_specs=[pl.BlockSpec((tm, wa), lambda i: (i, 0)),
                  pl.BlockSpec((tm, wb), lambda i: (i, 0)),
                  pl.BlockSpec((1, wa + wb), lambda i: (0, 0)),
                  pl.BlockSpec((wa + wb, N), lambda i: (0, 0)),
                  row_block, vec, vec],
        out_specs=[row_block, row_block],
        compiler_params=_params(("parallel",)),
        name="rms_mm_res_ln",
    )(o_a, o_b, gain.reshape(1, wa + wb), w, res, g.reshape(1, N), b.reshape(1, N))


def _sb_kernel(q_ref, k_ref, v_ref, o_ref, acc_ref, c_ref, *, blk, heads):
    qi = pl.program_id(1)
    row = lax.broadcasted_iota(jnp.int32, (blk, blk), 0)
    col = lax.broadcasted_iota(jnp.int32, (blk, blk), 1)
    suffix = jnp.where(row > col, 1.0, 0.0).astype(BF16)
    strict = col < row

    cols = [slice(h * HEAD_DIM, (h + 1) * HEAD_DIM) for h in range(heads)]

    def step(kb_far, n_blocks, diagonal):
        ks = pl.multiple_of(kb_far * blk, blk)
        key_rows = [pl.ds(ks + b * blk, blk) for b in range(n_blocks)]
        log_beta, log_1m = {}, {}
        for p in [(b, h) for b in reversed(range(n_blocks)) for h in range(heads)]:
            y = _dot_nt(q_ref[:, cols[p[1]]], k_ref[key_rows[p[0]], cols[p[1]]])
            lb = jnp.minimum(y, 0.0) - jnp.log(1.0 + jnp.exp2(jnp.abs(y) * -LOG2E))
            l1 = lb - y
            if diagonal and p[0] == n_blocks - 1:
                l1 = jnp.where(strict, l1, 0.0)
            log_beta[p] = lb
            log_1m[p] = l1.astype(BF16)
        weights = {}
        for h in range(heads):
            c = c_ref[h]
            for b in reversed(range(n_blocks)):
                p = (b, h)
                log_stay = _dot(log_1m[p], suffix)
                w = jnp.exp(log_beta[p] + (log_stay + c))
                if diagonal and b == n_blocks - 1:
                    w = jnp.where(strict, w, 0.0)
                weights[p] = w.astype(BF16)
                c = c + (log_stay[:, :1] + log_1m[p][:, :1].astype(F32))
            c_ref[h] = c
        for h in range(heads):
            w = jnp.concatenate([weights[(b, h)] for b in range(n_blocks)], axis=1)
            acc_ref[:, cols[h]] += _dot(w, v_ref[pl.ds(ks, n_blocks * blk), cols[h]])

    acc_ref[...] = jnp.zeros_like(acc_ref)
    c_ref[...] = jnp.zeros_like(c_ref)

    @pl.when(qi == 0)
    def _():
        step(0, 1, True)

    @pl.when(qi > 0)
    def _():
        step(qi - 1, 2, True)

    def more(carry):
        done, c_max = carry
        return (done < qi) & (c_max >= SB_UNDERFLOW)

    def body(carry):
        done, _ = carry
        step(qi - 1 - done, 1, False)
        return done + 1, jnp.max(c_ref[...])

    lax.while_loop(more, body, (jnp.minimum(qi, 1), jnp.max(c_ref[...])))
    o_ref[...] = acc_ref[...]


def _sb_attention(qkv, n_heads, q_col, k_col, v_col, blk=SB_BLOCK, heads=SB_HEADS_PER_STEP):
    T = qkv.shape[0]
    blk = min(blk, T)
    width = heads * HEAD_DIM
    assert T % blk == 0 and n_heads % heads == 0
    assert q_col % heads == 0 and k_col % heads == 0 and v_col % heads == 0
    return pl.pallas_call(
        functools.partial(_sb_kernel, blk=blk, heads=heads),
        out_shape=jax.ShapeDtypeStruct((T, n_heads * HEAD_DIM), F32),
        grid=(n_heads // heads, T // blk),
        in_specs=[pl.BlockSpec((blk, width), lambda h, i: (i, q_col // heads + h)),
                  pl.BlockSpec((T, width), lambda h, i: (0, k_col // heads + h)),
                  pl.BlockSpec((T, width), lambda h, i: (0, v_col // heads + h))],
        out_specs=pl.BlockSpec((blk, width), lambda h, i: (i, h)),
        scratch_shapes=[pltpu.VMEM((blk, width), F32), pltpu.VMEM((heads, blk, 1), F32)],
        compiler_params=_params(("parallel", "arbitrary")),
        name="sb_attention",
    )(qkv, qkv, qkv)


def _compress_kernel(x_ref, pos_ref, w1_ref, w2_ref, o_ref):
    n = x_ref.shape[1]
    half = CMP_STRIDE * HEAD_DIM
    x = x_ref[0].astype(BF16)
    w1 = w1_ref[0].astype(BF16)
    a = _dot(x, w1[:half])
    b = _dot(x, w1[half:])
    pos = _dot(pos_ref[0].astype(BF16), w1)
    b_next = pltpu.roll(b, n - 1, 0)
    hid = jax.nn.gelu(a + b_next + pos[0:1])
    out = _dot(hid.astype(BF16), w2_ref[0].astype(BF16))
    valid = lax.broadcasted_iota(jnp.int32, out.shape, 0) < n - 1
    o_ref[0] = jnp.where(valid, out, 0.0).astype(o_ref.dtype)


def _compress(x_tok, pos_emb, w1, w2):
    S, T, _ = x_tok.shape
    n = T // CMP_STRIDE
    width = CMP_STRIDE * HEAD_DIM
    hidden = w1.shape[-1]
    x2 = x_tok.reshape(S, n, width)
    pos_flat = jnp.broadcast_to(pos_emb.reshape(S, 1, 2 * width), (S, SUBLANES, 2 * width))
    return pl.pallas_call(
        _compress_kernel,
        out_shape=jax.ShapeDtypeStruct((S, n, HEAD_DIM), BF16),
        grid=(S,),
        in_specs=[pl.BlockSpec((1, n, width), lambda s: (s, 0, 0)),
                  pl.BlockSpec((1, SUBLANES, 2 * width), lambda s: (s, 0, 0)),
                  pl.BlockSpec((1, 2 * width, hidden), lambda s: (s, 0, 0)),
                  pl.BlockSpec((1, hidden, HEAD_DIM), lambda s: (s, 0, 0))],
        out_specs=pl.BlockSpec((1, n, HEAD_DIM), lambda s: (s, 0, 0)),
        compiler_params=_params(("parallel",)),
        name="nsa_compress",
    )(x2, pos_flat, w1, w2)


def _nsa_kernel(q_ref, kc_ref, vc_ref, ks_ref, vs_ref, kw_ref, vw_ref, gate_ref, c2s_ref, o_ref,
                m_ref, mb_ref, acc_ref, s_ref, *, blk, tk):
    R = NSA_REP
    qi = pl.program_id(1)
    q0 = qi * blk
    n_cmp = kc_ref.shape[1]
    n_sel = c2s_ref.shape[1]
    rows = [slice(r * blk, (r + 1) * blk) for r in range(R)]
    qs = jnp.concatenate([q_ref[:, r * HEAD_DIM:(r + 1) * HEAD_DIM] for r in range(R)], axis=0)
    q_pos = q0 + lax.broadcasted_iota(jnp.int32, (blk, 1), 0)
    q_lane = q0 + lax.broadcasted_iota(jnp.int32, (1, blk), 1)

    def with_ones(v):
        return jnp.concatenate([v, jnp.ones_like(v)], axis=1)

    def softmax_av(s, bias, v1):
        sc = s + bias
        p = jnp.exp(sc - jnp.max(sc, axis=-1, keepdims=True)).astype(BF16)
        pv = _dot(p, v1)
        return pv[:, :HEAD_DIM] / pv[:, HEAD_DIM:]

    kc = kc_ref[0]
    s_cmp = [_dot_nt(qs[rows[r]], kc) for r in range(R)]
    cmp_end = lax.broadcasted_iota(jnp.int32, (1, n_cmp), 1) * CMP_STRIDE + (CMP_BLOCK - 1)
    bias_c = jnp.where(cmp_end <= q_pos, 0.0, MASK_FILL)
    o_cmp, p_sum = [], None
    for r in range(R):
        sc = s_cmp[r] + bias_c
        p = jnp.exp(sc - jnp.max(sc, axis=-1, keepdims=True))
        inv_sum = 1.0 / jnp.sum(p, axis=-1, keepdims=True)
        o_cmp.append(jnp.where(q_pos >= CMP_BLOCK - 1, _dot(p.astype(BF16), vc_ref[0]) * inv_sum, 0.0))
        p_sum = p * inv_sum if p_sum is None else p_sum + p * inv_sum
    imp = _split_dot(p_sum, c2s_ref[...])

    wk = WINDOW + blk
    ws = pl.multiple_of(jnp.maximum(q0 - WINDOW, 0), blk)
    dist = q_pos - (ws + lax.broadcasted_iota(jnp.int32, (1, wk), 1))
    bias_w = jnp.where((dist >= 0) & (dist < WINDOW), 0.0, MASK_FILL)
    k_w = kw_ref[pl.ds(ws, wk), :]
    s_w = [_dot_nt(qs[rows[r]], k_w) for r in range(R)]

    blk_id = lax.broadcasted_iota(jnp.int32, (n_sel, 1), 0)
    forced = (blk_id == 0) | (blk_id == (q_lane >> SEL_SHIFT))
    valid = blk_id * SEL_BLOCK <= q_lane
    work = jnp.where(forced, jnp.inf, jnp.where(valid, imp.T, -jnp.inf))
    blk_idf = blk_id.astype(F32)
    left = work
    o_win = []
    v1_w = with_ones(vw_ref[pl.ds(ws, wk), :])
    n_steps = min(SEL_TOPK, n_sel)
    for step in range(n_steps):
        top = jnp.max(left, axis=0, keepdims=True)
        first = jnp.min(jnp.where(left == top, blk_idf, float(n_sel)), axis=0, keepdims=True)
        left = jnp.where(blk_idf == first, -jnp.inf, left)
        while len(o_win) * n_steps < (step + 1) * R:
            o_win.append(softmax_av(s_w[len(o_win)], bias_w, v1_w))
    sel_bias = jnp.where(left != work, 0.0, MASK_FILL).T.astype(BF16)

    gates = jax.nn.sigmoid(gate_ref[...])

    def gate(r, branch):
        return gates[:, r * N_GATES + branch:r * N_GATES + branch + 1]

    o_ref[...] = jnp.concatenate([gate(r, 0) * o_cmp[r] + gate(r, 2) * o_win[r] for r in range(R)], axis=1)

    tok = lax.broadcasted_iota(jnp.int32, (n_sel, tk), 1)
    sel_row = lax.broadcasted_iota(jnp.int32, (n_sel, tk), 0)
    key_off = lax.broadcasted_iota(jnp.int32, (1, tk), 1)

    def key_rows(kb):
        return pl.ds(pl.multiple_of(kb * tk, tk), tk)

    def block_bias(kb):
        ks = kb * tk
        expand = jnp.where(((ks + tok) >> SEL_SHIFT) == sel_row, 1.0, 0.0).astype(BF16)
        bias = _dot(sel_bias, expand)
        return jnp.where(ks + key_off <= q_pos, bias, MASK_FILL)

    def put_scores(r, raw, bias):
        sc = raw + bias
        s_ref[rows[r], :] = sc
        mb_ref[rows[r], :] = jnp.max(sc, axis=-1, keepdims=True)

    def take_probs(r):
        m_old = m_ref[rows[r], :]
        m_new = jnp.maximum(m_old, mb_ref[rows[r], :])
        m_ref[rows[r], :] = m_new
        return jnp.exp(s_ref[rows[r], :] - m_new).astype(BF16), jnp.exp(m_old - m_new)

    def accumulate(r, p, alpha, v1):
        acc_ref[rows[r], :] = alpha * acc_ref[rows[r], :] + _dot(p, v1)

    m_ref[...] = jnp.full_like(m_ref, MASK_FILL)
    acc_ref[...] = jnp.zeros_like(acc_ref)
    kb_diag = qi // (tk // blk)
    bias = block_bias(0)
    for r in range(R):
        put_scores(r, _dot_nt(qs[rows[r]], ks_ref[key_rows(0), :]), bias)

    def sel_body(kb, carry):
        k_next = ks_ref[key_rows(kb + 1), :]
        bias_next = block_bias(kb + 1)
        v1 = with_ones(vs_ref[key_rows(kb), :])
        for r in range(R):
            p, alpha = take_probs(r)
            put_scores(r, _dot_nt(qs[rows[r]], k_next), bias_next)
            accumulate(r, p, alpha, v1)
        return carry

    lax.fori_loop(0, kb_diag, sel_body, 0)
    v1 = with_ones(vs_ref[key_rows(kb_diag), :])
    for r in range(R):
        p, alpha = take_probs(r)
        accumulate(r, p, alpha, v1)
    o_sel = acc_ref[:, :HEAD_DIM] / acc_ref[:, HEAD_DIM:]
    o_ref[...] += jnp.concatenate([gate(r, 1) * o_sel[rows[r]] for r in range(R)], axis=1)


def _nsa_attention(tok, k_cmp, v_cmp, gates, ks_col, kw_col, vs_col, vw_col, gate_col,
                   blk=NSA_QUERY_BLOCK, tk=NSA_KEY_BLOCK):
    T = tok.shape[0]
    G = NSA_KV_GROUPS
    blk = min(blk, T)
    tk = min(tk, T)
    assert T % tk == 0 and tk % blk == 0 and WINDOW % blk == 0 and WINDOW + blk <= T
    n_cmp = T // CMP_STRIDE
    n_sel = T // SEL_BLOCK
    cmp_start = np.arange(n_cmp)[:, None] * CMP_STRIDE
    sel_start = np.arange(n_sel)[None, :] * SEL_BLOCK
    overlap = np.clip(np.minimum(cmp_start + CMP_BLOCK, sel_start + SEL_BLOCK)
                      - np.maximum(cmp_start, sel_start), 0, None)
    c2s = jnp.asarray(overlap.astype(np.float32) / CMP_BLOCK, dtype=BF16)
    width = NSA_REP * HEAD_DIM
    tok_spec = lambda col: pl.BlockSpec((T, HEAD_DIM), lambda g, i: (0, col + g))
    cmp_spec = lambda w: pl.BlockSpec((1, n_cmp, w), lambda g, i: (g, 0, 0))
    return pl.pallas_call(
        functools.partial(_nsa_kernel, blk=blk, tk=tk),
        out_shape=jax.ShapeDtypeStruct((T, G * width), F32),
        grid=(G, T // blk),
        in_specs=[pl.BlockSpec((blk, width), lambda g, i: (i, g)),
                  cmp_spec(HEAD_DIM), cmp_spec(HEAD_DIM),
                  tok_spec(ks_col), tok_spec(vs_col), tok_spec(kw_col), tok_spec(vw_col),
                  pl.BlockSpec((blk, LANES), lambda g, i: (i, gate_col + g)),
                  pl.BlockSpec((n_cmp, n_sel), lambda g, i: (0, 0))],
        out_specs=pl.BlockSpec((blk, width), lambda g, i: (i, g)),
        scratch_shapes=[pltpu.VMEM((NSA_REP * blk, 1), F32),
                        pltpu.VMEM((NSA_REP * blk, 1), F32),
                        pltpu.VMEM((NSA_REP * blk, 2 * HEAD_DIM), F32),
                        pltpu.VMEM((NSA_REP * blk, tk), F32)],
        compiler_params=_params(("parallel", "arbitrary")),
        name="nsa_attention",
    )(tok, k_cmp, v_cmp, tok, tok, tok, tok, gates, c2s)


def _cross_kernel(q_ref, kv_ref, o_ref, *, n_heads):
    D = q_ref.shape[1]
    dh = D // n_heads
    scores = [_dot_nt(q_ref[:, h * dh:(h + 1) * dh], kv_ref[:, h * dh:(h + 1) * dh]) for h in range(n_heads)]
    for h, s in enumerate(scores):
        p = jnp.exp(s - jnp.max(s, axis=-1, keepdims=True))
        p = p / jnp.sum(p, axis=-1, keepdims=True)
        o_ref[:, h * dh:(h + 1) * dh] = _dot(p.astype(BF16), kv_ref[:, D + h * dh:D + (h + 1) * dh]).astype(o_ref.dtype)


def _cross_attention(q, kv, n_heads, tm=ROW_TILE):
    T, D = q.shape
    M = kv.shape[0]
    tm = min(tm, T)
    return pl.pallas_call(
        functools.partial(_cross_kernel, n_heads=n_heads),
        out_shape=jax.ShapeDtypeStruct((T, D), BF16),
        grid=(T // tm,),
        in_specs=[pl.BlockSpec((tm, D), lambda i: (i, 0)),
                  pl.BlockSpec((M, 2 * D), lambda i: (0, 0))],
        out_specs=pl.BlockSpec((tm, D), lambda i: (i, 0)),
        compiler_params=_params(("parallel",)),
        name="cross_attention",
    )(q, kv)


def _rope_tables(T):
    half = HEAD_DIM // 2
    inv_freq = ROPE_THETA ** (-np.arange(half, dtype=np.float64) / half)
    ang = np.arange(T, dtype=np.float64)[:, None] * inv_freq[None, :]
    cos, sin = np.cos(ang).astype(np.float32), np.sin(ang).astype(np.float32)
    return jnp.asarray(np.concatenate([cos, cos], axis=1)), jnp.asarray(np.concatenate([-sin, sin], axis=1))


def _pad_to(w, axis, mult):
    pad = -w.shape[axis] % mult
    if not pad:
        return w
    shape = list(w.shape)
    shape[axis] = pad
    return jnp.concatenate([w, jnp.zeros(shape, w.dtype)], axis=axis)


def _ffn(x_f32, x_mm, w_gate, w_up, w_down, ln_g, ln_b, alpha, emit_bf16):
    d_ff = w_gate.shape[1]
    main = d_ff // FF_TILE * FF_TILE
    assert (d_ff - main) % LANES == 0 and main % (FF_DOWN_STEPS * LANES) == 0
    h = _swiglu_up(x_mm, w_gate, w_up, FF_TILE, main // FF_TILE)
    wd = w_down.astype(BF16)
    tail = None
    if main < d_ff:
        tail = (_swiglu_up(x_mm, w_gate[:, main:], w_up[:, main:], d_ff - main, 1), wd[main:])
    return _mm_res_ln(h, wd, x_f32, ln_g, ln_b, alpha, 0.5, emit_bf16, n_k=FF_DOWN_STEPS, tail=tail)


def _mixer(x_f32, x_bf16, w_in, cmp_pos_k, cmp_w1_k, cmp_w2_k, cmp_pos_v, cmp_w1_v, cmp_w2_v,
           mix_norm_g, w_out, ln_g, ln_b, alpha):
    T = x_f32.shape[0]
    G = NSA_KV_GROUPS
    sbw = SB_HEADS * HEAD_DIM
    nqw = NSA_HEADS * HEAD_DIM
    kvw = G * HEAD_DIM
    bounds = np.cumsum([0, sbw, sbw, sbw, nqw, kvw, kvw, kvw, kvw, kvw, kvw, NSA_HEADS * N_GATES])
    w_t = w_in.T.astype(BF16)
    (w_sbq, w_sbk, w_sbv, w_nq, w_kc, w_vc, w_ks, w_vs, w_kw, w_vw, w_gate) = [
        w_t[bounds[i]:bounds[i + 1]] for i in range(11)]
    per_group = NSA_REP * N_GATES
    w_gate = jnp.concatenate([_pad_to(w_gate[g * per_group:(g + 1) * per_group], 0, LANES)
                              for g in range(G)], axis=0)
    tables = _rope_tables(T)
    cat = lambda ws: jnp.concatenate(ws, axis=0)
    q_scale = lambda n_q, n_rest: jnp.concatenate([jnp.full((n_q,), HEAD_DIM ** -0.5, F32), jnp.ones((n_rest,), F32)])
    sb_tok = _project(x_bf16, w_t, BF16, ROW_TILE, 3 * sbw // 2, n_cols=3 * sbw,
                      col_scale=q_scale(sbw, 2 * sbw), w_transposed=True)
    nsa_tok = _project(x_bf16, cat([w_nq, w_ks, w_kw, w_vs, w_vw]), BF16, ROW_TILE, nqw + 4 * kvw,
                       col_scale=q_scale(nqw, 4 * kvw), rope_tables=tables, rope_cols=nqw + 2 * kvw,
                       w_transposed=True)
    cmp_gate = _project(x_bf16, cat([w_kc, w_vc, w_gate]), F32, ROW_TILE, 2 * kvw + G * LANES,
                        rope_tables=tables, rope_cols=kvw, w_transposed=True)

    o_sb = _sb_attention(sb_tok, SB_HEADS, 0, SB_HEADS, 2 * SB_HEADS)

    head = lambda a, g: a[:, g * HEAD_DIM:(g + 1) * HEAD_DIM]
    streams = jnp.stack([head(cmp_gate, g) for g in range(2 * G)])
    rep = lambda a, b: jnp.stack([a] * G + [b] * G)
    cmp = _compress(streams, rep(cmp_pos_k, cmp_pos_v), rep(cmp_w1_k, cmp_w1_v), rep(cmp_w2_k, cmp_w2_v))
    o_nsa = _nsa_attention(nsa_tok, cmp[:G], cmp[G:], cmp_gate, ks_col=NSA_HEADS, kw_col=NSA_HEADS + G,
                           vs_col=NSA_HEADS + 2 * G, vw_col=NSA_HEADS + 3 * G, gate_col=2 * G)
    return _rms_mm_res_ln(o_sb, o_nsa, mix_norm_g, w_out.astype(BF16), x_f32, ln_g, ln_b, alpha)


def _memory_block(x_f32, x_bf16, mem, w_q, w_k, w_v, w_o, ln_g, ln_b, alpha):
    D = w_q.shape[1]
    q = _project(x_bf16, w_q.astype(BF16), BF16, ROW_TILE, D,
                 col_scale=jnp.full((D,), (D // MEM_HEADS) ** -0.5, F32))
    kv = _project(mem.astype(BF16), jnp.concatenate([w_k, w_v], axis=1).astype(BF16), BF16, ROW_TILE, D)
    o = _cross_attention(q, kv, MEM_HEADS)
    return _mm_res_ln(o, w_o.astype(BF16), x_f32, ln_g, ln_b, alpha, 1.0, True)


def kernel(x, mem, ln1_g, ln1_b, ffn1_gate, ffn1_up, ffn1_down, w_in, cmp_pos_k, cmp_w1_k, cmp_w2_k, cmp_pos_v, cmp_w1_v, cmp_w2_v, mix_norm_g, w_out, ln2_g, ln2_b, mem_wq, mem_wk, mem_wv, mem_wo, ln3_g, ln3_b, ffn2_gate, ffn2_up, ffn2_down, ln4_g, ln4_b):
    n_layers = ffn1_gate.shape[0]
    alpha = (2 * n_layers) ** 0.25
    outs = []
    for bi in range(x.shape[0]):
        xf = x[bi]
        xb = xf
        for l in range(n_layers):
            xf, xb = _ffn(xf, xb, ffn1_gate[l], ffn1_up[l], ffn1_down[l], ln1_g[l], ln1_b[l], alpha, True)
            xf, xb = _mixer(xf, xb, w_in[l], cmp_pos_k[l], cmp_w1_k[l], cmp_w2_k[l], cmp_pos_v[l], cmp_w1_v[l],
                            cmp_w2_v[l], mix_norm_g[l], w_out[l], ln2_g[l], ln2_b[l], alpha)
            xf, xb = _memory_block(xf, xb, mem[bi], mem_wq[l], mem_wk[l], mem_wv[l], mem_wo[l], ln3_g[l], ln3_b[l],
                                   alpha)
            xf, xb = _ffn(xf, xb, ffn2_gate[l], ffn2_up[l], ffn2_down[l], ln4_g[l], ln4_b[l], alpha,
                          l + 1 < n_layers)
        outs.append(xf)
    return outs[0][None] if len(outs) == 1 else jnp.stack(outs)
```

```python
import functools

import numpy as np
import jax
import jax.numpy as jnp
from jax import lax
from jax.experimental import pallas as pl
from jax.experimental.pallas import tpu as pltpu

HEAD_DIM = 128
SB_HEADS = 8
NSA_HEADS = 8
NSA_KV_GROUPS = 2
NSA_REP = NSA_HEADS // NSA_KV_GROUPS
N_GATES = 3
CMP_BLOCK = 32
CMP_STRIDE = 16
SEL_BLOCK = 64
SEL_SHIFT = 6
SEL_TOPK = 16
WINDOW = 512
MEM_HEADS = 4
ROPE_THETA = 10000.0
LN_EPS = 1e-5
RMS_EPS = 1e-6
MASK_FILL = -1e30
LOG2E = 1.4426950408889634
SB_UNDERFLOW = -110.0
assert 1 << SEL_SHIFT == SEL_BLOCK

LANES = 128
SUBLANES = 8
VMEM_LIMIT = 56 * 1024 * 1024

ROW_TILE = 1024
LN_ROW_TILE = 512
LN_ROW_GROUP = 2
FF_TILE = 512
FF_DOWN_STEPS = 4
SB_BLOCK = 256
SB_HEADS_PER_STEP = 4
NSA_QUERY_BLOCK = 512
NSA_KEY_BLOCK = 1024

BF16 = jnp.bfloat16
F32 = jnp.float32


def _params(sem):
    return pltpu.CompilerParams(dimension_semantics=sem, vmem_limit_bytes=VMEM_LIMIT)


def _dot(a, b):
    return jnp.dot(a, b, preferred_element_type=F32)


def _dot_nt(a, b):
    return lax.dot_general(a, b, (((1,), (1,)), ((), ())), preferred_element_type=F32)


def _split_dot(a, b):
    hi = a.astype(BF16)
    lo = (a - hi.astype(F32)).astype(BF16)
    return _dot(hi, b) + _dot(lo, b)


def _layer_norm(z, g, b):
    mu = jnp.mean(z, axis=-1, keepdims=True)
    zc = z - mu
    var = jnp.mean(zc * zc, axis=-1, keepdims=True)
    return zc * lax.rsqrt(var + LN_EPS) * g + b


def _proj_kernel(x_ref, w_ref, *rest, scaled, rope_heads, w_transposed):
    rest = list(rest)
    o_ref = rest.pop()
    y = (_dot_nt if w_transposed else _dot)(x_ref[...], w_ref[...])
    if scaled:
        y = y * rest.pop(0)[...]
    if rope_heads:
        cos_ref, sin_ref = rest
        c = cos_ref[...]
        s = sin_ref[...]
        heads = []
        for h in range(rope_heads):
            yh = y[:, h * HEAD_DIM:(h + 1) * HEAD_DIM]
            heads.append(yh * c + pltpu.roll(yh, HEAD_DIM // 2, 1) * s)
        if rope_heads * HEAD_DIM < y.shape[1]:
            heads.append(y[:, rope_heads * HEAD_DIM:])
        y = jnp.concatenate(heads, axis=1) if len(heads) > 1 else heads[0]
    o_ref[...] = y.astype(o_ref.dtype)


def _project(x, w, out_dtype, tm, tn, col_scale=None, rope_tables=None, rope_cols=None, w_transposed=False,
             n_cols=None):
    M, K = x.shape
    N = n_cols if n_cols is not None else (w.shape[0] if w_transposed else w.shape[1])
    tm = min(tm, M)
    assert M % tm == 0 and N % tn == 0
    rope_heads = 0 if rope_tables is None else (tn if rope_cols is None else rope_cols) // HEAD_DIM
    in_specs = [pl.BlockSpec((tm, K), lambda i, j: (i, 0)),
                pl.BlockSpec((tn, K), lambda i, j: (j, 0)) if w_transposed
                else pl.BlockSpec((K, tn), lambda i, j: (0, j))]
    args = [x, w]
    if col_scale is not None:
        in_specs.append(pl.BlockSpec((1, tn), lambda i, j: (0, j)))
        args.append(col_scale.reshape(1, N))
    if rope_tables is not None:
        in_specs += [pl.BlockSpec((tm, HEAD_DIM), lambda i, j: (i, 0))] * 2
        args += list(rope_tables)
    return pl.pallas_call(
        functools.partial(_proj_kernel, scaled=col_scale is not None, rope_heads=rope_heads,
                          w_transposed=w_transposed),
        out_shape=jax.ShapeDtypeStruct((M, N), out_dtype),
        grid=(M // tm, N // tn),
        in_specs=in_specs,
        out_specs=pl.BlockSpec((tm, tn), lambda i, j: (i, j)),
        compiler_params=_params(("parallel", "arbitrary")),
        name="proj_rope" if rope_tables is not None else "proj",
    )(*args)


def _swiglu_up_kernel(x_ref, wg_ref, wu_ref, o_ref, wg_bf16, wu_bf16):
    @pl.when(pl.program_id(1) == 0)
    def _():
        wg_bf16[...] = wg_ref[...].astype(BF16)
        wu_bf16[...] = wu_ref[...].astype(BF16)

    x = x_ref[...]
    if x.dtype != BF16:
        x = x.astype(BF16)
    g = _dot(x, wg_bf16[...])
    u = _dot(x, wu_bf16[...])
    o_ref[...] = (jax.nn.silu(g) * u).astype(o_ref.dtype)


def _swiglu_up(x, wg, wu, tn, n_blocks, tm=ROW_TILE):
    M, K = x.shape
    if x.dtype == BF16 and M % (2 * tm) == 0:
        tm *= 2
    tm = min(tm, M)
    assert M % tm == 0 and n_blocks * tn <= wg.shape[1] and wg.shape == wu.shape
    return pl.pallas_call(
        _swiglu_up_kernel,
        out_shape=jax.ShapeDtypeStruct((M, n_blocks * tn), BF16),
        grid=(n_blocks, M // tm),
        in_specs=[pl.BlockSpec((tm, K), lambda j, i: (i, 0)),
                  pl.BlockSpec((K, tn), lambda j, i: (0, j)),
                  pl.BlockSpec((K, tn), lambda j, i: (0, j))],
        out_specs=pl.BlockSpec((tm, tn), lambda j, i: (i, j)),
        scratch_shapes=[pltpu.VMEM((K, tn), BF16), pltpu.VMEM((K, tn), BF16)],
        compiler_params=_params(("parallel", "arbitrary")),
        name="swiglu_up",
    )(x, wg, wu)


def _row_chunks(tm, n_split):
    return [slice(c * tm // n_split, (c + 1) * tm // n_split) for c in range(n_split)]


def _res_ln_store(chunks, totals, res_ref, g_ref, b_ref, of_ref, ob_ref, alpha, coef):
    for c, y in zip(chunks, totals):
        out = _layer_norm(alpha * res_ref[c, :] + coef * y, g_ref[...], b_ref[...])
        of_ref[c, :] = out
        if ob_ref is not None:
            ob_ref[c, :] = out.astype(BF16)


def _mm_res_ln_kernel(*refs, alpha, coef, emit_bf16, n_k, n_split, has_tail):
    refs = list(refs)
    h_ref, w_ref = refs.pop(0), refs.pop(0)
    ht_ref, wt_ref = (refs.pop(0), refs.pop(0)) if has_tail else (None, None)
    res_ref, g_ref, b_ref, of_ref = refs.pop(0), refs.pop(0), refs.pop(0), refs.pop(0)
    ob_ref = refs.pop(0) if emit_bf16 else None
    acc_ref = refs.pop(0) if n_k > 1 else None
    chunks = _row_chunks(h_ref.shape[0], n_split)

    def partial_products():
        return [_dot(h_ref[c, :], w_ref[...]) for c in chunks]

    def finish(totals):
        if has_tail:
            totals = [y + _dot(ht_ref[c, :], wt_ref[...]) for c, y in zip(chunks, totals)]
        _res_ln_store(chunks, totals, res_ref, g_ref, b_ref, of_ref, ob_ref, alpha, coef)

    if n_k == 1:
        finish(partial_products())
        return
    k = pl.program_id(1)
    r = pl.program_id(2)

    @pl.when(k == 0)
    def _():
        for c, y in zip(chunks, partial_products()):
            acc_ref[r, c, :] = y

    @pl.when((k > 0) & (k < n_k - 1))
    def _():
        for c, y in zip(chunks, partial_products()):
            acc_ref[r, c, :] += y

    @pl.when(k == n_k - 1)
    def _():
        finish([acc_ref[r, c, :] + y for c, y in zip(chunks, partial_products())])


def _mm_res_ln(h, w, res, g, b, alpha, coef, emit_bf16, tm=LN_ROW_TILE, n_k=1, n_split=2, tail=None):
    M, K = h.shape
    N = w.shape[1]
    tm = min(tm, M)
    tk = K // n_k
    group = LN_ROW_GROUP if n_k > 1 else 1
    assert M % (tm * group) == 0 and K % n_k == 0 and K <= w.shape[0] and (n_k == 1 or tk % LANES == 0)
    late_row = lambda i, k, r: (jnp.where(k == n_k - 1, group * i + r, group * i), 0)
    row_block = pl.BlockSpec((tm, N), late_row)
    vec = pl.BlockSpec((1, N), lambda i, k, r: (0, 0))
    in_specs = [pl.BlockSpec((tm, tk), lambda i, k, r: (group * i + r, k)),
                pl.BlockSpec((tk, N), lambda i, k, r: (k, 0))]
    args = [h, w]
    if tail is not None:
        kt = tail[0].shape[1]
        in_specs += [pl.BlockSpec((tm, kt), late_row), pl.BlockSpec((kt, N), lambda i, k, r: (0, 0))]
        args += list(tail)
    out = pl.pallas_call(
        functools.partial(_mm_res_ln_kernel, alpha=alpha, coef=coef, emit_bf16=emit_bf16,
                          n_k=n_k, n_split=n_split, has_tail=tail is not None),
        out_shape=[jax.ShapeDtypeStruct((M, N), F32)] + [jax.ShapeDtypeStruct((M, N), BF16)] * emit_bf16,
        grid=(M // (tm * group), n_k, group),
        in_specs=in_specs + [row_block, vec, vec],
        out_specs=[row_block] * (1 + emit_bf16),
        scratch_shapes=[pltpu.VMEM((group, tm, N), F32)] if n_k > 1 else [],
        compiler_params=_params(("parallel", "arbitrary", "arbitrary")),
        name="mm_res_ln",
    )(*args, res, g.reshape(1, N), b.reshape(1, N))
    return (out[0], out[1]) if emit_bf16 else (out[0], None)


def _rms_mm_res_ln_kernel(a_ref, b_ref, gain_ref, w_ref, res_ref, g_ref, beta_ref, of_ref, ob_ref, *,
                          alpha, n_split):
    chunks = _row_chunks(a_ref.shape[0], n_split)
    wa = a_ref.shape[1]
    gain = gain_ref[...]

    def rms(o, gn):
        return (o * lax.rsqrt(jnp.mean(o * o, axis=-1, keepdims=True) + RMS_EPS) * gn).astype(BF16)

    totals = [_dot(jnp.concatenate([rms(a_ref[c, :], gain[:, :wa]), rms(b_ref[c, :], gain[:, wa:])], axis=1),
                   w_ref[...]) for c in chunks]
    _res_ln_store(chunks, totals, res_ref, g_ref, beta_ref, of_ref, ob_ref, alpha, 1.0)


def _rms_mm_res_ln(o_a, o_b, gain, w, res, g, b, alpha, tm=LN_ROW_TILE, n_split=2):
    M, wa = o_a.shape
    wb = o_b.shape[1]
    N = w.shape[1]
    tm = min(tm, M)
    assert M % tm == 0 and w.shape[0] == wa + wb
    row_block = pl.BlockSpec((tm, N), lambda i: (i, 0))
    vec = pl.BlockSpec((1, N), lambda i: (0, 0))
    return pl.pallas_call(
        functools.partial(_rms_mm_res_ln_kernel, alpha=alpha, n_split=n_split),
        out_shape=[jax.ShapeDtypeStruct((M, N), F32), jax.ShapeDtypeStruct((M, N), BF16)],
        grid=(M // tm,),
        in_specs=[pl.BlockSpec((tm, wa), lambda i: (i, 0)),
                  pl.BlockSpec((tm, wb), lambda i: (i, 0)),
                  pl.BlockSpec((1, wa + wb), lambda i: (0, 0)),
                  pl.BlockSpec((wa + wb, N), lambda i: (0, 0)),
                  row_block, vec, vec],
        out_specs=[row_block, row_block],
        compiler_params=_params(("parallel",)),
        name="rms_mm_res_ln",
    )(o_a, o_b, gain.reshape(1, wa + wb), w, res, g.reshape(1, N), b.reshape(1, N))


def _sb_kernel(q_ref, k_ref, v_ref, o_ref, acc_ref, c_ref, *, blk, heads):
    qi = pl.program_id(1)
    row = lax.broadcasted_iota(jnp.int32, (blk, blk), 0)
    col = lax.broadcasted_iota(jnp.int32, (blk, blk), 1)
    suffix = jnp.where(row > col, 1.0, 0.0).astype(BF16)
    strict = col < row

    cols = [slice(h * HEAD_DIM, (h + 1) * HEAD_DIM) for h in range(heads)]

    def step(kb_far, n_blocks, diagonal):
        ks = pl.multiple_of(kb_far * blk, blk)
        key_rows = [pl.ds(ks + b * blk, blk) for b in range(n_blocks)]
        log_beta, log_1m = {}, {}
        for p in [(b, h) for b in reversed(range(n_blocks)) for h in range(heads)]:
            y = _dot_nt(q_ref[:, cols[p[1]]], k_ref[key_rows[p[0]], cols[p[1]]])
            lb = jnp.minimum(y, 0.0) - jnp.log(1.0 + jnp.exp2(jnp.abs(y) * -LOG2E))
            l1 = lb - y
            if diagonal and p[0] == n_blocks - 1:
                l1 = jnp.where(strict, l1, 0.0)
            log_beta[p] = lb
            log_1m[p] = l1.astype(BF16)
        weights = {}
        for h in range(heads):
            c = c_ref[h]
            for b in reversed(range(n_blocks)):
                p = (b, h)
                log_stay = _dot(log_1m[p], suffix)
                w = jnp.exp(log_beta[p] + (log_stay + c))
                if diagonal and b == n_blocks - 1:
                    w = jnp.where(strict, w, 0.0)
                weights[p] = w.astype(BF16)
                c = c + (log_stay[:, :1] + log_1m[p][:, :1].astype(F32))
            c_ref[h] = c
        for h in range(heads):
            w = jnp.concatenate([weights[(b, h)] for b in range(n_blocks)], axis=1)
            acc_ref[:, cols[h]] += _dot(w, v_ref[pl.ds(ks, n_blocks * blk), cols[h]])

    acc_ref[...] = jnp.zeros_like(acc_ref)
    c_ref[...] = jnp.zeros_like(c_ref)

    @pl.when(qi == 0)
    def _():
        step(0, 1, True)

    @pl.when(qi > 0)
    def _():
        step(qi - 1, 2, True)

    def more(carry):
        done, c_max = carry
        return (done < qi) & (c_max >= SB_UNDERFLOW)

    def body(carry):
        done, _ = carry
        step(qi - 1 - done, 1, False)
        return done + 1, jnp.max(c_ref[...])

    lax.while_loop(more, body, (jnp.minimum(qi, 1), jnp.max(c_ref[...])))
    o_ref[...] = acc_ref[...]


def _sb_attention(qkv, n_heads, q_col, k_col, v_col, blk=SB_BLOCK, heads=SB_HEADS_PER_STEP):
    T = qkv.shape[0]
    blk = min(blk, T)
    width = heads * HEAD_DIM
    assert T % blk == 0 and n_heads % heads == 0
    assert q_col % heads == 0 and k_col % heads == 0 and v_col % heads == 0
    return pl.pallas_call(
        functools.partial(_sb_kernel, blk=blk, heads=heads),
        out_shape=jax.ShapeDtypeStruct((T, n_heads * HEAD_DIM), F32),
        grid=(n_heads // heads, T // blk),
        in_specs=[pl.BlockSpec((blk, width), lambda h, i: (i, q_col // heads + h)),
                  pl.BlockSpec((T, width), lambda h, i: (0, k_col // heads + h)),
                  pl.BlockSpec((T, width), lambda h, i: (0, v_col // heads + h))],
        out_specs=pl.BlockSpec((blk, width), lambda h, i: (i, h)),
        scratch_shapes=[pltpu.VMEM((blk, width), F32), pltpu.VMEM((heads, blk, 1), F32)],
        compiler_params=_params(("parallel", "arbitrary")),
        name="sb_attention",
    )(qkv, qkv, qkv)


def _compress_kernel(x_ref, pos_ref, w1_ref, w2_ref, o_ref):
    n = x_ref.shape[1]
    half = CMP_STRIDE * HEAD_DIM
    x = x_ref[0].astype(BF16)
    w1 = w1_ref[0].astype(BF16)
    a = _dot(x, w1[:half])
    b = _dot(x, w1[half:])
    pos = _dot(pos_ref[0].astype(BF16), w1)
    b_next = pltpu.roll(b, n - 1, 0)
    hid = jax.nn.gelu(a + b_next + pos[0:1])
    out = _dot(hid.astype(BF16), w2_ref[0].astype(BF16))
    valid = lax.broadcasted_iota(jnp.int32, out.shape, 0) < n - 1
    o_ref[0] = jnp.where(valid, out, 0.0).astype(o_ref.dtype)


def _compress(x_tok, pos_emb, w1, w2):
    S, T, _ = x_tok.shape
    n = T // CMP_STRIDE
    width = CMP_STRIDE * HEAD_DIM
    hidden = w1.shape[-1]
    x2 = x_tok.reshape(S, n, width)
    pos_flat = jnp.broadcast_to(pos_emb.reshape(S, 1, 2 * width), (S, SUBLANES, 2 * width))
    return pl.pallas_call(
        _compress_kernel,
        out_shape=jax.ShapeDtypeStruct((S, n, HEAD_DIM), BF16),
        grid=(S,),
        in_specs=[pl.BlockSpec((1, n, width), lambda s: (s, 0, 0)),
                  pl.BlockSpec((1, SUBLANES, 2 * width), lambda s: (s, 0, 0)),
                  pl.BlockSpec((1, 2 * width, hidden), lambda s: (s, 0, 0)),
                  pl.BlockSpec((1, hidden, HEAD_DIM), lambda s: (s, 0, 0))],
        out_specs=pl.BlockSpec((1, n, HEAD_DIM), lambda s: (s, 0, 0)),
        compiler_params=_params(("parallel",)),
        name="nsa_compress",
    )(x2, pos_flat, w1, w2)


def _nsa_kernel(q_ref, kc_ref, vc_ref, ks_ref, vs_ref, kw_ref, vw_ref, gate_ref, c2s_ref, o_ref,
                m_ref, mb_ref, acc_ref, s_ref, *, blk, tk):
    R = NSA_REP
    qi = pl.program_id(1)
    q0 = qi * blk
    n_cmp = kc_ref.shape[1]
    n_sel = c2s_ref.shape[1]
    rows = [slice(r * blk, (r + 1) * blk) for r in range(R)]
    qs = jnp.concatenate([q_ref[:, r * HEAD_DIM:(r + 1) * HEAD_DIM] for r in range(R)], axis=0)
    q_pos = q0 + lax.broadcasted_iota(jnp.int32, (blk, 1), 0)
    q_lane = q0 + lax.broadcasted_iota(jnp.int32, (1, blk), 1)

    def with_ones(v):
        return jnp.concatenate([v, jnp.ones_like(v)], axis=1)

    def softmax_av(s, bias, v1):
        sc = s + bias
        p = jnp.exp(sc - jnp.max(sc, axis=-1, keepdims=True)).astype(BF16)
        pv = _dot(p, v1)
        return pv[:, :HEAD_DIM] / pv[:, HEAD_DIM:]

    kc = kc_ref[0]
    s_cmp = [_dot_nt(qs[rows[r]], kc) for r in range(R)]
    cmp_end = lax.broadcasted_iota(jnp.int32, (1, n_cmp), 1) * CMP_STRIDE + (CMP_BLOCK - 1)
    bias_c = jnp.where(cmp_end <= q_pos, 0.0, MASK_FILL)
    o_cmp, p_sum = [], None
    for r in range(R):
        sc = s_cmp[r] + bias_c
        p = jnp.exp(sc - jnp.max(sc, axis=-1, keepdims=True))
        inv_sum = 1.0 / jnp.sum(p, axis=-1, keepdims=True)
        o_cmp.append(jnp.where(q_pos >= CMP_BLOCK - 1, _dot(p.astype(BF16), vc_ref[0]) * inv_sum, 0.0))
        p_sum = p * inv_sum if p_sum is None else p_sum + p * inv_sum
    imp = _split_dot(p_sum, c2s_ref[...])

    wk = WINDOW + blk
    ws = pl.multiple_of(jnp.maximum(q0 - WINDOW, 0), blk)
    dist = q_pos - (ws + lax.broadcasted_iota(jnp.int32, (1, wk), 1))
    bias_w = jnp.where((dist >= 0) & (dist < WINDOW), 0.0, MASK_FILL)
    k_w = kw_ref[pl.ds(ws, wk), :]
    s_w = [_dot_nt(qs[rows[r]], k_w) for r in range(R)]

    blk_id = lax.broadcasted_iota(jnp.int32, (n_sel, 1), 0)
    forced = (blk_id == 0) | (blk_id == (q_lane >> SEL_SHIFT))
    valid = blk_id * SEL_BLOCK <= q_lane
    work = jnp.where(forced, jnp.inf, jnp.where(valid, imp.T, -jnp.inf))
    blk_idf = blk_id.astype(F32)
    left = work
    o_win = []
    v1_w = with_ones(vw_ref[pl.ds(ws, wk), :])
    n_steps = min(SEL_TOPK, n_sel)
    for step in range(n_steps):
        top = jnp.max(left, axis=0, keepdims=True)
        first = jnp.min(jnp.where(left == top, blk_idf, float(n_sel)), axis=0, keepdims=True)
        left = jnp.where(blk_idf == first, -jnp.inf, left)
        while len(o_win) * n_steps < (step + 1) * R:
            o_win.append(softmax_av(s_w[len(o_win)], bias_w, v1_w))
    sel_bias = jnp.where(left != work, 0.0, MASK_FILL).T.astype(BF16)

    gates = jax.nn.sigmoid(gate_ref[...])

    def gate(r, branch):
        return gates[:, r * N_GATES + branch:r * N_GATES + branch + 1]

    o_ref[...] = jnp.concatenate([gate(r, 0) * o_cmp[r] + gate(r, 2) * o_win[r] for r in range(R)], axis=1)

    tok = lax.broadcasted_iota(jnp.int32, (n_sel, tk), 1)
    sel_row = lax.broadcasted_iota(jnp.int32, (n_sel, tk), 0)
    key_off = lax.broadcasted_iota(jnp.int32, (1, tk), 1)

    def key_rows(kb):
        return pl.ds(pl.multiple_of(kb * tk, tk), tk)

    def block_bias(kb):
        ks = kb * tk
        expand = jnp.where(((ks + tok) >> SEL_SHIFT) == sel_row, 1.0, 0.0).astype(BF16)
        bias = _dot(sel_bias, expand)
        return jnp.where(ks + key_off <= q_pos, bias, MASK_FILL)

    def put_scores(r, raw, bias):
        sc = raw + bias
        s_ref[rows[r], :] = sc
        mb_ref[rows[r], :] = jnp.max(sc, axis=-1, keepdims=True)

    def take_probs(r):
        m_old = m_ref[rows[r], :]
        m_new = jnp.maximum(m_old, mb_ref[rows[r], :])
        m_ref[rows[r], :] = m_new
        return jnp.exp(s_ref[rows[r], :] - m_new).astype(BF16), jnp.exp(m_old - m_new)

    def accumulate(r, p, alpha, v1):
        acc_ref[rows[r], :] = alpha * acc_ref[rows[r], :] + _dot(p, v1)

    m_ref[...] = jnp.full_like(m_ref, MASK_FILL)
    acc_ref[...] = jnp.zeros_like(acc_ref)
    kb_diag = qi // (tk // blk)
    bias = block_bias(0)
    for r in range(R):
        put_scores(r, _dot_nt(qs[rows[r]], ks_ref[key_rows(0), :]), bias)

    def sel_body(kb, carry):
        k_next = ks_ref[key_rows(kb + 1), :]
        bias_next = block_bias(kb + 1)
        v1 = with_ones(vs_ref[key_rows(kb), :])
        for r in range(R):
            p, alpha = take_probs(r)
            put_scores(r, _dot_nt(qs[rows[r]], k_next), bias_next)
            accumulate(r, p, alpha, v1)
        return carry

    lax.fori_loop(0, kb_diag, sel_body, 0)
    v1 = with_ones(vs_ref[key_rows(kb_diag), :])
    for r in range(R):
        p, alpha = take_probs(r)
        accumulate(r, p, alpha, v1)
    o_sel = acc_ref[:, :HEAD_DIM] / acc_ref[:, HEAD_DIM:]
    o_ref[...] += jnp.concatenate([gate(r, 1) * o_sel[rows[r]] for r in range(R)], axis=1)


def _nsa_attention(tok, k_cmp, v_cmp, gates, ks_col, kw_col, vs_col, vw_col, gate_col,
                   blk=NSA_QUERY_BLOCK, tk=NSA_KEY_BLOCK):
    T = tok.shape[0]
    G = NSA_KV_GROUPS
    blk = min(blk, T)
    tk = min(tk, T)
    assert T % tk == 0 and tk % blk == 0 and WINDOW % blk == 0 and WINDOW + blk <= T
    n_cmp = T // CMP_STRIDE
    n_sel = T // SEL_BLOCK
    cmp_start = np.arange(n_cmp)[:, None] * CMP_STRIDE
    sel_start = np.arange(n_sel)[None, :] * SEL_BLOCK
    overlap = np.clip(np.minimum(cmp_start + CMP_BLOCK, sel_start + SEL_BLOCK)
                      - np.maximum(cmp_start, sel_start), 0, None)
    c2s = jnp.asarray(overlap.astype(np.float32) / CMP_BLOCK, dtype=BF16)
    width = NSA_REP * HEAD_DIM
    tok_spec = lambda col: pl.BlockSpec((T, HEAD_DIM), lambda g, i: (0, col + g))
    cmp_spec = lambda w: pl.BlockSpec((1, n_cmp, w), lambda g, i: (g, 0, 0))
    return pl.pallas_call(
        functools.partial(_nsa_kernel, blk=blk, tk=tk),
        out_shape=jax.ShapeDtypeStruct((T, G * width), F32),
        grid=(G, T // blk),
        in_specs=[pl.BlockSpec((blk, width), lambda g, i: (i, g)),
                  cmp_spec(HEAD_DIM), cmp_spec(HEAD_DIM),
                  tok_spec(ks_col), tok_spec(vs_col), tok_spec(kw_col), tok_spec(vw_col),
                  pl.BlockSpec((blk, LANES), lambda g, i: (i, gate_col + g)),
                  pl.BlockSpec((n_cmp, n_sel), lambda g, i: (0, 0))],
        out_specs=pl.BlockSpec((blk, width), lambda g, i: (i, g)),
        scratch_shapes=[pltpu.VMEM((NSA_REP * blk, 1), F32),
                        pltpu.VMEM((NSA_REP * blk, 1), F32),
                        pltpu.VMEM((NSA_REP * blk, 2 * HEAD_DIM), F32),
                        pltpu.VMEM((NSA_REP * blk, tk), F32)],
        compiler_params=_params(("parallel", "arbitrary")),
        name="nsa_attention",
    )(tok, k_cmp, v_cmp, tok, tok, tok, tok, gates, c2s)


def _cross_kernel(q_ref, kv_ref, o_ref, *, n_heads):
    D = q_ref.shape[1]
    dh = D // n_heads
    scores = [_dot_nt(q_ref[:, h * dh:(h + 1) * dh], kv_ref[:, h * dh:(h + 1) * dh]) for h in range(n_heads)]
    for h, s in enumerate(scores):
        p = jnp.exp(s - jnp.max(s, axis=-1, keepdims=True))
        p = p / jnp.sum(p, axis=-1, keepdims=True)
        o_ref[:, h * dh:(h + 1) * dh] = _dot(p.astype(BF16), kv_ref[:, D + h * dh:D + (h + 1) * dh]).astype(o_ref.dtype)


def _cross_attention(q, kv, n_heads, tm=ROW_TILE):
    T, D = q.shape
    M = kv.shape[0]
    tm = min(tm, T)
    return pl.pallas_call(
        functools.partial(_cross_kernel, n_heads=n_heads),
        out_shape=jax.ShapeDtypeStruct((T, D), BF16),
        grid=(T // tm,),
        in_specs=[pl.BlockSpec((tm, D), lambda i: (i, 0)),
                  pl.BlockSpec((M, 2 * D), lambda i: (0, 0))],
        out_specs=pl.BlockSpec((tm, D), lambda i: (i, 0)),
        compiler_params=_params(("parallel",)),
        name="cross_attention",
    )(q, kv)


def _rope_tables(T):
    half = HEAD_DIM // 2
    inv_freq = ROPE_THETA ** (-np.arange(half, dtype=np.float64) / half)
    ang = np.arange(T, dtype=np.float64)[:, None] * inv_freq[None, :]
    cos, sin = np.cos(ang).astype(np.float32), np.sin(ang).astype(np.float32)
    return jnp.asarray(np.concatenate([cos, cos], axis=1)), jnp.asarray(np.concatenate([-sin, sin], axis=1))


def _pad_to(w, axis, mult):
    pad = -w.shape[axis] % mult
    if not pad:
        return w
    shape = list(w.shape)
    shape[axis] = pad
    return jnp.concatenate([w, jnp.zeros(shape, w.dtype)], axis=axis)


def _ffn(x_f32, x_mm, w_gate, w_up, w_down, ln_g, ln_b, alpha, emit_bf16):
    d_ff = w_gate.shape[1]
    main = d_ff // FF_TILE * FF_TILE
    assert (d_ff - main) % LANES == 0 and main % (FF_DOWN_STEPS * LANES) == 0
    h = _swiglu_up(x_mm, w_gate, w_up, FF_TILE, main // FF_TILE)
    wd = w_down.astype(BF16)
    tail = None
    if main < d_ff:
        tail = (_swiglu_up(x_mm, w_gate[:, main:], w_up[:, main:], d_ff - main, 1), wd[main:])
    return _mm_res_ln(h, wd, x_f32, ln_g, ln_b, alpha, 0.5, emit_bf16, n_k=FF_DOWN_STEPS, tail=tail)


def _mixer(x_f32, x_bf16, w_in, cmp_pos_k, cmp_w1_k, cmp_w2_k, cmp_pos_v, cmp_w1_v, cmp_w2_v,
           mix_norm_g, w_out, ln_g, ln_b, alpha):
    T = x_f32.shape[0]
    G = NSA_KV_GROUPS
    sbw = SB_HEADS * HEAD_DIM
    nqw = NSA_HEADS * HEAD_DIM
    kvw = G * HEAD_DIM
    bounds = np.cumsum([0, sbw, sbw, sbw, nqw, kvw, kvw, kvw, kvw, kvw, kvw, NSA_HEADS * N_GATES])
    w_t = w_in.T.astype(BF16)
    (w_sbq, w_sbk, w_sbv, w_nq, w_kc, w_vc, w_ks, w_vs, w_kw, w_vw, w_gate) = [
        w_t[bounds[i]:bounds[i + 1]] for i in range(11)]
    per_group = NSA_REP * N_GATES
    w_gate = jnp.concatenate([_pad_to(w_gate[g * per_group:(g + 1) * per_group], 0, LANES)
                              for g in range(G)], axis=0)
    tables = _rope_tables(T)
    cat = lambda ws: jnp.concatenate(ws, axis=0)
    q_scale = lambda n_q, n_rest: jnp.concatenate([jnp.full((n_q,), HEAD_DIM ** -0.5, F32), jnp.ones((n_rest,), F32)])
    sb_tok = _project(x_bf16, w_t, BF16, ROW_TILE, 3 * sbw // 2, n_cols=3 * sbw,
                      col_scale=q_scale(sbw, 2 * sbw), w_transposed=True)
    nsa_tok = _project(x_bf16, cat([w_nq, w_ks, w_kw, w_vs, w_vw]), BF16, ROW_TILE, nqw + 4 * kvw,
                       col_scale=q_scale(nqw, 4 * kvw), rope_tables=tables, rope_cols=nqw + 2 * kvw,
                       w_transposed=True)
    cmp_gate = _project(x_bf16, cat([w_kc, w_vc, w_gate]), F32, ROW_TILE, 2 * kvw + G * LANES,
                        rope_tables=tables, rope_cols=kvw, w_transposed=True)

    o_sb = _sb_attention(sb_tok, SB_HEADS, 0, SB_HEADS, 2 * SB_HEADS)

    head = lambda a, g: a[:, g * HEAD_DIM:(g + 1) * HEAD_DIM]
    streams = jnp.stack([head(cmp_gate, g) for g in range(2 * G)])
    rep = lambda a, b: jnp.stack([a] * G + [b] * G)
    cmp = _compress(streams, rep(cmp_pos_k, cmp_pos_v), rep(cmp_w1_k, cmp_w1_v), rep(cmp_w2_k, cmp_w2_v))
    o_nsa = _nsa_attention(nsa_tok, cmp[:G], cmp[G:], cmp_gate, ks_col=NSA_HEADS, kw_col=NSA_HEADS + G,
                           vs_col=NSA_HEADS + 2 * G, vw_col=NSA_HEADS + 3 * G, gate_col=2 * G)
    return _rms_mm_res_ln(o_sb, o_nsa, mix_norm_g, w_out.astype(BF16), x_f32, ln_g, ln_b, alpha)


def _memory_block(x_f32, x_bf16, mem, w_q, w_k, w_v, w_o, ln_g, ln_b, alpha):
    D = w_q.shape[1]
    q = _project(x_bf16, w_q.astype(BF16), BF16, ROW_TILE, D,
                 col_scale=jnp.full((D,), (D // MEM_HEADS) ** -0.5, F32))
    kv = _project(mem.astype(BF16), jnp.concatenate([w_k, w_v], axis=1).astype(BF16), BF16, ROW_TILE, D)
    o = _cross_attention(q, kv, MEM_HEADS)
    return _mm_res_ln(o, w_o.astype(BF16), x_f32, ln_g, ln_b, alpha, 1.0, True)


def kernel(x, mem, ln1_g, ln1_b, ffn1_gate, ffn1_up, ffn1_down, w_in, cmp_pos_k, cmp_w1_k, cmp_w2_k, cmp_pos_v, cmp_w1_v, cmp_w2_v, mix_norm_g, w_out, ln2_g, ln2_b, mem_wq, mem_wk, mem_wv, mem_wo, ln3_g, ln3_b, ffn2_gate, ffn2_up, ffn2_down, ln4_g, ln4_b):
    n_layers = ffn1_gate.shape[0]
    alpha = (2 * n_layers) ** 0.25
    outs = []
    for bi in range(x.shape[0]):
        xf = x[bi]
        xb = xf
        for l in range(n_layers):
            xf, xb = _ffn(xf, xb, ffn1_gate[l], ffn1_up[l], ffn1_down[l], ln1_g[l], ln1_b[l], alpha, True)
            xf, xb = _mixer(xf, xb, w_in[l], cmp_pos_k[l], cmp_w1_k[l], cmp_w2_k[l], cmp_pos_v[l], cmp_w1_v[l],
                            cmp_w2_v[l], mix_norm_g[l], w_out[l], ln2_g[l], ln2_b[l], alpha)
            xf, xb = _memory_block(xf, xb, mem[bi], mem_wq[l], mem_wk[l], mem_wv[l], mem_wo[l], ln3_g[l], ln3_b[l],
                                   alpha)
            xf, xb = _ffn(xf, xb, ffn2_gate[l], ffn2_up[l], ffn2_down[l], ln4_g[l], ln4_b[l], alpha,
                          l + 1 < n_layers)
        outs.append(xf)
    return outs[0][None] if len(outs) == 1 else jnp.stack(outs)
```

```python
import functools

import numpy as np
import jax
import jax.numpy as jnp
from jax import lax
from jax.experimental import pallas as pl
from jax.experimental.pallas import tpu as pltpu

HEAD_DIM = 128
SB_HEADS = 8
NSA_HEADS = 8
NSA_KV_GROUPS = 2
NSA_REP = NSA_HEADS // NSA_KV_GROUPS
N_GATES = 3
CMP_BLOCK = 32
CMP_STRIDE = 16
SEL_BLOCK = 64
SEL_SHIFT = 6
SEL_TOPK = 16
WINDOW = 512
MEM_HEADS = 4
ROPE_THETA = 10000.0
LN_EPS = 1e-5
RMS_EPS = 1e-6
MASK_FILL = -1e30
LOG2E = 1.4426950408889634
SB_UNDERFLOW = -110.0
assert 1 << SEL_SHIFT == SEL_BLOCK

LANES = 128
SUBLANES = 8
VMEM_LIMIT = 56 * 1024 * 1024

ROW_TILE = 1024
LN_ROW_TILE = 512
LN_ROW_GROUP = 2
FF_TILE = 512
FF_DOWN_STEPS = 4
SB_BLOCK = 256
SB_HEADS_PER_STEP = 4
NSA_QUERY_BLOCK = 512
NSA_KEY_BLOCK = 1024

BF16 = jnp.bfloat16
F32 = jnp.float32


def _params(sem):
    return pltpu.CompilerParams(dimension_semantics=sem, vmem_limit_bytes=VMEM_LIMIT)


def _dot(a, b):
    return jnp.dot(a, b, preferred_element_type=F32)


def _dot_nt(a, b):
    return lax.dot_general(a, b, (((1,), (1,)), ((), ())), preferred_element_type=F32)


def _split_dot(a, b):
    hi = a.astype(BF16)
    lo = (a - hi.astype(F32)).astype(BF16)
    return _dot(hi, b) + _dot(lo, b)


def _layer_norm(z, g, b):
    mu = jnp.mean(z, axis=-1, keepdims=True)
    zc = z - mu
    var = jnp.mean(zc * zc, axis=-1, keepdims=True)
    return zc * lax.rsqrt(var + LN_EPS) * g + b


def _proj_kernel(x_ref, w_ref, *rest, scaled, rope_heads, w_transposed):
    rest = list(rest)
    o_ref = rest.pop()
    y = (_dot_nt if w_transposed else _dot)(x_ref[...], w_ref[...])
    if scaled:
        y = y * rest.pop(0)[...]
    if rope_heads:
        cos_ref, sin_ref = rest
        c = cos_ref[...]
        s = sin_ref[...]
        heads = []
        for h in range(rope_heads):
            yh = y[:, h * HEAD_DIM:(h + 1) * HEAD_DIM]
            heads.append(yh * c + pltpu.roll(yh, HEAD_DIM // 2, 1) * s)
        if rope_heads * HEAD_DIM < y.shape[1]:
            heads.append(y[:, rope_heads * HEAD_DIM:])
        y = jnp.concatenate(heads, axis=1) if len(heads) > 1 else heads[0]
    o_ref[...] = y.astype(o_ref.dtype)


def _project(x, w, out_dtype, tm, tn, col_scale=None, rope_tables=None, rope_cols=None, w_transposed=False,
             n_cols=None):
    M, K = x.shape
    N = n_cols if n_cols is not None else (w.shape[0] if w_transposed else w.shape[1])
    tm = min(tm, M)
    assert M % tm == 0 and N % tn == 0
    rope_heads = 0 if rope_tables is None else (tn if rope_cols is None else rope_cols) // HEAD_DIM
    in_specs = [pl.BlockSpec((tm, K), lambda i, j: (i, 0)),
                pl.BlockSpec((tn, K), lambda i, j: (j, 0)) if w_transposed
                else pl.BlockSpec((K, tn), lambda i, j: (0, j))]
    args = [x, w]
    if col_scale is not None:
        in_specs.append(pl.BlockSpec((1, tn), lambda i, j: (0, j)))
        args.append(col_scale.reshape(1, N))
    if rope_tables is not None:
        in_specs += [pl.BlockSpec((tm, HEAD_DIM), lambda i, j: (i, 0))] * 2
        args += list(rope_tables)
    return pl.pallas_call(
        functools.partial(_proj_kernel, scaled=col_scale is not None, rope_heads=rope_heads,
                          w_transposed=w_transposed),
        out_shape=jax.ShapeDtypeStruct((M, N), out_dtype),
        grid=(M // tm, N // tn),
        in_specs=in_specs,
        out_specs=pl.BlockSpec((tm, tn), lambda i, j: (i, j)),
        compiler_params=_params(("parallel", "arbitrary")),
        name="proj_rope" if rope_tables is not None else "proj",
    )(*args)


def _swiglu_up_kernel(x_ref, wg_ref, wu_ref, o_ref, wg_bf16, wu_bf16):
    @pl.when(pl.program_id(1) == 0)
    def _():
        wg_bf16[...] = wg_ref[...].astype(BF16)
        wu_bf16[...] = wu_ref[...].astype(BF16)

    x = x_ref[...]
    if x.dtype != BF16:
        x = x.astype(BF16)
    g = _dot(x, wg_bf16[...])
    u = _dot(x, wu_bf16[...])
    o_ref[...] = (jax.nn.silu(g) * u).astype(o_ref.dtype)


def _swiglu_up(x, wg, wu, tn, n_blocks, tm=ROW_TILE):
    M, K = x.shape
    tm = min(tm, M)
    assert M % tm == 0 and n_blocks * tn <= wg.shape[1] and wg.shape == wu.shape
    return pl.pallas_call(
        _swiglu_up_kernel,
        out_shape=jax.ShapeDtypeStruct((M, n_blocks * tn), BF16),
        grid=(n_blocks, M // tm),
        in_specs=[pl.BlockSpec((tm, K), lambda j, i: (i, 0)),
                  pl.BlockSpec((K, tn), lambda j, i: (0, j)),
                  pl.BlockSpec((K, tn), lambda j, i: (0, j))],
        out_specs=pl.BlockSpec((tm, tn), lambda j, i: (i, j)),
        scratch_shapes=[pltpu.VMEM((K, tn), BF16), pltpu.VMEM((K, tn), BF16)],
        compiler_params=_params(("parallel", "arbitrary")),
        name="swiglu_up",
    )(x, wg, wu)


def _row_chunks(tm, n_split):
    return [slice(c * tm // n_split, (c + 1) * tm // n_split) for c in range(n_split)]


def _res_ln_store(chunks, totals, res_ref, g_ref, b_ref, of_ref, ob_ref, alpha, coef):
    for c, y in zip(chunks, totals):
        out = _layer_norm(alpha * res_ref[c, :] + coef * y, g_ref[...], b_ref[...])
        of_ref[c, :] = out
        if ob_ref is not None:
            ob_ref[c, :] = out.astype(BF16)


def _mm_res_ln_kernel(*refs, alpha, coef, emit_bf16, n_k, n_split, has_tail):
    refs = list(refs)
    h_ref, w_ref = refs.pop(0), refs.pop(0)
    ht_ref, wt_ref = (refs.pop(0), refs.pop(0)) if has_tail else (None, None)
    res_ref, g_ref, b_ref, of_ref = refs.pop(0), refs.pop(0), refs.pop(0), refs.pop(0)
    ob_ref = refs.pop(0) if emit_bf16 else None
    acc_ref = refs.pop(0) if n_k > 1 else None
    chunks = _row_chunks(h_ref.shape[0], n_split)

    def partial_products():
        return [_dot(h_ref[c, :], w_ref[...]) for c in chunks]

    def finish(totals):
        if has_tail:
            totals = [y + _dot(ht_ref[c, :], wt_ref[...]) for c, y in zip(chunks, totals)]
        _res_ln_store(chunks, totals, res_ref, g_ref, b_ref, of_ref, ob_ref, alpha, coef)

    if n_k == 1:
        finish(partial_products())
        return
    k = pl.program_id(1)
    r = pl.program_id(2)

    @pl.when(k == 0)
    def _():
        for c, y in zip(chunks, partial_products()):
            acc_ref[r, c, :] = y

    @pl.when((k > 0) & (k < n_k - 1))
    def _():
        for c, y in zip(chunks, partial_products()):
            acc_ref[r, c, :] += y

    @pl.when(k == n_k - 1)
    def _():
        finish([acc_ref[r, c, :] + y for c, y in zip(chunks, partial_products())])


def _mm_res_ln(h, w, res, g, b, alpha, coef, emit_bf16, tm=LN_ROW_TILE, n_k=1, n_split=2, tail=None):
    M, K = h.shape
    N = w.shape[1]
    tm = min(tm, M)
    tk = K // n_k
    group = LN_ROW_GROUP if n_k > 1 else 1
    assert M % (tm * group) == 0 and K % n_k == 0 and K <= w.shape[0] and (n_k == 1 or tk % LANES == 0)
    late_row = lambda i, k, r: (jnp.where(k == n_k - 1, group * i + r, group * i), 0)
    row_block = pl.BlockSpec((tm, N), late_row)
    vec = pl.BlockSpec((1, N), lambda i, k, r: (0, 0))
    in_specs = [pl.BlockSpec((tm, tk), lambda i, k, r: (group * i + r, k)),
                pl.BlockSpec((tk, N), lambda i, k, r: (k, 0))]
    args = [h, w]
    if tail is not None:
        kt = tail[0].shape[1]
        in_specs += [pl.BlockSpec((tm, kt), late_row), pl.BlockSpec((kt, N), lambda i, k, r: (0, 0))]
        args += list(tail)
    out = pl.pallas_call(
        functools.partial(_mm_res_ln_kernel, alpha=alpha, coef=coef, emit_bf16=emit_bf16,
                          n_k=n_k, n_split=n_split, has_tail=tail is not None),
        out_shape=[jax.ShapeDtypeStruct((M, N), F32)] + [jax.ShapeDtypeStruct((M, N), BF16)] * emit_bf16,
        grid=(M // (tm * group), n_k, group),
        in_specs=in_specs + [row_block, vec, vec],
        out_specs=[row_block] * (1 + emit_bf16),
        scratch_shapes=[pltpu.VMEM((group, tm, N), F32)] if n_k > 1 else [],
        compiler_params=_params(("parallel", "arbitrary", "arbitrary")),
        name="mm_res_ln",
    )(*args, res, g.reshape(1, N), b.reshape(1, N))
    return (out[0], out[1]) if emit_bf16 else (out[0], None)


def _rms_mm_res_ln_kernel(a_ref, b_ref, gain_ref, w_ref, res_ref, g_ref, beta_ref, of_ref, ob_ref, *,
                          alpha, n_split):
    chunks = _row_chunks(a_ref.shape[0], n_split)
    wa = a_ref.shape[1]
    gain = gain_ref[...]

    def rms(o, gn):
        return (o * lax.rsqrt(jnp.mean(o * o, axis=-1, keepdims=True) + RMS_EPS) * gn).astype(BF16)

    totals = [_dot(jnp.concatenate([rms(a_ref[c, :], gain[:, :wa]), rms(b_ref[c, :], gain[:, wa:])], axis=1),
                   w_ref[...]) for c in chunks]
    _res_ln_store(chunks, totals, res_ref, g_ref, beta_ref, of_ref, ob_ref, alpha, 1.0)


def _rms_mm_res_ln(o_a, o_b, gain, w, res, g, b, alpha, tm=LN_ROW_TILE, n_split=2):
    M, wa = o_a.shape
    wb = o_b.shape[1]
    N = w.shape[1]
    tm = min(tm, M)
    assert M % tm == 0 and w.shape[0] == wa + wb
    row_block = pl.BlockSpec((tm, N), lambda i: (i, 0))
    vec = pl.BlockSpec((1, N), lambda i: (0, 0))
    return pl.pallas_call(
        functools.partial(_rms_mm_res_ln_kernel, alpha=alpha, n_split=n_split),
        out_shape=[jax.ShapeDtypeStruct((M, N), F32), jax.ShapeDtypeStruct((M, N), BF16)],
        grid=(M // tm,),
        in_specs=[pl.BlockSpec((tm, wa), lambda i: (i, 0)),
                  pl.BlockSpec((tm, wb), lambda i: (i, 0)),
                  pl.BlockSpec((1, wa + wb), lambda i: (0, 0)),
                  pl.BlockSpec((wa + wb, N), lambda i: (0, 0)),
                  row_block, vec, vec],
        out_specs=[row_block, row_block],
        compiler_params=_params(("parallel",)),
        name="rms_mm_res_ln",
    )(o_a, o_b, gain.reshape(1, wa + wb), w, res, g.reshape(1, N), b.reshape(1, N))


def _sb_kernel(q_ref, k_ref, v_ref, o_ref, acc_ref, c_ref, *, blk, heads):
    qi = pl.program_id(1)
    row = lax.broadcasted_iota(jnp.int32, (blk, blk), 0)
    col = lax.broadcasted_iota(jnp.int32, (blk, blk), 1)
    suffix = jnp.where(row > col, 1.0, 0.0).astype(BF16)
    strict = col < row

    cols = [slice(h * HEAD_DIM, (h + 1) * HEAD_DIM) for h in range(heads)]

    def step(kb_far, n_blocks, diagonal):
        ks = pl.multiple_of(kb_far * blk, blk)
        key_rows = [pl.ds(ks + b * blk, blk) for b in range(n_blocks)]
        log_beta, log_1m = {}, {}
        for p in [(b, h) for b in reversed(range(n_blocks)) for h in range(heads)]:
            y = _dot_nt(q_ref[:, cols[p[1]]], k_ref[key_rows[p[0]], cols[p[1]]])
            lb = jnp.minimum(y, 0.0) - jnp.log(1.0 + jnp.exp2(jnp.abs(y) * -LOG2E))
            l1 = lb - y
            if diagonal and p[0] == n_blocks - 1:
                l1 = jnp.where(strict, l1, 0.0)
            log_beta[p] = lb
            log_1m[p] = l1.astype(BF16)
        weights = {}
        for h in range(heads):
            c = c_ref[h]
            for b in reversed(range(n_blocks)):
                p = (b, h)
                log_stay = _dot(log_1m[p], suffix)
                w = jnp.exp(log_beta[p] + (log_stay + c))
                if diagonal and b == n_blocks - 1:
                    w = jnp.where(strict, w, 0.0)
                weights[p] = w.astype(BF16)
                c = c + (log_stay[:, :1] + log_1m[p][:, :1].astype(F32))
            c_ref[h] = c
        for h in range(heads):
            w = jnp.concatenate([weights[(b, h)] for b in range(n_blocks)], axis=1)
            acc_ref[:, cols[h]] += _dot(w, v_ref[pl.ds(ks, n_blocks * blk), cols[h]])

    acc_ref[...] = jnp.zeros_like(acc_ref)
    c_ref[...] = jnp.zeros_like(c_ref)

    @pl.when(qi == 0)
    def _():
        step(0, 1, True)

    @pl.when(qi > 0)
    def _():
        step(qi - 1, 2, True)

    def more(carry):
        done, c_max = carry
        return (done < qi) & (c_max >= SB_UNDERFLOW)

    def body(carry):
        done, _ = carry
        step(qi - 1 - done, 1, False)
        return done + 1, jnp.max(c_ref[...])

    lax.while_loop(more, body, (jnp.minimum(qi, 1), jnp.max(c_ref[...])))
    o_ref[...] = acc_ref[...]


def _sb_attention(qkv, n_heads, q_col, k_col, v_col, blk=SB_BLOCK, heads=SB_HEADS_PER_STEP):
    T = qkv.shape[0]
    blk = min(blk, T)
    width = heads * HEAD_DIM
    assert T % blk == 0 and n_heads % heads == 0
    assert q_col % heads == 0 and k_col % heads == 0 and v_col % heads == 0
    return pl.pallas_call(
        functools.partial(_sb_kernel, blk=blk, heads=heads),
        out_shape=jax.ShapeDtypeStruct((T, n_heads * HEAD_DIM), F32),
        grid=(n_heads // heads, T // blk),
        in_specs=[pl.BlockSpec((blk, width), lambda h, i: (i, q_col // heads + h)),
                  pl.BlockSpec((T, width), lambda h, i: (0, k_col // heads + h)),
                  pl.BlockSpec((T, width), lambda h, i: (0, v_col // heads + h))],
        out_specs=pl.BlockSpec((blk, width), lambda h, i: (i, h)),
        scratch_shapes=[pltpu.VMEM((blk, width), F32), pltpu.VMEM((heads, blk, 1), F32)],
        compiler_params=_params(("parallel", "arbitrary")),
        name="sb_attention",
    )(qkv, qkv, qkv)


def _compress_kernel(x_ref, pos_ref, w1_ref, w2_ref, o_ref):
    n = x_ref.shape[0] // CMP_STRIDE
    half = CMP_STRIDE * HEAD_DIM
    w1 = w1_ref[0].astype(BF16)
    a = b = None
    for l in range(CMP_STRIDE):
        xl = x_ref[pl.ds(l, n, stride=CMP_STRIDE), :].astype(BF16)
        al = _dot(xl, w1[l * HEAD_DIM:(l + 1) * HEAD_DIM])
        bl = _dot(xl, w1[half + l * HEAD_DIM:half + (l + 1) * HEAD_DIM])
        a = al if a is None else a + al
        b = bl if b is None else b + bl
    pos = _dot(pos_ref[0].astype(BF16), w1)
    b_next = pltpu.roll(b, n - 1, 0)
    hid = jax.nn.gelu(a + b_next + pos[0:1])
    out = _dot(hid.astype(BF16), w2_ref[0].astype(BF16))
    valid = lax.broadcasted_iota(jnp.int32, out.shape, 0) < n - 1
    o_ref[0] = jnp.where(valid, out, 0.0).astype(o_ref.dtype)


def _compress(x_tok, pos_emb, w1, w2):
    S = pos_emb.shape[0]
    T = x_tok.shape[0]
    n = T // CMP_STRIDE
    width = CMP_STRIDE * HEAD_DIM
    hidden = w1.shape[-1]
    pos_flat = jnp.broadcast_to(pos_emb.reshape(S, 1, 2 * width), (S, SUBLANES, 2 * width))
    return pl.pallas_call(
        _compress_kernel,
        out_shape=jax.ShapeDtypeStruct((S, n, HEAD_DIM), BF16),
        grid=(S,),
        in_specs=[pl.BlockSpec((T, HEAD_DIM), lambda s: (0, s)),
                  pl.BlockSpec((1, SUBLANES, 2 * width), lambda s: (s, 0, 0)),
                  pl.BlockSpec((1, 2 * width, hidden), lambda s: (s, 0, 0)),
                  pl.BlockSpec((1, hidden, HEAD_DIM), lambda s: (s, 0, 0))],
        out_specs=pl.BlockSpec((1, n, HEAD_DIM), lambda s: (s, 0, 0)),
        compiler_params=_params(("parallel",)),
        name="nsa_compress",
    )(x_tok, pos_flat, w1, w2)


def _nsa_kernel(q_ref, kc_ref, vc_ref, ks_ref, vs_ref, kw_ref, vw_ref, gate_ref, c2s_ref, o_ref,
                m_ref, mb_ref, acc_ref, s_ref, *, blk, tk):
    R = NSA_REP
    qi = pl.program_id(1)
    q0 = qi * blk
    n_cmp = kc_ref.shape[1]
    n_sel = c2s_ref.shape[1]
    rows = [slice(r * blk, (r + 1) * blk) for r in range(R)]
    qs = jnp.concatenate([q_ref[:, r * HEAD_DIM:(r + 1) * HEAD_DIM] for r in range(R)], axis=0)
    q_pos = q0 + lax.broadcasted_iota(jnp.int32, (blk, 1), 0)
    q_lane = q0 + lax.broadcasted_iota(jnp.int32, (1, blk), 1)

    def with_ones(v):
        return jnp.concatenate([v, jnp.ones_like(v)], axis=1)

    def softmax_av(s, bias, v1):
        sc = s + bias
        p = jnp.exp(sc - jnp.max(sc, axis=-1, keepdims=True)).astype(BF16)
        pv = _dot(p, v1)
        return pv[:, :HEAD_DIM] / pv[:, HEAD_DIM:]

    kc = kc_ref[0]
    s_cmp = [_dot_nt(qs[rows[r]], kc) for r in range(R)]
    cmp_end = lax.broadcasted_iota(jnp.int32, (1, n_cmp), 1) * CMP_STRIDE + (CMP_BLOCK - 1)
    bias_c = jnp.where(cmp_end <= q_pos, 0.0, MASK_FILL)
    o_cmp, p_sum = [], None
    for r in range(R):
        sc = s_cmp[r] + bias_c
        p = jnp.exp(sc - jnp.max(sc, axis=-1, keepdims=True))
        inv_sum = 1.0 / jnp.sum(p, axis=-1, keepdims=True)
        o_cmp.append(jnp.where(q_pos >= CMP_BLOCK - 1, _dot(p.astype(BF16), vc_ref[0]) * inv_sum, 0.0))
        p_sum = p * inv_sum if p_sum is None else p_sum + p * inv_sum
    imp = _split_dot(p_sum, c2s_ref[...])

    wk = WINDOW + blk
    ws = pl.multiple_of(jnp.maximum(q0 - WINDOW, 0), blk)
    dist = q_pos - (ws + lax.broadcasted_iota(jnp.int32, (1, wk), 1))
    bias_w = jnp.where((dist >= 0) & (dist < WINDOW), 0.0, MASK_FILL)
    k_w = kw_ref[pl.ds(ws, wk), :]
    s_w = [_dot_nt(qs[rows[r]], k_w) for r in range(R)]

    blk_id = lax.broadcasted_iota(jnp.int32, (n_sel, 1), 0)
    forced = (blk_id == 0) | (blk_id == (q_lane >> SEL_SHIFT))
    valid = blk_id * SEL_BLOCK <= q_lane
    work = jnp.where(forced, jnp.inf, jnp.where(valid, imp.T, -jnp.inf))
    blk_idf = blk_id.astype(F32)
    left = work
    o_win = []
    v1_w = with_ones(vw_ref[pl.ds(ws, wk), :])
    n_steps = min(SEL_TOPK, n_sel)
    for step in range(n_steps):
        top = jnp.max(left, axis=0, keepdims=True)
        first = jnp.min(jnp.where(left == top, blk_idf, float(n_sel)), axis=0, keepdims=True)
        left = jnp.where(blk_idf == first, -jnp.inf, left)
        while len(o_win) * n_steps < (step + 1) * R:
            o_win.append(softmax_av(s_w[len(o_win)], bias_w, v1_w))
    sel_bias = jnp.where(left != work, 0.0, MASK_FILL).T.astype(BF16)

    gates = jax.nn.sigmoid(gate_ref[...])

    def gate(r, branch):
        return gates[:, r * N_GATES + branch:r * N_GATES + branch + 1]

    o_ref[...] = jnp.concatenate([gate(r, 0) * o_cmp[r] + gate(r, 2) * o_win[r] for r in range(R)], axis=1)

    tok = lax.broadcasted_iota(jnp.int32, (n_sel, tk), 1)
    sel_row = lax.broadcasted_iota(jnp.int32, (n_sel, tk), 0)
    key_off = lax.broadcasted_iota(jnp.int32, (1, tk), 1)

    def key_rows(kb):
        return pl.ds(pl.multiple_of(kb * tk, tk), tk)

    def block_bias(kb):
        ks = kb * tk
        expand = jnp.where(((ks + tok) >> SEL_SHIFT) == sel_row, 1.0, 0.0).astype(BF16)
        bias = _dot(sel_bias, expand)
        return jnp.where(ks + key_off <= q_pos, bias, MASK_FILL)

    def put_scores(r, raw, bias):
        sc = raw + bias
        s_ref[rows[r], :] = sc
        mb_ref[rows[r], :] = jnp.max(sc, axis=-1, keepdims=True)

    def take_probs(r):
        m_old = m_ref[rows[r], :]
        m_new = jnp.maximum(m_old, mb_ref[rows[r], :])
        m_ref[rows[r], :] = m_new
        return jnp.exp(s_ref[rows[r], :] - m_new).astype(BF16), jnp.exp(m_old - m_new)

    def accumulate(r, p, alpha, v1):
        acc_ref[rows[r], :] = alpha * acc_ref[rows[r], :] + _dot(p, v1)

    m_ref[...] = jnp.full_like(m_ref, MASK_FILL)
    acc_ref[...] = jnp.zeros_like(acc_ref)
    kb_diag = qi // (tk // blk)
    bias = block_bias(0)
    for r in range(R):
        put_scores(r, _dot_nt(qs[rows[r]], ks_ref[key_rows(0), :]), bias)

    def sel_body(kb, carry):
        k_next = ks_ref[key_rows(kb + 1), :]
        bias_next = block_bias(kb + 1)
        v1 = with_ones(vs_ref[key_rows(kb), :])
        for r in range(R):
            p, alpha = take_probs(r)
            put_scores(r, _dot_nt(qs[rows[r]], k_next), bias_next)
            accumulate(r, p, alpha, v1)
        return carry

    lax.fori_loop(0, kb_diag, sel_body, 0)
    v1 = with_ones(vs_ref[key_rows(kb_diag), :])
    for r in range(R):
        p, alpha = take_probs(r)
        accumulate(r, p, alpha, v1)
    o_sel = acc_ref[:, :HEAD_DIM] / acc_ref[:, HEAD_DIM:]
    o_ref[...] += jnp.concatenate([gate(r, 1) * o_sel[rows[r]] for r in range(R)], axis=1)


def _nsa_attention(tok, k_cmp, v_cmp, gates, ks_col, kw_col, vs_col, vw_col, gate_col,
                   blk=NSA_QUERY_BLOCK, tk=NSA_KEY_BLOCK):
    T = tok.shape[0]
    G = NSA_KV_GROUPS
    blk = min(blk, T)
    tk = min(tk, T)
    assert T % tk == 0 and tk % blk == 0 and WINDOW % blk == 0 and WINDOW + blk <= T
    n_cmp = T // CMP_STRIDE
    n_sel = T // SEL_BLOCK
    cmp_start = np.arange(n_cmp)[:, None] * CMP_STRIDE
    sel_start = np.arange(n_sel)[None, :] * SEL_BLOCK
    overlap = np.clip(np.minimum(cmp_start + CMP_BLOCK, sel_start + SEL_BLOCK)
                      - np.maximum(cmp_start, sel_start), 0, None)
    c2s = jnp.asarray(overlap.astype(np.float32) / CMP_BLOCK, dtype=BF16)
    width = NSA_REP * HEAD_DIM
    tok_spec = lambda col: pl.BlockSpec((T, HEAD_DIM), lambda g, i: (0, col + g))
    cmp_spec = lambda w: pl.BlockSpec((1, n_cmp, w), lambda g, i: (g, 0, 0))
    return pl.pallas_call(
        functools.partial(_nsa_kernel, blk=blk, tk=tk),
        out_shape=jax.ShapeDtypeStruct((T, G * width), F32),
        grid=(G, T // blk),
        in_specs=[pl.BlockSpec((blk, width), lambda g, i: (i, g)),
                  cmp_spec(HEAD_DIM), cmp_spec(HEAD_DIM),
                  tok_spec(ks_col), tok_spec(vs_col), tok_spec(kw_col), tok_spec(vw_col),
                  pl.BlockSpec((blk, LANES), lambda g, i: (i, gate_col + g)),
                  pl.BlockSpec((n_cmp, n_sel), lambda g, i: (0, 0))],
        out_specs=pl.BlockSpec((blk, width), lambda g, i: (i, g)),
        scratch_shapes=[pltpu.VMEM((NSA_REP * blk, 1), F32),
                        pltpu.VMEM((NSA_REP * blk, 1), F32),
                        pltpu.VMEM((NSA_REP * blk, 2 * HEAD_DIM), F32),
                        pltpu.VMEM((NSA_REP * blk, tk), F32)],
        compiler_params=_params(("parallel", "arbitrary")),
        name="nsa_attention",
    )(tok, k_cmp, v_cmp, tok, tok, tok, tok, gates, c2s)


def _cross_kernel(q_ref, kv_ref, o_ref, *, n_heads):
    D = q_ref.shape[1]
    dh = D // n_heads
    scores = [_dot_nt(q_ref[:, h * dh:(h + 1) * dh], kv_ref[:, h * dh:(h + 1) * dh]) for h in range(n_heads)]
    for h, s in enumerate(scores):
        p = jnp.exp(s - jnp.max(s, axis=-1, keepdims=True))
        p = p / jnp.sum(p, axis=-1, keepdims=True)
        o_ref[:, h * dh:(h + 1) * dh] = _dot(p.astype(BF16), kv_ref[:, D + h * dh:D + (h + 1) * dh]).astype(o_ref.dtype)


def _cross_attention(q, kv, n_heads, tm=ROW_TILE):
    T, D = q.shape
    M = kv.shape[0]
    tm = min(tm, T)
    return pl.pallas_call(
        functools.partial(_cross_kernel, n_heads=n_heads),
        out_shape=jax.ShapeDtypeStruct((T, D), BF16),
        grid=(T // tm,),
        in_specs=[pl.BlockSpec((tm, D), lambda i: (i, 0)),
                  pl.BlockSpec((M, 2 * D), lambda i: (0, 0))],
        out_specs=pl.BlockSpec((tm, D), lambda i: (i, 0)),
        compiler_params=_params(("parallel",)),
        name="cross_attention",
    )(q, kv)


def _rope_tables(T):
    half = HEAD_DIM // 2
    inv_freq = ROPE_THETA ** (-np.arange(half, dtype=np.float64) / half)
    ang = np.arange(T, dtype=np.float64)[:, None] * inv_freq[None, :]
    cos, sin = np.cos(ang).astype(np.float32), np.sin(ang).astype(np.float32)
    return jnp.asarray(np.concatenate([cos, cos], axis=1)), jnp.asarray(np.concatenate([-sin, sin], axis=1))


def _pad_to(w, axis, mult):
    pad = -w.shape[axis] % mult
    if not pad:
        return w
    shape = list(w.shape)
    shape[axis] = pad
    return jnp.concatenate([w, jnp.zeros(shape, w.dtype)], axis=axis)


def _ffn(x_f32, x_mm, w_gate, w_up, w_down, ln_g, ln_b, alpha, emit_bf16):
    d_ff = w_gate.shape[1]
    main = d_ff // FF_TILE * FF_TILE
    assert (d_ff - main) % LANES == 0 and main % (FF_DOWN_STEPS * LANES) == 0
    h = _swiglu_up(x_mm, w_gate, w_up, FF_TILE, main // FF_TILE)
    wd = w_down.astype(BF16)
    tail = None
    if main < d_ff:
        tail = (_swiglu_up(x_mm, w_gate[:, main:], w_up[:, main:], d_ff - main, 1), wd[main:])
    return _mm_res_ln(h, wd, x_f32, ln_g, ln_b, alpha, 0.5, emit_bf16, n_k=FF_DOWN_STEPS, tail=tail)


def _mixer(x_f32, x_bf16, w_in, cmp_pos_k, cmp_w1_k, cmp_w2_k, cmp_pos_v, cmp_w1_v, cmp_w2_v,
           mix_norm_g, w_out, ln_g, ln_b, alpha):
    T = x_f32.shape[0]
    G = NSA_KV_GROUPS
    sbw = SB_HEADS * HEAD_DIM
    nqw = NSA_HEADS * HEAD_DIM
    kvw = G * HEAD_DIM
    bounds = np.cumsum([0, sbw, sbw, sbw, nqw, kvw, kvw, kvw, kvw, kvw, kvw, NSA_HEADS * N_GATES])
    w_t = w_in.T.astype(BF16)
    (w_sbq, w_sbk, w_sbv, w_nq, w_kc, w_vc, w_ks, w_vs, w_kw, w_vw, w_gate) = [
        w_t[bounds[i]:bounds[i + 1]] for i in range(11)]
    per_group = NSA_REP * N_GATES
    w_gate = jnp.concatenate([_pad_to(w_gate[g * per_group:(g + 1) * per_group], 0, LANES)
                              for g in range(G)], axis=0)
    tables = _rope_tables(T)
    cat = lambda ws: jnp.concatenate(ws, axis=0)
    q_scale = lambda n_q, n_rest: jnp.concatenate([jnp.full((n_q,), HEAD_DIM ** -0.5, F32), jnp.ones((n_rest,), F32)])
    sb_tok = _project(x_bf16, w_t, BF16, ROW_TILE, 3 * sbw // 2, n_cols=3 * sbw,
                      col_scale=q_scale(sbw, 2 * sbw), w_transposed=True)
    nsa_tok = _project(x_bf16, cat([w_nq, w_ks, w_kw, w_vs, w_vw]), BF16, ROW_TILE, nqw + 4 * kvw,
                       col_scale=q_scale(nqw, 4 * kvw), rope_tables=tables, rope_cols=nqw + 2 * kvw,
                       w_transposed=True)
    cmp_gate = _project(x_bf16, cat([w_kc, w_vc, w_gate]), F32, ROW_TILE, 2 * kvw + G * LANES,
                        rope_tables=tables, rope_cols=kvw, w_transposed=True)

    o_sb = _sb_attention(sb_tok, SB_HEADS, 0, SB_HEADS, 2 * SB_HEADS)

    rep = lambda a, b: jnp.stack([a] * G + [b] * G)
    cmp = _compress(cmp_gate, rep(cmp_pos_k, cmp_pos_v), rep(cmp_w1_k, cmp_w1_v), rep(cmp_w2_k, cmp_w2_v))
    o_nsa = _nsa_attention(nsa_tok, cmp[:G], cmp[G:], cmp_gate, ks_col=NSA_HEADS, kw_col=NSA_HEADS + G,
                           vs_col=NSA_HEADS + 2 * G, vw_col=NSA_HEADS + 3 * G, gate_col=2 * G)
    return _rms_mm_res_ln(o_sb, o_nsa, mix_norm_g, w_out.astype(BF16), x_f32, ln_g, ln_b, alpha)


def _memory_block(x_f32, x_bf16, mem, w_q, w_k, w_v, w_o, ln_g, ln_b, alpha):
    D = w_q.shape[1]
    q = _project(x_bf16, w_q.astype(BF16), BF16, ROW_TILE, D,
                 col_scale=jnp.full((D,), (D // MEM_HEADS) ** -0.5, F32))
    kv = _project(mem.astype(BF16), jnp.concatenate([w_k, w_v], axis=1).astype(BF16), BF16, ROW_TILE, D)
    o = _cross_attention(q, kv, MEM_HEADS)
    return _mm_res_ln(o, w_o.astype(BF16), x_f32, ln_g, ln_b, alpha, 1.0, True)


def kernel(x, mem, ln1_g, ln1_b, ffn1_gate, ffn1_up, ffn1_down, w_in, cmp_pos_k, cmp_w1_k, cmp_w2_k, cmp_pos_v, cmp_w1_v, cmp_w2_v, mix_norm_g, w_out, ln2_g, ln2_b, mem_wq, mem_wk, mem_wv, mem_wo, ln3_g, ln3_b, ffn2_gate, ffn2_up, ffn2_down, ln4_g, ln4_b):
    n_layers = ffn1_gate.shape[0]
    alpha = (2 * n_layers) ** 0.25
    outs = []
    for bi in range(x.shape[0]):
        xf = x[bi]
        xb = xf
        for l in range(n_layers):
            xf, xb = _ffn(xf, xb, ffn1_gate[l], ffn1_up[l], ffn1_down[l], ln1_g[l], ln1_b[l], alpha, True)
            xf, xb = _mixer(xf, xb, w_in[l], cmp_pos_k[l], cmp_w1_k[l], cmp_w2_k[l], cmp_pos_v[l], cmp_w1_v[l],
                            cmp_w2_v[l], mix_norm_g[l], w_out[l], ln2_g[l], ln2_b[l], alpha)
            xf, xb = _memory_block(xf, xb, mem[bi], mem_wq[l], mem_wk[l], mem_wv[l], mem_wo[l], ln3_g[l], ln3_b[l],
                                   alpha)
            xf, xb = _ffn(xf, xb, ffn2_gate[l], ffn2_up[l], ffn2_down[l], ln4_g[l], ln4_b[l], alpha,
                          l + 1 < n_layers)
        outs.append(xf)
    return outs[0][None] if len(outs) == 1 else jnp.stack(outs)
```
